```python
import math
import jax, jax.numpy as jnp
from jax import lax
import numpy as np

D_MODEL = 1024
BATCH = 4
SEQ = 4096
DEPTH = 2
DEC_BATCH = 32
DEC_SEQ = 8
PAST_LEN = 8192
PAGE_SIZE = 128

D_A = D_MODEL // 2
CONV_W = 3
HD_B = 64
D_B = D_MODEL // 2
H_B = D_B // HD_B
R_W = 64
R_A = 64
R_G = 128
N_SHIFT = 3 * D_B + R_W + R_A + R_G
LNX_EPS = 64e-5
H_C = 4
DH_C = 64
Q_W = 2 * H_C * DH_C
K_W = 2 * H_C * DH_C
D_C = H_C * 2 * DH_C
SUBLN_EPS = 1e-5
Q_BLOCK = 128
N_IN = 3 * D_A + N_SHIFT + Q_W + K_W + D_C + 3 * D_MODEL
D_FF = 2816
NORM_EPS = 1e-6

kernel_name = 'hybrid_conv_rwkv7_diffattn_decoder_step'


def rms_norm(x, w, eps=NORM_EPS):
    xf = x.astype(jnp.float32)
    y = xf * lax.rsqrt(jnp.mean(xf * xf, axis=-1, keepdims=True) + eps)
    return (y * w.astype(jnp.float32)).astype(x.dtype)


def swiglu(h, w_gate, w_up, w_down):
    return (jax.nn.silu(h @ w_gate) * (h @ w_up)) @ w_down


def short_conv(xin, gb, gc, conv_w, conv_state):
    u = gc * xin
    ext = jnp.concatenate([conv_state.astype(u.dtype), u], axis=1)
    T = u.shape[1]
    y = ext[:, 0:T] * conv_w[0]
    for j in range(1, CONV_W):
        y = y + ext[:, j:j + T] * conv_w[j]
    return gb * y, ext[:, -(CONV_W - 1):]


def wkv7_scan(r, w, k, v, a, b, s0):
    def step(S, inp):
        r_t, w_t, k_t, v_t, a_t, b_t = inp
        sa = jnp.einsum('bhvk,bhk->bhv', S, a_t)
        S = S * w_t[:, :, None, :] + sa[..., None] * b_t[:, :, None, :] + v_t[..., None] * k_t[:, :, None, :]
        return S, jnp.einsum('bhvk,bhk->bhv', S, r_t)
    xs = (jnp.swapaxes(r, 0, 1), jnp.swapaxes(w, 0, 1), jnp.swapaxes(k, 0, 1),
          jnp.swapaxes(v, 0, 1), jnp.swapaxes(a, 0, 1), jnp.swapaxes(b, 0, 1))
    s_final, y = lax.scan(step, s0, xs)
    return jnp.swapaxes(y, 0, 1), s_final


def rwkv7_mix(z, shift_state, wkv_state, mu, w_up, w0, a_up, a0, g_up, k_k, k_a, r_k, lnx_w, lnx_b):
    B, T, _ = z.shape
    f32 = jnp.float32
    z_prev = jnp.concatenate([shift_state[:, None].astype(z.dtype), z[:, :-1]], axis=1)
    zm = z + (z_prev - z) * mu
    r, k, v, w_lo, a_lo, g_lo = jnp.split(
        zm, [D_B, 2 * D_B, 3 * D_B, 3 * D_B + R_W, 3 * D_B + R_W + R_A], axis=-1)
    heads = lambda t: t.reshape(B, T, H_B, HD_B)
    w_log = -jax.nn.softplus(-(w0 + jnp.tanh(w_lo) @ w_up).astype(f32)) - 0.5
    decay = jnp.exp(-jnp.exp(w_log))
    a = jax.nn.sigmoid(a0 + a_lo @ a_up)
    g = jax.nn.sigmoid(g_lo) @ g_up
    kk = heads(k * k_k).astype(f32)
    kk = kk * lax.rsqrt(jnp.maximum(jnp.sum(kk * kk, axis=-1, keepdims=True), 1e-24))
    k = k * (1 + (a - 1) * k_a)
    rh, kh, vh, ah = heads(r), heads(k), heads(v), heads(a)
    y, s_new = wkv7_scan(rh.astype(f32), heads(decay), kh.astype(f32), vh.astype(f32),
                         -kk, kk * ah.astype(f32), wkv_state.astype(f32))
    mean = jnp.mean(y, axis=-1, keepdims=True)
    var = jnp.mean(jnp.square(y - mean), axis=-1, keepdims=True)
    yn = ((y - mean) * lax.rsqrt(var + LNX_EPS)).reshape(B, T, D_B) * lnx_w + lnx_b
    bonus = (jnp.sum(rh * kh * r_k, axis=-1, keepdims=True) * vh).reshape(B, T, D_B)
    out = (yn.astype(z.dtype) + bonus) * g
    return out, z[:, -1], s_new.astype(wkv_state.dtype)


def alibi_slopes():
    return jnp.asarray([2.0 ** (-8.0 * (h + 1) / H_C) for h in range(H_C)], jnp.float32)


def diff_attn_block(q, k, v, q_pos, k_pos, lam, subln_w, lam_init):
    s = jnp.einsum('bqhd,bkhd->bhqk', q, k).astype(jnp.float32) * (DH_C ** -0.5)
    dist = (q_pos[:, None] - k_pos[None, :]).astype(jnp.float32)
    slopes = jnp.repeat(alibi_slopes(), 2)
    s = s - slopes[:, None, None] * dist
    s = jnp.where(dist >= 0, s, -jnp.inf)
    p = jax.nn.softmax(s, axis=-1)
    B, _, Tq, Tk = p.shape
    p = p.reshape(B, H_C, 2, Tq, Tk)
    attn = p[:, :, 0] - lam * p[:, :, 1]
    o = jnp.einsum('bhqk,bkhe->bqhe', attn.astype(v.dtype), v)
    o = rms_norm(o, subln_w, SUBLN_EPS) * (1.0 - lam_init)
    return o.reshape(B, Tq, D_C)


def prompt_attention(q, k, v, lam, subln_w, lam_init):
    B, T = q.shape[0], q.shape[1]
    nb = T // Q_BLOCK
    k_pos = jnp.arange(T)
    qb = jnp.swapaxes(q.reshape(B, nb, Q_BLOCK, 2 * H_C, DH_C), 0, 1)
    def blk(args):
        i, q_i = args
        q_pos = i * Q_BLOCK + jnp.arange(Q_BLOCK)
        return diff_attn_block(q_i, k, v, q_pos, k_pos, lam, subln_w, lam_init)
    o = lax.map(blk, (jnp.arange(nb), qb))
    return jnp.swapaxes(o, 0, 1).reshape(B, T, D_C)


def sample_attention(q, k_new, v_new, cache_k, cache_v, page_table, l, lam, subln_w, lam_init):
    Bd, Tn = q.shape[0], q.shape[1]
    past_len = page_table.shape[1] * PAGE_SIZE
    k_past = cache_k[l, page_table].reshape(Bd, past_len, 2 * H_C, DH_C)
    v_past = cache_v[l, page_table].reshape(Bd, past_len, H_C, 2 * DH_C)
    k_all = jnp.concatenate([k_past.astype(k_new.dtype), k_new], axis=1)
    v_all = jnp.concatenate([v_past.astype(v_new.dtype), v_new], axis=1)
    q_pos = past_len + jnp.arange(Tn)
    k_pos = jnp.arange(past_len + Tn)
    return diff_attn_block(q, k_all, v_all, q_pos, k_pos, lam, subln_w, lam_init)


def run_trunk(x, W, conv0, shift0, wkv0, attend):
    k_rows, v_rows, wkv_out, shift_out, conv_out = [], [], [], [], []
    B, T = x.shape[0], x.shape[1]
    for l in range(DEPTH):
        x = x + 0.5 * swiglu(rms_norm(x, W['ffn1_norm'][l]), W['ffn1_w_gate'][l], W['ffn1_w_up'][l], W['ffn1_w_down'][l])
        h = rms_norm(x, W['mix_norm'][l])
        proj = h @ W['w_in'][l]
        o1 = D_A; o2 = 2 * D_A; o3 = 3 * D_A; o4 = o3 + N_SHIFT
        o5 = o4 + Q_W; o6 = o5 + K_W; o7 = o6 + D_C
        xin, gb, gc, z, q, k, v, gates = jnp.split(proj, [o1, o2, o3, o4, o5, o6, o7], axis=-1)
        y_a, conv_new = short_conv(xin, gb, gc, W['conv_w'][l], conv0[l])
        y_b, shift_new, wkv_new = rwkv7_mix(
            z, shift0[l], wkv0[l], W['tm_mu'][l], W['w_up_w'][l], W['w0'][l], W['a_up_w'][l], W['a0'][l],
            W['g_up_w'][l], W['k_k'][l], W['k_a'][l], W['r_k'][l], W['lnx_w'][l], W['lnx_b'][l])
        q = q.reshape(B, T, 2 * H_C, DH_C)
        k = k.reshape(B, T, 2 * H_C, DH_C)
        v = v.reshape(B, T, H_C, 2 * DH_C)
        lam_init = 0.8 - 0.6 * math.exp(-0.3 * l)
        lam = (jnp.exp(jnp.sum(W['lam_q1'][l].astype(jnp.float32) * W['lam_k1'][l].astype(jnp.float32)))
               - jnp.exp(jnp.sum(W['lam_q2'][l].astype(jnp.float32) * W['lam_k2'][l].astype(jnp.float32)))
               + lam_init)
        y_c = attend(l, q, k, v, lam, W['subln_w'][l], lam_init)
        g_a, g_b, g_c = jnp.split(jax.nn.sigmoid(gates), 3, axis=-1)
        merged = (g_a * (y_a @ W['w_branch_a'][l]) + g_b * (y_b @ W['w_branch_b'][l])
                  + g_c * (y_c @ W['w_branch_c'][l]))
        x = x + merged @ W['w_out'][l]
        x = x + 0.5 * swiglu(rms_norm(x, W['ffn2_norm'][l]), W['ffn2_w_gate'][l], W['ffn2_w_up'][l], W['ffn2_w_down'][l])
        k_rows.append(k); v_rows.append(v); wkv_out.append(wkv_new)
        shift_out.append(shift_new); conv_out.append(conv_new)
    y = rms_norm(x, W['final_norm'])
    return (y, jnp.stack(k_rows), jnp.stack(v_rows), jnp.stack(wkv_out),
            jnp.stack(shift_out), jnp.stack(conv_out))


def setup_inputs(seed: int = 0) -> dict:
    key = jax.random.key(seed)
    keys = jax.random.split(key, 48)
    ks = iter([keys[i] for i in range(48)])
    def nrm(shape, scale=1.0):
        return jax.random.normal(next(ks), shape, jnp.float32) * scale
    L = DEPTH
    n_pages = PAST_LEN // PAGE_SIZE
    n_pool = (DEC_BATCH * n_pages * 5) // 4
    x_prompt = nrm((BATCH, SEQ, D_MODEL))
    x_sample = nrm((DEC_BATCH, DEC_SEQ, D_MODEL))
    cache_k = nrm((L, n_pool, PAGE_SIZE, 2 * H_C, DH_C))
    cache_v = nrm((L, n_pool, PAGE_SIZE, H_C, 2 * DH_C))
    state_wkv = nrm((L, DEC_BATCH, H_B, HD_B, HD_B), 0.1)
    state_shift = nrm((L, DEC_BATCH, N_SHIFT))
    state_conv = nrm((L, DEC_BATCH, CONV_W - 1, D_A))
    page_table = jax.random.permutation(next(ks), n_pool)[: DEC_BATCH * n_pages].reshape(DEC_BATCH, n_pages).astype(jnp.int32)
    return {
        'x_prompt': x_prompt, 'x_sample': x_sample,
        'cache_k': cache_k, 'cache_v': cache_v,
        'state_wkv': state_wkv, 'state_shift': state_shift, 'state_conv': state_conv,
        'page_table': page_table,
        'ffn1_norm': 1.0 + nrm((L, D_MODEL), 0.02),
        'ffn1_w_gate': nrm((L, D_MODEL, D_FF), D_MODEL ** -0.5),
        'ffn1_w_up': nrm((L, D_MODEL, D_FF), D_MODEL ** -0.5),
        'ffn1_w_down': nrm((L, D_FF, D_MODEL), D_FF ** -0.5),
        'mix_norm': 1.0 + nrm((L, D_MODEL), 0.02),
        'w_in': nrm((L, D_MODEL, N_IN), D_MODEL ** -0.5),
        'conv_w': nrm((L, CONV_W, D_A), CONV_W ** -0.5),
        'tm_mu': jax.random.uniform(next(ks), (L, N_SHIFT), jnp.float32),
        'w_up_w': nrm((L, R_W, D_B), R_W ** -0.5),
        'w0': nrm((L, D_B), 0.5),
        'a_up_w': nrm((L, R_A, D_B), R_A ** -0.5),
        'a0': nrm((L, D_B), 0.1),
        'g_up_w': nrm((L, R_G, D_B), R_G ** -0.5),
        'k_k': 0.85 + nrm((L, D_B), 0.02),
        'k_a': 1.0 + nrm((L, D_B), 0.02),
        'r_k': nrm((L, H_B, HD_B), 0.1),
        'lnx_w': 1.0 + nrm((L, D_B), 0.02),
        'lnx_b': nrm((L, D_B), 0.01),
        'lam_q1': nrm((L, DH_C), 0.1),
        'lam_k1': nrm((L, DH_C), 0.1),
        'lam_q2': nrm((L, DH_C), 0.1),
        'lam_k2': nrm((L, DH_C), 0.1),
        'subln_w': 1.0 + nrm((L, 2 * DH_C), 0.02),
        'w_branch_a': nrm((L, D_A, D_MODEL), D_A ** -0.5),
        'w_branch_b': nrm((L, D_B, D_MODEL), D_B ** -0.5),
        'w_branch_c': nrm((L, D_C, D_MODEL), D_C ** -0.5),
        'w_out': nrm((L, D_MODEL, D_MODEL), D_MODEL ** -0.5),
        'ffn2_norm': 1.0 + nrm((L, D_MODEL), 0.02),
        'ffn2_w_gate': nrm((L, D_MODEL, D_FF), D_MODEL ** -0.5),
        'ffn2_w_up': nrm((L, D_MODEL, D_FF), D_MODEL ** -0.5),
        'ffn2_w_down': nrm((L, D_FF, D_MODEL), D_FF ** -0.5),
        'final_norm': 1.0 + nrm((D_MODEL,), 0.02),
    }


def reference(x_prompt, x_sample, cache_k, cache_v, state_wkv, state_shift, state_conv, page_table,
              ffn1_norm, ffn1_w_gate, ffn1_w_up, ffn1_w_down, mix_norm, w_in, conv_w, tm_mu,
              w_up_w, w0, a_up_w, a0, g_up_w, k_k, k_a, r_k, lnx_w, lnx_b,
              lam_q1, lam_k1, lam_q2, lam_k2, subln_w, w_branch_a, w_branch_b, w_branch_c, w_out,
              ffn2_norm, ffn2_w_gate, ffn2_w_up, ffn2_w_down, final_norm):
    W = dict(ffn1_norm=ffn1_norm, ffn1_w_gate=ffn1_w_gate, ffn1_w_up=ffn1_w_up, ffn1_w_down=ffn1_w_down,
             mix_norm=mix_norm, w_in=w_in, conv_w=conv_w, tm_mu=tm_mu, w_up_w=w_up_w, w0=w0,
             a_up_w=a_up_w, a0=a0, g_up_w=g_up_w, k_k=k_k, k_a=k_a, r_k=r_k, lnx_w=lnx_w, lnx_b=lnx_b,
             lam_q1=lam_q1, lam_k1=lam_k1, lam_q2=lam_q2, lam_k2=lam_k2, subln_w=subln_w,
             w_branch_a=w_branch_a, w_branch_b=w_branch_b, w_branch_c=w_branch_c, w_out=w_out,
             ffn2_norm=ffn2_norm, ffn2_w_gate=ffn2_w_gate, ffn2_w_up=ffn2_w_up, ffn2_w_down=ffn2_w_down,
             final_norm=final_norm)
    bp = x_prompt.shape[0]
    conv0 = jnp.zeros((DEPTH, bp, CONV_W - 1, D_A), x_prompt.dtype)
    shift0 = jnp.zeros((DEPTH, bp, N_SHIFT), x_prompt.dtype)
    wkv0 = jnp.zeros((DEPTH, bp, H_B, HD_B, HD_B), x_prompt.dtype)
    attend_prompt = lambda l, q, k, v, lam, sw, li: prompt_attention(q, k, v, lam, sw, li)
    y_prompt, k_prompt, v_prompt, wkv_prompt, shift_prompt, conv_prompt = run_trunk(
        x_prompt, W, conv0, shift0, wkv0, attend_prompt)
    attend_sample = lambda l, q, k, v, lam, sw, li: sample_attention(q, k, v, cache_k, cache_v, page_table, l, lam, sw, li)
    y_sample, k_sample, v_sample, wkv_sample, shift_sample, conv_sample = run_trunk(
        x_sample, W, state_conv, state_shift, state_wkv, attend_sample)
    return (y_prompt, y_sample, k_prompt, v_prompt, k_sample, v_sample,
            wkv_prompt, wkv_sample, shift_prompt, shift_sample, conv_prompt, conv_sample)
```

```python
import functools
import math

import jax
import jax.numpy as jnp
from jax import lax
from jax.experimental import pallas as pl
from jax.experimental.pallas import tpu as pltpu

F32 = jnp.float32
BF16 = jnp.bfloat16

NORM_EPS = 1e-6
LNX_EPS = 64e-5
SUBLN_EPS = 1e-5
KK_EPS = 1e-24
NEG_BIG = -1e30

V7X_LANES = 128
V7X_SUBLANES = 8
V7X_VMEM_LIMIT_BYTES = 56 * 1024 * 1024

NT_DIMS = (((1,), (1,)), ((), ()))
NN_DIMS = (((1,), (0,)), ((), ()))
TN_DIMS = (((0,), (0,)), ((), ()))


def _params(semantics):
    return pltpu.CompilerParams(dimension_semantics=semantics, vmem_limit_bytes=V7X_VMEM_LIMIT_BYTES)


def _const_spec(shape):
    nd = len(shape)
    return pl.BlockSpec(shape, lambda *_: (0,) * nd, pipeline_mode=pl.Buffered(1))


def _rms(x, w, eps):
    ms = jnp.mean(x * x, axis=-1, keepdims=True)
    return x * lax.rsqrt(ms + eps) * w


def _split_bf16(x, n):
    pieces = []
    rem = x
    for i in range(n):
        p = rem.astype(BF16)
        pieces.append(p)
        if i + 1 < n:
            rem = rem - p.astype(F32)
    return pieces


def _mm(a, b, dims=NN_DIMS, pa=1, pb=1):
    a_p = _split_bf16(a, pa)
    b_p = _split_bf16(b, pb)
    order = max(pa, pb)
    acc = None
    for i, ai in enumerate(a_p):
        for j, bj in enumerate(b_p):
            if i + j >= order:
                continue
            t = lax.dot_general(ai, bj, dims, preferred_element_type=F32)
            acc = t if acc is None else acc + t
    return acc


def _ffn_body(x_ref, nw_ref, wg_ref, wu_ref, wd_ref, *rest, n_chunks, final):
    if final:
        fw_ref, o_ref, acc_ref = rest
    else:
        o_ref, acc_ref = rest
    x = x_ref[...]
    h = _rms(x, nw_ref[...], NORM_EPS).astype(BF16)
    for c in range(n_chunks):
        g = jnp.dot(h, wg_ref[c], preferred_element_type=F32)
        u = jnp.dot(h, wu_ref[c], preferred_element_type=F32)
        act = (g * jax.nn.sigmoid(g) * u).astype(BF16)
        d = jnp.dot(act, wd_ref[c], preferred_element_type=F32)
        if c == 0:
            acc_ref[...] = d
        else:
            acc_ref[...] += d
    y = x + 0.5 * acc_ref[...]
    if final:
        y = _rms(y, fw_ref[...], NORM_EPS)
    o_ref[...] = y


def _ffn(x, norm_w, wg, wu, wd, final_w=None):
    m, d = x.shape
    n_chunks, _, fc = wg.shape
    tm = min(m, 512)
    final = final_w is not None
    in_specs = [
        pl.BlockSpec((tm, d), lambda i: (i, 0)),
        _const_spec((1, d)),
        _const_spec((n_chunks, d, fc)),
        _const_spec((n_chunks, d, fc)),
        _const_spec((n_chunks, fc, d)),
    ]
    args = [x, norm_w, wg, wu, wd]
    if final:
        in_specs.append(_const_spec((1, d)))
        args.append(final_w)
    return pl.pallas_call(
        functools.partial(_ffn_body, n_chunks=n_chunks, final=final),
        grid=(m // tm,),
        in_specs=in_specs,
        out_specs=pl.BlockSpec((tm, d), lambda i: (i, 0)),
        out_shape=jax.ShapeDtypeStruct((m, d), F32),
        scratch_shapes=[pltpu.VMEM((tm, d), F32)],
        compiler_params=_params(("parallel",)),
        name="ffn",
    )(*args)


def _mixin_body(x_ref, nw_ref, w_ref, *out_refs, widths, q_index, q_scale, col_tile):
    h = _rms(x_ref[...], nw_ref[...], NORM_EPS).astype(BF16)
    start = 0
    for idx, (ref, width) in enumerate(zip(out_refs, widths)):
        for c0 in range(0, width, col_tile):
            wd = min(col_tile, width - c0)
            r = jnp.dot(h, w_ref[:, start + c0:start + c0 + wd], preferred_element_type=F32)
            if idx == q_index:
                r = r * q_scale
            ref[:, c0:c0 + wd] = r
        start += width


def _mixin(x, norm_w, w_in, widths, q_index, q_scale):
    m, d = x.shape
    n_in = w_in.shape[1]
    tm = min(m, 512)
    return pl.pallas_call(
        functools.partial(_mixin_body, widths=widths, q_index=q_index, q_scale=q_scale, col_tile=512),
        grid=(m // tm,),
        in_specs=[pl.BlockSpec((tm, d), lambda i: (i, 0)), _const_spec((1, d)), _const_spec((d, n_in))],
        out_specs=[pl.BlockSpec((tm, w), lambda i: (i, 0)) for w in widths],
        out_shape=[jax.ShapeDtypeStruct((m, w), F32) for w in widths],
        compiler_params=_params(("parallel",)),
        name="mixin",
    )(x, norm_w, w_in)


def _conv_body(cin_ref, st_ref, cw_ref, ya_ref, cn_ref, buf, *, tt, da):
    @pl.when(pl.program_id(1) == 0)
    def _():
        buf[pl.ds(6, 2), :] = st_ref[0]

    c = cin_ref[0]
    xin = c[:, 0:da]
    gb = c[:, da:2 * da]
    gc = c[:, 2 * da:3 * da]
    u = gc * xin
    buf[pl.ds(8, tt), :] = u
    cw = cw_ref[...]
    y = buf[pl.ds(6, tt), :] * cw[0:1, :]
    y = y + buf[pl.ds(7, tt), :] * cw[1:2, :]
    y = y + u * cw[2:3, :]
    ya_ref[0] = gb * y
    tail = buf[pl.ds(tt + 6, 2), :]
    cn_ref[0] = tail
    buf[pl.ds(6, 2), :] = tail


def _short_conv(cin, conv_state, conv_w):
    b, t, w3 = cin.shape
    da = w3 // 3
    tt = min(t, 512)
    return pl.pallas_call(
        functools.partial(_conv_body, tt=tt, da=da),
        grid=(b, t // tt),
        in_specs=[
            pl.BlockSpec((1, tt, w3), lambda i, j: (i, j, 0)),
            pl.BlockSpec((1, 2, da), lambda i, j: (i, 0, 0)),
            pl.BlockSpec((3, da), lambda i, j: (0, 0)),
        ],
        out_specs=[
            pl.BlockSpec((1, tt, da), lambda i, j: (i, j, 0)),
            pl.BlockSpec((1, 2, da), lambda i, j: (i, 0, 0)),
        ],
        out_shape=[jax.ShapeDtypeStruct((b, t, da), F32), jax.ShapeDtypeStruct((b, 2, da), F32)],
        scratch_shapes=[pltpu.VMEM((tt + 8, da), F32)],
        compiler_params=_params(("parallel", "arbitrary")),
        name="short_conv",
    )(cin, conv_state, conv_w)


def _pre_body(z_ref, ss_ref, mu_ref, wpad_ref, w0_ref, apad_ref, a0_ref, gup_ref, kk_ref, ka_ref, ones_ref,
              r_ref, lw_ref, k_ref, v_ref, a_ref, b_ref, g_ref, sn_ref, buf, *, tt, db, r_lo, r_g):
    @pl.when(pl.program_id(1) == 0)
    def _():
        buf[pl.ds(7, 1), :] = ss_ref[0]

    z = z_ref[0]
    buf[pl.ds(8, tt), :] = z
    zp = buf[pl.ds(7, tt), :]
    zm = z + (zp - z) * mu_ref[...]
    last = buf[pl.ds(tt + 7, 1), :]
    sn_ref[0] = last
    buf[pl.ds(7, 1), :] = last

    r = zm[:, 0:db]
    k = zm[:, db:2 * db]
    v = zm[:, 2 * db:3 * db]
    lo = zm[:, 3 * db:3 * db + r_lo]
    g_lo = zm[:, 3 * db + r_lo:3 * db + r_lo + r_g]
    w_part = jnp.dot(jnp.tanh(lo).astype(BF16), wpad_ref[...], preferred_element_type=F32)
    a_part = jnp.dot(lo.astype(BF16), apad_ref[...], preferred_element_type=F32)
    xw = -(w0_ref[...] + w_part)
    softplus = jnp.maximum(xw, 0.0) + jnp.log(1.0 + jnp.exp(-jnp.abs(xw)))
    w_log = -softplus - 0.5
    lw = -jnp.exp(w_log)
    a = jax.nn.sigmoid(a0_ref[...] + a_part)
    g = jnp.dot(jax.nn.sigmoid(g_lo).astype(BF16), gup_ref[...], preferred_element_type=F32)
    kk = k * kk_ref[...]
    ss = _mm(kk * kk, ones_ref[...].astype(F32), pa=2, pb=1)
    kkn = kk * lax.rsqrt(jnp.maximum(ss, KK_EPS))
    r_ref[0] = r
    lw_ref[0] = lw
    k_ref[0] = k * (1.0 + (a - 1.0) * ka_ref[...])
    v_ref[0] = v
    a_ref[0] = -kkn
    b_ref[0] = kkn * a
    g_ref[0] = g


def _rwkv_pre(z, shift_state, mu, wpad, w0, apad, a0, gup, k_k, k_a, ones_bd, db):
    b, t, ns = z.shape
    r_lo = wpad.shape[0]
    r_g = gup.shape[0]
    tt = min(t, 512)
    row = lambda n: pl.BlockSpec((1, n), lambda i, j: (0, 0))
    full = lambda s: pl.BlockSpec(s, lambda i, j: (0, 0))
    seq = lambda n: pl.BlockSpec((1, tt, n), lambda i, j: (i, j, 0))
    return pl.pallas_call(
        functools.partial(_pre_body, tt=tt, db=db, r_lo=r_lo, r_g=r_g),
        grid=(b, t // tt),
        in_specs=[
            seq(ns),
            pl.BlockSpec((1, 1, ns), lambda i, j: (i, 0, 0)),
            row(ns), full((r_lo, db)), row(db), full((r_lo, db)), row(db), full((r_g, db)),
            row(db), row(db), full((db, db)),
        ],
        out_specs=[seq(db)] * 7 + [pl.BlockSpec((1, 1, ns), lambda i, j: (i, 0, 0))],
        out_shape=[jax.ShapeDtypeStruct((b, t, db), F32)] * 7 + [jax.ShapeDtypeStruct((b, 1, ns), F32)],
        scratch_shapes=[pltpu.VMEM((tt + 8, ns), F32)],
        compiler_params=_params(("parallel", "arbitrary")),
        name="rwkv_pre",
    )(z, shift_state, mu, wpad, w0, apad, a0, gup, k_k, k_a, ones_bd)


SCAN_SUB = 16
SCAN_P = 2


def _neumann_inverse(l_mat, n):
    c = l_mat.shape[0]
    eye = (lax.broadcasted_iota(jnp.int32, (c, c), 0) == lax.broadcasted_iota(jnp.int32, (c, c), 1)).astype(F32)
    x = eye + l_mat
    p = l_mat
    span = 2
    while span < n:
        p = _mm(p, p, pa=SCAN_P, pb=SCAN_P)
        x = x + _mm(x, p, pa=SCAN_P, pb=SCAN_P)
        span *= 2
    return x


def _unit_lower_inverse(l_mat):
    c = l_mat.shape[0]
    if c <= SCAN_SUB:
        return _neumann_inverse(l_mat, c)
    rows = lax.broadcasted_iota(jnp.int32, (c, c), 0) // SCAN_SUB
    cols = lax.broadcasted_iota(jnp.int32, (c, c), 1) // SCAN_SUB
    same = rows == cols
    l_diag = jnp.where(same, l_mat, 0.0)
    l_off = jnp.where(same, 0.0, l_mat)
    t_diag = _neumann_inverse(l_diag, SCAN_SUB)
    z = _mm(t_diag, l_off, pa=SCAN_P, pb=SCAN_P)
    out = t_diag
    span = 1
    nblk = c // SCAN_SUB
    zp = z
    factors = []
    while span < nblk:
        factors.append(zp)
        span *= 2
        if span < nblk:
            zp = _mm(zp, zp, pa=SCAN_P, pb=SCAN_P)
    for f in reversed(factors):
        out = out + _mm(f, out, pa=SCAN_P, pb=SCAN_P)
    return out


def _scan_body(r_ref, lw_ref, k_ref, v_ref, a_ref, b_ref, g_ref, s0_ref, rk_ref, lnw_ref, lnb_ref,
               o_ref, sT_ref, st_ref, *, c, nh, hd):
    ci = pl.program_id(1)

    @pl.when(ci == 0)
    def _():
        for h in range(nh):
            st_ref[h] = s0_ref[0, h].T

    r = r_ref[0]
    lw = lw_ref[0]
    k = k_ref[0]
    v = v_ref[0]
    a = a_ref[0]
    b = b_ref[0]
    rows = lax.broadcasted_iota(jnp.int32, (c, c), 0)
    cols = lax.broadcasted_iota(jnp.int32, (c, c), 1)
    incl = rows >= cols
    strict = rows > cols
    cum = _mm(incl.astype(F32), lw, pa=1, pb=3)
    cum_last = cum[c - 1:c, :]
    e_pos = jnp.exp(cum)
    e_prev = jnp.exp(cum - lw)
    e_neg = jnp.exp(-cum)
    e_rem = jnp.exp(cum_last - cum)
    at = a * e_prev
    rt = r * e_pos
    bt = b * e_neg
    kt = k * e_neg
    bh = b * e_rem
    kh = k * e_rem
    cum_cols = _mm(lw, jnp.ones((c, hd), F32), dims=TN_DIMS, pa=3, pb=1)
    rows2 = lax.broadcasted_iota(jnp.int32, (c, 2 * c), 0)
    cols2 = lax.broadcasted_iota(jnp.int32, (c, 2 * c), 1)
    incl2 = rows2 >= jnp.where(cols2 >= c, cols2 - c, cols2)
    rk = rk_ref[...]
    outs = []
    for h in range(nh):
        sl = slice(h * hd, (h + 1) * hd)
        at_h, rt_h, bt_h, kt_h, v_h = at[:, sl], rt[:, sl], bt[:, sl], kt[:, sl], v[:, sl]
        st = st_ref[h]
        l_ab = jnp.where(strict, _mm(at_h, bt_h, dims=NT_DIMS, pa=SCAN_P, pb=SCAN_P), 0.0)
        l_ak = jnp.where(strict, _mm(at_h, kt_h, dims=NT_DIMS, pa=SCAN_P, pb=SCAN_P), 0.0)
        t_inv = _unit_lower_inverse(l_ab)
        gmat = _mm(at_h, st, pa=SCAN_P, pb=SCAN_P) + _mm(l_ak, v_h, pa=SCAN_P, pb=SCAN_P)
        u = _mm(t_inv, gmat, pa=SCAN_P, pb=SCAN_P)
        uv = jnp.concatenate([u, v_h], axis=0)
        bk_t = jnp.concatenate([bt_h, kt_h], axis=0)
        rbk = jnp.where(incl2, _mm(rt_h, bk_t, dims=NT_DIMS, pa=SCAN_P, pb=SCAN_P), 0.0)
        y = _mm(rt_h, st, pa=SCAN_P, pb=SCAN_P) + _mm(rbk, uv, pa=SCAN_P, pb=SCAN_P)
        bk_h = jnp.concatenate([bh[:, sl], kh[:, sl]], axis=0)
        decay = jnp.exp(cum_cols[h * hd:(h + 1) * hd, :])
        st_ref[h] = st * decay + _mm(bk_h, uv, dims=TN_DIMS, pa=SCAN_P, pb=SCAN_P)
        mean = jnp.mean(y, axis=-1, keepdims=True)
        yc = y - mean
        var = jnp.mean(yc * yc, axis=-1, keepdims=True)
        yn = yc * lax.rsqrt(var + LNX_EPS)
        bonus = jnp.sum(r[:, sl] * k[:, sl] * rk[:, sl], axis=-1, keepdims=True) * v_h
        outs.append((yn, bonus))
    yn_all = jnp.concatenate([o[0] for o in outs], axis=1)
    bonus_all = jnp.concatenate([o[1] for o in outs], axis=1)
    o_ref[0] = (yn_all * lnw_ref[...] + lnb_ref[...] + bonus_all) * g_ref[0]

    @pl.when(ci == pl.num_programs(1) - 1)
    def _():
        for h in range(nh):
            sT_ref[0, h] = st_ref[h].T


def _rwkv_scan(r, lw, k, v, a, b, g, s0, r_k, lnx_w, lnx_b):
    bsz, t, db = r.shape
    _, nh, hd, _ = s0.shape
    c = min(t, 64)
    seq = pl.BlockSpec((1, c, db), lambda i, j: (i, j, 0))
    row = pl.BlockSpec((1, db), lambda i, j: (0, 0))
    st_spec = pl.BlockSpec((1, nh, hd, hd), lambda i, j: (i, 0, 0, 0))
    return pl.pallas_call(
        functools.partial(_scan_body, c=c, nh=nh, hd=hd),
        grid=(bsz, t // c),
        in_specs=[seq] * 7 + [st_spec, row, row, row],
        out_specs=[seq, st_spec],
        out_shape=[jax.ShapeDtypeStruct((bsz, t, db), F32), jax.ShapeDtypeStruct((bsz, nh, hd, hd), F32)],
        scratch_shapes=[pltpu.VMEM((nh, hd, hd), F32)],
        compiler_params=_params(("parallel", "arbitrary")),
        name="rwkv_scan",
    )(r, lw, k, v, a, b, g, s0, r_k, lnx_w, lnx_b)


def _lam_value(lq1, lk1, lq2, lk2, lam_init):
    s1 = jnp.sum(lq1 * lk1, axis=-1, keepdims=True)
    s2 = jnp.sum(lq2 * lk2, axis=-1, keepdims=True)
    return jnp.exp(s1) - jnp.exp(s2) + lam_init


def _split_maps(q, dh):
    lane = lax.broadcasted_iota(jnp.int32, q.shape, 1)
    q1 = jnp.where(lane < dh, q, 0.0).astype(BF16)
    q2 = jnp.where(lane >= dh, q, 0.0).astype(BF16)
    return q1, q2


def _stack_maps(q, tn, dh):
    q2x = jnp.concatenate([q, q], axis=0)
    row = lax.broadcasted_iota(jnp.int32, q2x.shape, 0)
    lane = lax.broadcasted_iota(jnp.int32, q2x.shape, 1)
    keep = (row < tn) == (lane < dh)
    return jnp.where(keep, q2x, 0.0).astype(BF16)


def _online_update(s, v_bf, m_ref, l_ref, acc_ref):
    m_old = m_ref[...]
    m_new = jnp.maximum(m_old, jnp.max(s, axis=-1, keepdims=True))
    alpha = jnp.exp(m_old - m_new)
    p = jnp.exp(s - m_new)
    l_ref[...] = alpha * l_ref[...] + jnp.sum(p, axis=-1, keepdims=True)
    acc_ref[...] = alpha * acc_ref[...] + jnp.dot(p.astype(BF16), v_bf, preferred_element_type=F32)
    m_ref[...] = m_new


def _diff_finish(o1, o2, lam, subln_w, lam_init):
    o = o1 - lam * o2
    return _rms(o, subln_w, SUBLN_EPS) * (1.0 - lam_init)


def _pattn_body(q_ref, k_ref, v_ref, slope_ref, lq1_ref, lk1_ref, lq2_ref, lk2_ref, sw_ref, o_ref,
                m1, l1, acc1, m2, l2, acc2, *, tq, dh, lam_init):
    qi = pl.program_id(2)
    ki = pl.program_id(3)

    @pl.when(ki == 0)
    def _():
        for m_ref, l_ref, acc_ref in ((m1, l1, acc1), (m2, l2, acc2)):
            m_ref[...] = jnp.full(m_ref.shape, NEG_BIG, F32)
            l_ref[...] = jnp.zeros(l_ref.shape, F32)
            acc_ref[...] = jnp.zeros(acc_ref.shape, F32)

    @pl.when(ki <= qi)
    def _():
        q1, q2 = _split_maps(q_ref[0], dh)
        k_bf = k_ref[0].astype(BF16)
        v_bf = v_ref[0].astype(BF16)
        rel = lax.broadcasted_iota(jnp.int32, (tq, tq), 0) - lax.broadcasted_iota(jnp.int32, (tq, tq), 1)
        dist = (rel + (qi - ki) * tq).astype(F32)
        bias = jnp.where(dist >= 0.0, -slope_ref[0][:, 0:1] * dist, NEG_BIG)
        s1 = lax.dot_general(q1, k_bf, NT_DIMS, preferred_element_type=F32) + bias
        _online_update(s1, v_bf, m1, l1, acc1)
        s2 = lax.dot_general(q2, k_bf, NT_DIMS, preferred_element_type=F32) + bias
        _online_update(s2, v_bf, m2, l2, acc2)

    @pl.when(ki == qi)
    def _():
        lam = _lam_value(lq1_ref[...], lk1_ref[...], lq2_ref[...], lk2_ref[...], lam_init)
        o_ref[0] = _diff_finish(acc1[...] / l1[...], acc2[...] / l2[...], lam, sw_ref[...], lam_init)


def _prompt_attention(q, k, v, slopes, lq1, lk1, lq2, lk2, subln_w, lam_init, n_heads):
    b, t, w = q.shape
    hw = w // n_heads
    dh = hw // 2
    tq = min(t, 512)
    nq = t // tq
    qspec = pl.BlockSpec((1, tq, hw), lambda bi, h, qi, ki: (bi, qi, h))
    kspec = pl.BlockSpec((1, tq, hw), lambda bi, h, qi, ki: (bi, jnp.minimum(ki, qi), h))
    small = lambda n: pl.BlockSpec((1, n), lambda bi, h, qi, ki: (0, 0))
    return pl.pallas_call(
        functools.partial(_pattn_body, tq=tq, dh=dh, lam_init=lam_init),
        grid=(b, n_heads, nq, nq),
        in_specs=[qspec, kspec, kspec,
                  pl.BlockSpec((1, 1, V7X_LANES), lambda bi, h, qi, ki: (h, 0, 0)),
                  small(dh), small(dh), small(dh), small(dh), small(hw)],
        out_specs=qspec,
        out_shape=jax.ShapeDtypeStruct((b, t, w), F32),
        scratch_shapes=[pltpu.VMEM((tq, 1), F32), pltpu.VMEM((tq, 1), F32), pltpu.VMEM((tq, hw), F32),
                        pltpu.VMEM((tq, 1), F32), pltpu.VMEM((tq, 1), F32), pltpu.VMEM((tq, hw), F32)],
        compiler_params=_params(("parallel", "parallel", "parallel", "arbitrary")),
        name="prompt_attention",
    )(q, k, v, slopes, lq1, lk1, lq2, lk2, subln_w)


def _sattn_body(pt_ref, q_ref, kn_ref, vn_ref, slope_ref, lq1_ref, lk1_ref, lq2_ref, lk2_ref, sw_ref, *rest,
                ppb, page, n_heads, tn, dh, past_len, lam_init):
    k_refs = rest[:ppb]
    v_refs = rest[ppb:2 * ppb]
    o_ref, m_s, l_s, acc_s = rest[2 * ppb:]
    del pt_ref
    si = pl.program_id(1)
    hw = 2 * dh

    @pl.when(si == 0)
    def _():
        m_s[...] = jnp.full(m_s.shape, NEG_BIG, F32)
        l_s[...] = jnp.zeros(l_s.shape, F32)
        acc_s[...] = jnp.zeros(acc_s.shape, F32)

    q = q_ref[0]
    row = lax.broadcasted_iota(jnp.int32, (2 * tn, page), 0)
    qrow = jnp.where(row >= tn, row - tn, row)
    kcol = lax.broadcasted_iota(jnp.int32, (2 * tn, page), 1)
    q12s = [_stack_maps(q[:, h * hw:(h + 1) * hw], tn, dh) for h in range(n_heads)]
    for h in range(n_heads):
        sl = slice(h * hw, (h + 1) * hw)
        q12 = q12s[h]
        slope = slope_ref[h][:, 0:1]
        s_parts = []
        for j in range(ppb):
            k_bf = k_refs[j][:, sl].astype(BF16)
            s = lax.dot_general(q12, k_bf, NT_DIMS, preferred_element_type=F32)
            dist = (past_len + qrow - ((si * ppb + j) * page + kcol)).astype(F32)
            s_parts.append(s - slope * dist)
        s_all = jnp.concatenate(s_parts, axis=1)
        m_old = m_s[h]
        m_new = jnp.maximum(m_old, jnp.max(s_all, axis=-1, keepdims=True))
        alpha = jnp.exp(m_old - m_new)
        p = jnp.exp(s_all - m_new)
        l_s[h] = alpha * l_s[h] + jnp.sum(p, axis=-1, keepdims=True)
        pv = None
        for j in range(ppb):
            v_bf = v_refs[j][:, sl].astype(BF16)
            t = jnp.dot(p[:, j * page:(j + 1) * page].astype(BF16), v_bf, preferred_element_type=F32)
            pv = t if pv is None else pv + t
        acc_s[h] = alpha * acc_s[h] + pv
        m_s[h] = m_new

    @pl.when(si == pl.num_programs(1) - 1)
    def _():
        lam = _lam_value(lq1_ref[...], lk1_ref[...], lq2_ref[...], lk2_ref[...], lam_init)
        kn = kn_ref[0]
        vn = vn_ref[0]
        row_n = lax.broadcasted_iota(jnp.int32, (2 * tn, tn), 0)
        rel = (jnp.where(row_n >= tn, row_n - tn, row_n)
               - lax.broadcasted_iota(jnp.int32, (2 * tn, tn), 1)).astype(F32)
        outs = []
        for h in range(n_heads):
            sl = slice(h * hw, (h + 1) * hw)
            q12 = q12s[h]
            slope = slope_ref[h][:, 0:1]
            s = lax.dot_general(q12, kn[:, sl].astype(BF16), NT_DIMS, preferred_element_type=F32)
            s = jnp.where(rel >= 0.0, s - slope * rel, NEG_BIG)
            m_old = m_s[h]
            m_new = jnp.maximum(m_old, jnp.max(s, axis=-1, keepdims=True))
            alpha = jnp.exp(m_old - m_new)
            p = jnp.exp(s - m_new)
            l_fin = alpha * l_s[h] + jnp.sum(p, axis=-1, keepdims=True)
            acc = alpha * acc_s[h] + jnp.dot(p.astype(BF16), vn[:, sl].astype(BF16), preferred_element_type=F32)
            o = acc / l_fin
            outs.append(_diff_finish(o[0:tn], o[tn:2 * tn], lam, sw_ref[...], lam_init))
        o_ref[0] = jnp.concatenate(outs, axis=1)


def _sample_attention(q, k_new, v_new, cache_k, cache_v, page_table, layer, slopes, lq1, lk1, lq2, lk2, subln_w,
                      lam_init, n_heads):
    b, tn, w = q.shape
    page = cache_k.shape[2]
    n_pages = page_table.shape[1]
    hw = w // n_heads
    dh = hw // 2
    ppb = math.gcd(n_pages, 8)
    n_steps = n_pages // ppb
    pt_flat = page_table.reshape(-1)

    def page_spec(j):
        return pl.BlockSpec((None, None, page, w),
                            lambda bi, si, pt: (layer, pt[bi * n_pages + si * ppb + j], 0, 0))

    new_spec = pl.BlockSpec((1, tn, w), lambda bi, si, pt: (bi, 0, 0))
    small = lambda n: pl.BlockSpec((1, n), lambda bi, si, pt: (0, 0))
    grid_spec = pltpu.PrefetchScalarGridSpec(
        num_scalar_prefetch=1,
        grid=(b, n_steps),
        in_specs=[new_spec, new_spec, new_spec,
                  pl.BlockSpec((n_heads, 1, V7X_LANES), lambda bi, si, pt: (0, 0, 0)),
                  small(dh), small(dh), small(dh), small(dh), small(hw)]
                 + [page_spec(j) for j in range(ppb)] + [page_spec(j) for j in range(ppb)],
        out_specs=new_spec,
        scratch_shapes=[pltpu.VMEM((n_heads, 2 * tn, 1), F32), pltpu.VMEM((n_heads, 2 * tn, 1), F32),
                        pltpu.VMEM((n_heads, 2 * tn, hw), F32)],
    )
    return pl.pallas_call(
        functools.partial(_sattn_body, ppb=ppb, page=page, n_heads=n_heads, tn=tn, dh=dh,
                          past_len=n_pages * page, lam_init=lam_init),
        grid_spec=grid_spec,
        out_shape=jax.ShapeDtypeStruct((b, tn, w), F32),
        compiler_params=_params(("parallel", "arbitrary")),
        name="sample_attention",
    )(pt_flat, q, k_new, v_new, slopes, lq1, lk1, lq2, lk2, subln_w,
      *([cache_k] * ppb), *([cache_v] * ppb))


def _merge_body(x_ref, ya_ref, yb_ref, yc_ref, gt_ref, wa_ref, wb_ref, wc_ref, wo_ref, o_ref, *, d):
    gates = gt_ref[...]
    merged = jax.nn.sigmoid(gates[:, 0:d]) * jnp.dot(ya_ref[...].astype(BF16), wa_ref[...], preferred_element_type=F32)
    merged = merged + jax.nn.sigmoid(gates[:, d:2 * d]) * jnp.dot(
        yb_ref[...].astype(BF16), wb_ref[...], preferred_element_type=F32)
    merged = merged + jax.nn.sigmoid(gates[:, 2 * d:3 * d]) * jnp.dot(
        yc_ref[...].astype(BF16), wc_ref[...], preferred_element_type=F32)
    o_ref[...] = x_ref[...] + jnp.dot(merged.astype(BF16), wo_ref[...], preferred_element_type=F32)


def _merge(x, ya, yb, yc, gates, wa, wb, wc, wo):
    m, d = x.shape
    tm = min(m, 512)
    tile = lambda n: pl.BlockSpec((tm, n), lambda i: (i, 0))
    return pl.pallas_call(
        functools.partial(_merge_body, d=d),
        grid=(m // tm,),
        in_specs=[tile(d), tile(ya.shape[1]), tile(yb.shape[1]), tile(yc.shape[1]), tile(3 * d),
                  _const_spec(wa.shape), _const_spec(wb.shape), _const_spec(wc.shape), _const_spec(wo.shape)],
        out_specs=tile(d),
        out_shape=jax.ShapeDtypeStruct((m, d), F32),
        compiler_params=_params(("parallel",)),
        name="merge",
    )(x, ya, yb, yc, gates, wa, wb, wc, wo)


def _prep_layer_weights(l, W, dims):
    d, d_ff, db, r_w, r_a = dims["d"], dims["d_ff"], dims["db"], dims["r_w"], dims["r_a"]
    fc = 256 if d_ff % 256 == 0 else V7X_LANES
    nck = d_ff // fc
    row = lambda a: a.reshape(1, -1)

    def ffn(prefix):
        wg = W[prefix + "_w_gate"][l].astype(BF16).reshape(d, nck, fc).transpose(1, 0, 2)
        wu = W[prefix + "_w_up"][l].astype(BF16).reshape(d, nck, fc).transpose(1, 0, 2)
        wd = W[prefix + "_w_down"][l].astype(BF16).reshape(nck, fc, d)
        return row(W[prefix + "_norm"][l]), wg, wu, wd

    zeros_w = jnp.zeros((r_a, db), BF16)
    zeros_a = jnp.zeros((r_w, db), BF16)
    head = jnp.arange(db) // dims["hd"]
    return dict(
        ffn1=ffn("ffn1"), ffn2=ffn("ffn2"),
        mix_norm=row(W["mix_norm"][l]), w_in=W["w_in"][l].astype(BF16),
        conv_w=W["conv_w"][l],
        mu=row(W["tm_mu"][l]),
        wpad=jnp.concatenate([W["w_up_w"][l].astype(BF16), zeros_w], axis=0),
        apad=jnp.concatenate([zeros_a, W["a_up_w"][l].astype(BF16)], axis=0),
        w0=row(W["w0"][l]), a0=row(W["a0"][l]), gup=W["g_up_w"][l].astype(BF16),
        k_k=row(W["k_k"][l]), k_a=row(W["k_a"][l]), r_k=row(W["r_k"][l]),
        lnx_w=row(W["lnx_w"][l]), lnx_b=row(W["lnx_b"][l]),
        ones_bd=(head[:, None] == head[None, :]).astype(BF16),
        lq1=row(W["lam_q1"][l]), lk1=row(W["lam_k1"][l]), lq2=row(W["lam_q2"][l]), lk2=row(W["lam_k2"][l]),
        subln_w=row(W["subln_w"][l]),
        wa=W["w_branch_a"][l].astype(BF16), wb=W["w_branch_b"][l].astype(BF16),
        wc=W["w_branch_c"][l].astype(BF16), wo=W["w_out"][l].astype(BF16),
    )


def _run_trunk(x, layers, final_w, conv0, shift0, wkv0, attend, dims):
    b, t, d = x.shape
    m = b * t
    da, db, ns = dims["da"], dims["db"], dims["ns"]
    qw, kw, dc = dims["qw"], dims["kw"], dims["dc"]
    widths = (3 * da, ns, qw, kw, dc, 3 * d)
    depth = len(layers)
    xf = x.reshape(m, d)
    k_rows, v_rows, wkv_out, shift_out, conv_out = [], [], [], [], []
    for l, lw in enumerate(layers):
        xf = _ffn(xf, *lw["ffn1"])
        cin, z, q, k, v, gates = _mixin(xf, lw["mix_norm"], lw["w_in"], widths, 2, dims["dh"] ** -0.5)
        y_a, conv_new = _short_conv(cin.reshape(b, t, 3 * da), conv0[l], lw["conv_w"])
        r_, lw_, k_, v_, a_, b_, g_, shift_new = _rwkv_pre(
            z.reshape(b, t, ns), shift0[l].reshape(b, 1, ns), lw["mu"], lw["wpad"], lw["w0"], lw["apad"], lw["a0"],
            lw["gup"], lw["k_k"], lw["k_a"], lw["ones_bd"], db)
        y_b, wkv_new = _rwkv_scan(r_, lw_, k_, v_, a_, b_, g_, wkv0[l], lw["r_k"], lw["lnx_w"], lw["lnx_b"])
        lam_init = 0.8 - 0.6 * math.exp(-0.3 * l)
        y_c = attend(l, q.reshape(b, t, qw), k.reshape(b, t, kw), v.reshape(b, t, dc), lw, lam_init)
        xf = _merge(xf, y_a.reshape(m, da), y_b.reshape(m, db), y_c.reshape(m, dc), gates,
                    lw["wa"], lw["wb"], lw["wc"], lw["wo"])
        xf = _ffn(xf, *lw["ffn2"], final_w=final_w if l == depth - 1 else None)
        k_rows.append(k.reshape(b, t, 2 * dims["hc"], dims["dh"]))
        v_rows.append(v.reshape(b, t, dims["hc"], 2 * dims["dh"]))
        wkv_out.append(wkv_new)
        shift_out.append(shift_new.reshape(b, ns))
        conv_out.append(conv_new)
    return (xf.reshape(b, t, d), jnp.stack(k_rows), jnp.stack(v_rows), jnp.stack(wkv_out),
            jnp.stack(shift_out), jnp.stack(conv_out))


def kernel(x_prompt, x_sample, cache_k, cache_v, state_wkv, state_shift, state_conv, page_table, ffn1_norm, ffn1_w_gate, ffn1_w_up, ffn1_w_down, mix_norm, w_in, conv_w, tm_mu, w_up_w, w0, a_up_w, a0, g_up_w, k_k, k_a, r_k, lnx_w, lnx_b, lam_q1, lam_k1, lam_q2, lam_k2, subln_w, w_branch_a, w_branch_b, w_branch_c, w_out, ffn2_norm, ffn2_w_gate, ffn2_w_up, ffn2_w_down, final_norm):
    W = dict(ffn1_norm=ffn1_norm, ffn1_w_gate=ffn1_w_gate, ffn1_w_up=ffn1_w_up, ffn1_w_down=ffn1_w_down,
             mix_norm=mix_norm, w_in=w_in, conv_w=conv_w, tm_mu=tm_mu, w_up_w=w_up_w, w0=w0,
             a_up_w=a_up_w, a0=a0, g_up_w=g_up_w, k_k=k_k, k_a=k_a, r_k=r_k, lnx_w=lnx_w, lnx_b=lnx_b,
             lam_q1=lam_q1, lam_k1=lam_k1, lam_q2=lam_q2, lam_k2=lam_k2, subln_w=subln_w,
             w_branch_a=w_branch_a, w_branch_b=w_branch_b, w_branch_c=w_branch_c, w_out=w_out,
             ffn2_norm=ffn2_norm, ffn2_w_gate=ffn2_w_gate, ffn2_w_up=ffn2_w_up, ffn2_w_down=ffn2_w_down)
    depth = w_in.shape[0]
    d = x_prompt.shape[-1]
    _, n_pool, page, maps, dh = cache_k.shape
    hc = maps // 2
    nh, hd = r_k.shape[1], r_k.shape[2]
    db = nh * hd
    da = conv_w.shape[-1]
    assert conv_w.shape[1] == 3, "short conv kernel is written for width 3"
    r_w, r_a, r_g = w_up_w.shape[1], a_up_w.shape[1], g_up_w.shape[1]
    ns = state_shift.shape[-1]
    assert ns == 3 * db + r_w + r_a + r_g
    assert (r_w + r_a) % V7X_LANES == 0 and r_g % V7X_LANES == 0
    dims = dict(d=d, d_ff=ffn1_w_gate.shape[-1], da=da, db=db, hd=hd, ns=ns, r_w=r_w, r_a=r_a,
                qw=maps * dh, kw=maps * dh, dc=hc * 2 * dh, hc=hc, dh=dh)
    layers = [_prep_layer_weights(l, W, dims) for l in range(depth)]
    final_w = final_norm.reshape(1, d)
    slopes = jnp.broadcast_to(
        jnp.asarray([2.0 ** (-8.0 * (h + 1) / hc) for h in range(hc)], F32)[:, None, None], (hc, 1, V7X_LANES))
    cache_k2 = cache_k.reshape(depth, n_pool, page, maps * dh)
    cache_v2 = cache_v.reshape(depth, n_pool, page, hc * 2 * dh)

    def attend_prompt(l, q, k, v, lw, lam_init):
        return _prompt_attention(q, k, v, slopes, lw["lq1"], lw["lk1"], lw["lq2"], lw["lk2"], lw["subln_w"],
                                 lam_init, hc)

    def attend_sample(l, q, k, v, lw, lam_init):
        return _sample_attention(q, k, v, cache_k2, cache_v2, page_table, l, slopes, lw["lq1"], lw["lk1"],
                                 lw["lq2"], lw["lk2"], lw["subln_w"], lam_init, hc)

    bp = x_prompt.shape[0]
    conv0 = jnp.zeros((depth, bp, 2, da), F32)
    shift0 = jnp.zeros((depth, bp, ns), F32)
    wkv0 = jnp.zeros((depth, bp, nh, hd, hd), F32)
    y_p, k_p, v_p, wkv_p, shift_p, conv_p = _run_trunk(
        x_prompt, layers, final_w, conv0, shift0, wkv0, attend_prompt, dims)
    y_s, k_s, v_s, wkv_s, shift_s, conv_s = _run_trunk(
        x_sample, layers, final_w, state_conv, state_shift, state_wkv, attend_sample, dims)
    return (y_p, y_s, k_p, v_p, k_s, v_s, wkv_p, wkv_s, shift_p, shift_s, conv_p, conv_s)
```

```python
import functools
import math

import jax
import jax.numpy as jnp
from jax import lax
from jax.experimental import pallas as pl
from jax.experimental.pallas import tpu as pltpu

F32 = jnp.float32
BF16 = jnp.bfloat16

NORM_EPS = 1e-6
LNX_EPS = 64e-5
SUBLN_EPS = 1e-5
KK_EPS = 1e-24
NEG_BIG = -1e30
LOG2_E = math.log2(math.e)

V7X_LANES = 128
V7X_SUBLANES = 8
V7X_VMEM_LIMIT_BYTES = 56 * 1024 * 1024

NT_DIMS = (((1,), (1,)), ((), ()))
NN_DIMS = (((1,), (0,)), ((), ()))
TN_DIMS = (((0,), (0,)), ((), ()))


def _params(semantics):
    return pltpu.CompilerParams(dimension_semantics=semantics, vmem_limit_bytes=V7X_VMEM_LIMIT_BYTES)


def _const_spec(shape):
    nd = len(shape)
    return pl.BlockSpec(shape, lambda *_: (0,) * nd, pipeline_mode=pl.Buffered(1))


def _rms(x, w, eps):
    ms = jnp.mean(x * x, axis=-1, keepdims=True)
    return x * lax.rsqrt(ms + eps) * w


def _split_bf16(x, n):
    pieces = []
    rem = x
    for i in range(n):
        p = rem.astype(BF16)
        pieces.append(p)
        if i + 1 < n:
            rem = rem - p.astype(F32)
    return pieces


def _mm(a, b, dims=NN_DIMS, pa=1, pb=1):
    a_p = _split_bf16(a, pa)
    b_p = _split_bf16(b, pb)
    order = max(pa, pb)
    acc = None
    for i, ai in enumerate(a_p):
        for j, bj in enumerate(b_p):
            if i + j >= order:
                continue
            t = lax.dot_general(ai, bj, dims, preferred_element_type=F32)
            acc = t if acc is None else acc + t
    return acc


def _ffn_body(x_ref, nw_ref, wg_ref, wu_ref, wd_ref, *rest, n_chunks, final):
    if final:
        fw_ref, o_ref, acc_ref = rest
    else:
        o_ref, acc_ref = rest
    x = x_ref[...]
    h = _rms(x, nw_ref[...], NORM_EPS).astype(BF16)
    for c in range(n_chunks):
        g = jnp.dot(h, wg_ref[c], preferred_element_type=F32)
        u = jnp.dot(h, wu_ref[c], preferred_element_type=F32)
        act = (g * jax.nn.sigmoid(g) * u).astype(BF16)
        d = jnp.dot(act, wd_ref[c], preferred_element_type=F32)
        if c == 0:
            acc_ref[...] = d
        else:
            acc_ref[...] += d
    y = x + 0.5 * acc_ref[...]
    if final:
        y = _rms(y, fw_ref[...], NORM_EPS)
    o_ref[...] = y


def _ffn(x, norm_w, wg, wu, wd, final_w=None):
    m, d = x.shape
    n_chunks, _, fc = wg.shape
    tm = min(m, 512)
    final = final_w is not None
    in_specs = [
        pl.BlockSpec((tm, d), lambda i: (i, 0)),
        _const_spec((1, d)),
        _const_spec((n_chunks, d, fc)),
        _const_spec((n_chunks, d, fc)),
        _const_spec((n_chunks, fc, d)),
    ]
    args = [x, norm_w, wg, wu, wd]
    if final:
        in_specs.append(_const_spec((1, d)))
        args.append(final_w)
    return pl.pallas_call(
        functools.partial(_ffn_body, n_chunks=n_chunks, final=final),
        grid=(m // tm,),
        in_specs=in_specs,
        out_specs=pl.BlockSpec((tm, d), lambda i: (i, 0)),
        out_shape=jax.ShapeDtypeStruct((m, d), F32),
        scratch_shapes=[pltpu.VMEM((tm, d), F32)],
        compiler_params=_params(("parallel",)),
        name="ffn",
    )(*args)


def _mixin_body(x_ref, nw_ref, w_ref, *out_refs, widths, q_index, q_scale, col_tile):
    h = _rms(x_ref[...], nw_ref[...], NORM_EPS).astype(BF16)
    start = 0
    for idx, (ref, width) in enumerate(zip(out_refs, widths)):
        for c0 in range(0, width, col_tile):
            wd = min(col_tile, width - c0)
            r = jnp.dot(h, w_ref[:, start + c0:start + c0 + wd], preferred_element_type=F32)
            if idx == q_index:
                r = r * q_scale
            ref[:, c0:c0 + wd] = r
        start += width


def _mixin(x, norm_w, w_in, widths, q_index, q_scale):
    m, d = x.shape
    n_in = w_in.shape[1]
    tm = min(m, 512)
    return pl.pallas_call(
        functools.partial(_mixin_body, widths=widths, q_index=q_index, q_scale=q_scale, col_tile=512),
        grid=(m // tm,),
        in_specs=[pl.BlockSpec((tm, d), lambda i: (i, 0)), _const_spec((1, d)), _const_spec((d, n_in))],
        out_specs=[pl.BlockSpec((tm, w), lambda i: (i, 0)) for w in widths],
        out_shape=[jax.ShapeDtypeStruct((m, w), F32) for w in widths],
        compiler_params=_params(("parallel",)),
        name="mixin",
    )(x, norm_w, w_in)


def _conv_body(cin_ref, st_ref, cw_ref, ya_ref, cn_ref, buf, *, tt, da):
    @pl.when(pl.program_id(1) == 0)
    def _():
        buf[pl.ds(6, 2), :] = st_ref[0]

    c = cin_ref[0]
    xin = c[:, 0:da]
    gb = c[:, da:2 * da]
    gc = c[:, 2 * da:3 * da]
    u = gc * xin
    buf[pl.ds(8, tt), :] = u
    cw = cw_ref[...]
    y = buf[pl.ds(6, tt), :] * cw[0:1, :]
    y = y + buf[pl.ds(7, tt), :] * cw[1:2, :]
    y = y + u * cw[2:3, :]
    ya_ref[0] = gb * y
    tail = buf[pl.ds(tt + 6, 2), :]
    cn_ref[0] = tail
    buf[pl.ds(6, 2), :] = tail


def _short_conv(cin, conv_state, conv_w):
    b, t, w3 = cin.shape
    da = w3 // 3
    tt = min(t, 512)
    return pl.pallas_call(
        functools.partial(_conv_body, tt=tt, da=da),
        grid=(b, t // tt),
        in_specs=[
            pl.BlockSpec((1, tt, w3), lambda i, j: (i, j, 0)),
            pl.BlockSpec((1, 2, da), lambda i, j: (i, 0, 0)),
            pl.BlockSpec((3, da), lambda i, j: (0, 0)),
        ],
        out_specs=[
            pl.BlockSpec((1, tt, da), lambda i, j: (i, j, 0)),
            pl.BlockSpec((1, 2, da), lambda i, j: (i, 0, 0)),
        ],
        out_shape=[jax.ShapeDtypeStruct((b, t, da), F32), jax.ShapeDtypeStruct((b, 2, da), F32)],
        scratch_shapes=[pltpu.VMEM((tt + 8, da), F32)],
        compiler_params=_params(("parallel", "arbitrary")),
        name="short_conv",
    )(cin, conv_state, conv_w)


def _pre_body(z_ref, ss_ref, mu_ref, wpad_ref, w0_ref, apad_ref, a0_ref, gup_ref, kk_ref, ka_ref, ones_ref,
              r_ref, lw_ref, k_ref, v_ref, a_ref, b_ref, g_ref, sn_ref, buf, *, tt, db, r_lo, r_g):
    @pl.when(pl.program_id(1) == 0)
    def _():
        buf[pl.ds(7, 1), :] = ss_ref[0]

    z = z_ref[0]
    buf[pl.ds(8, tt), :] = z
    zp = buf[pl.ds(7, tt), :]
    zm = z + (zp - z) * mu_ref[...]
    last = buf[pl.ds(tt + 7, 1), :]
    sn_ref[0] = last
    buf[pl.ds(7, 1), :] = last

    r = zm[:, 0:db]
    k = zm[:, db:2 * db]
    v = zm[:, 2 * db:3 * db]
    lo = zm[:, 3 * db:3 * db + r_lo]
    g_lo = zm[:, 3 * db + r_lo:3 * db + r_lo + r_g]
    w_part = jnp.dot(jnp.tanh(lo).astype(BF16), wpad_ref[...], preferred_element_type=F32)
    a_part = jnp.dot(lo.astype(BF16), apad_ref[...], preferred_element_type=F32)
    xw = -(w0_ref[...] + w_part)
    softplus = jnp.maximum(xw, 0.0) + jnp.log(1.0 + jnp.exp(-jnp.abs(xw)))
    w_log = -softplus - 0.5
    lw = -jnp.exp(w_log)
    a = jax.nn.sigmoid(a0_ref[...] + a_part)
    g = jnp.dot(jax.nn.sigmoid(g_lo).astype(BF16), gup_ref[...], preferred_element_type=F32)
    kk = k * kk_ref[...]
    ss = _mm(kk * kk, ones_ref[...].astype(F32), pa=2, pb=1)
    kkn = kk * lax.rsqrt(jnp.maximum(ss, KK_EPS))
    r_ref[0] = r
    lw_ref[0] = lw
    k_ref[0] = k * (1.0 + (a - 1.0) * ka_ref[...])
    v_ref[0] = v
    a_ref[0] = -kkn
    b_ref[0] = kkn * a
    g_ref[0] = g


def _rwkv_pre(z, shift_state, mu, wpad, w0, apad, a0, gup, k_k, k_a, ones_bd, db):
    b, t, ns = z.shape
    r_lo = wpad.shape[0]
    r_g = gup.shape[0]
    tt = min(t, 512)
    row = lambda n: pl.BlockSpec((1, n), lambda i, j: (0, 0))
    full = lambda s: pl.BlockSpec(s, lambda i, j: (0, 0))
    seq = lambda n: pl.BlockSpec((1, tt, n), lambda i, j: (i, j, 0))
    return pl.pallas_call(
        functools.partial(_pre_body, tt=tt, db=db, r_lo=r_lo, r_g=r_g),
        grid=(b, t // tt),
        in_specs=[
            seq(ns),
            pl.BlockSpec((1, 1, ns), lambda i, j: (i, 0, 0)),
            row(ns), full((r_lo, db)), row(db), full((r_lo, db)), row(db), full((r_g, db)),
            row(db), row(db), full((db, db)),
        ],
        out_specs=[seq(db)] * 7 + [pl.BlockSpec((1, 1, ns), lambda i, j: (i, 0, 0))],
        out_shape=[jax.ShapeDtypeStruct((b, t, db), F32)] * 7 + [jax.ShapeDtypeStruct((b, 1, ns), F32)],
        scratch_shapes=[pltpu.VMEM((tt + 8, ns), F32)],
        compiler_params=_params(("parallel", "arbitrary")),
        name="rwkv_pre",
    )(z, shift_state, mu, wpad, w0, apad, a0, gup, k_k, k_a, ones_bd)


SCAN_SUB = 16
SCAN_P = 1


def _mm_each(a_list, b_list, dims=NN_DIMS):
    return [_mm(a, b, dims=dims, pa=SCAN_P, pb=SCAN_P) for a, b in zip(a_list, b_list)]


def _neumann_inverse_each(l_list, n):
    c = l_list[0].shape[0]
    eye = (lax.broadcasted_iota(jnp.int32, (c, c), 0) == lax.broadcasted_iota(jnp.int32, (c, c), 1)).astype(F32)
    x = [eye + l for l in l_list]
    p = l_list
    span = 2
    while span < n:
        p = _mm_each(p, p)
        xp = _mm_each(x, p)
        x = [xi + d for xi, d in zip(x, xp)]
        span *= 2
    return x


def _unit_lower_inverse_each(l_list):
    c = l_list[0].shape[0]
    if c <= SCAN_SUB:
        return _neumann_inverse_each(l_list, c)
    rows = lax.broadcasted_iota(jnp.int32, (c, c), 0) // SCAN_SUB
    cols = lax.broadcasted_iota(jnp.int32, (c, c), 1) // SCAN_SUB
    same = rows == cols
    t_diag = _neumann_inverse_each([jnp.where(same, l, 0.0) for l in l_list], SCAN_SUB)
    z = _mm_each(t_diag, [jnp.where(same, 0.0, l) for l in l_list])
    nblk = c // SCAN_SUB
    factors = []
    span = 1
    while span < nblk:
        factors.append(z)
        span *= 2
        if span < nblk:
            z = _mm_each(z, z)
    out = t_diag
    for f in reversed(factors):
        d = _mm_each(f, out)
        out = [o + di for o, di in zip(out, d)]
    return out


def _scan_body(r_ref, lw_ref, k_ref, v_ref, a_ref, b_ref, g_ref, s0_ref, rk_ref, lnw_ref, lnb_ref,
               o_ref, sT_ref, st_ref, *, c, bb, nh, hd):
    ci = pl.program_id(1)
    units = [(bi, h) for bi in range(bb) for h in range(nh)]

    @pl.when(ci == 0)
    def _():
        for bi, h in units:
            st_ref[bi, h] = s0_ref[bi, h].T

    rows = lax.broadcasted_iota(jnp.int32, (c, c), 0)
    cols = lax.broadcasted_iota(jnp.int32, (c, c), 1)
    tri = (rows >= cols).astype(F32)
    rows2 = lax.broadcasted_iota(jnp.int32, (c, 2 * c), 0)
    cols2 = lax.broadcasted_iota(jnp.int32, (c, 2 * c), 1)
    cols2 = jnp.where(cols2 >= c, cols2 - c, cols2)
    strict2 = rows2 > cols2
    incl2 = rows2 >= cols2
    ones_cols = jnp.ones((c, hd), F32)
    zeros_cv = jnp.zeros((c, hd), F32)

    ar, bk, bk_rem, v_u, decay = [], [], [], [], []
    for bi in range(bb):
        r = r_ref[bi]
        lw = lw_ref[bi]
        k = k_ref[bi]
        a = a_ref[bi]
        b = b_ref[bi]
        v = v_ref[bi]
        cum = _mm(tri, lw, pa=1, pb=3)
        cum_last = cum[c - 1:c, :]
        e_neg = jnp.exp(-cum)
        e_rem = jnp.exp(cum_last - cum)
        at = a * jnp.exp(cum - lw)
        rt = r * jnp.exp(cum)
        bt = b * e_neg
        kt = k * e_neg
        bh = b * e_rem
        kh = k * e_rem
        cum_cols = _mm(lw, ones_cols, dims=TN_DIMS, pa=3, pb=1)
        for h in range(nh):
            sl = slice(h * hd, (h + 1) * hd)
            ar.append(jnp.concatenate([at[:, sl], rt[:, sl]], axis=0))
            bk.append(jnp.concatenate([bt[:, sl], kt[:, sl]], axis=0))
            bk_rem.append(jnp.concatenate([bh[:, sl], kh[:, sl]], axis=0))
            v_u.append(v[:, sl])
            decay.append(jnp.exp(cum_cols[h * hd:(h + 1) * hd, :]))

    st = [st_ref[bi, h] for bi, h in units]
    m4 = _mm_each(ar, bk, dims=NT_DIMS)
    m_a = [jnp.where(strict2, m[0:c], 0.0) for m in m4]
    m_r = [jnp.where(incl2, m[c:2 * c], 0.0) for m in m4]
    t_inv = _unit_lower_inverse_each([m[:, 0:c] for m in m_a])
    g1 = _mm_each([x[0:c] for x in ar], st)
    g2 = _mm_each(m_a, [jnp.concatenate([zeros_cv, vh], axis=0) for vh in v_u])
    u = _mm_each(t_inv, [x + y for x, y in zip(g1, g2)])
    uv = [jnp.concatenate([ui, vh], axis=0) for ui, vh in zip(u, v_u)]
    y1 = _mm_each([x[c:2 * c] for x in ar], st)
    y2 = _mm_each(m_r, uv)
    st_upd = _mm_each(bk_rem, uv, dims=TN_DIMS)
    for (bi, h), s_old, d, upd in zip(units, st, decay, st_upd):
        st_ref[bi, h] = s_old * d + upd

    rk = rk_ref[...]
    for bi in range(bb):
        r = r_ref[bi]
        k = k_ref[bi]
        rkk = r * k * rk
        yn_parts, bonus_parts = [], []
        for h in range(nh):
            i = bi * nh + h
            sl = slice(h * hd, (h + 1) * hd)
            y = y1[i] + y2[i]
            mean = jnp.mean(y, axis=-1, keepdims=True)
            yc = y - mean
            var = jnp.mean(yc * yc, axis=-1, keepdims=True)
            yn_parts.append(yc * lax.rsqrt(var + LNX_EPS))
            bonus_parts.append(jnp.sum(rkk[:, sl], axis=-1, keepdims=True) * v_u[i])
        yn_all = jnp.concatenate(yn_parts, axis=1)
        bonus_all = jnp.concatenate(bonus_parts, axis=1)
        o_ref[bi] = (yn_all * lnw_ref[...] + lnb_ref[...] + bonus_all) * g_ref[bi]

    @pl.when(ci == pl.num_programs(1) - 1)
    def _():
        for bi, h in units:
            sT_ref[bi, h] = st_ref[bi, h].T


def _rwkv_scan(r, lw, k, v, a, b, g, s0, r_k, lnx_w, lnx_b):
    bsz, t, db = r.shape
    _, nh, hd, _ = s0.shape
    c = min(t, 64)
    bb = 4 if (t // c == 1 and bsz % 4 == 0) else 1
    seq = pl.BlockSpec((bb, c, db), lambda i, j: (i, j, 0))
    row = pl.BlockSpec((1, db), lambda i, j: (0, 0))
    st_spec = pl.BlockSpec((bb, nh, hd, hd), lambda i, j: (i, 0, 0, 0))
    return pl.pallas_call(
        functools.partial(_scan_body, c=c, bb=bb, nh=nh, hd=hd),
        grid=(bsz // bb, t // c),
        in_specs=[seq] * 7 + [st_spec, row, row, row],
        out_specs=[seq, st_spec],
        out_shape=[jax.ShapeDtypeStruct((bsz, t, db), F32), jax.ShapeDtypeStruct((bsz, nh, hd, hd), F32)],
        scratch_shapes=[pltpu.VMEM((bb, nh, hd, hd), F32)],
        compiler_params=_params(("parallel", "arbitrary")),
        name="rwkv_scan",
    )(r, lw, k, v, a, b, g, s0, r_k, lnx_w, lnx_b)


def _lam_value(lq1, lk1, lq2, lk2, lam_init):
    s1 = jnp.sum(lq1 * lk1, axis=-1, keepdims=True)
    s2 = jnp.sum(lq2 * lk2, axis=-1, keepdims=True)
    return jnp.exp(s1) - jnp.exp(s2) + lam_init


def _stack_maps(q, tn, dh):
    q2x = jnp.concatenate([q, q], axis=0)
    row = lax.broadcasted_iota(jnp.int32, q2x.shape, 0)
    lane = lax.broadcasted_iota(jnp.int32, q2x.shape, 1)
    keep = (row < tn) == (lane < dh)
    return jnp.where(keep, q2x, 0.0).astype(BF16)


def _lane_tile(x, width):
    reps = width // V7X_LANES
    return x if reps == 1 else jnp.concatenate([x] * reps, axis=1)


def _diff_finish(o12, tn, lam, subln_w, lam_init):
    o = o12[0:tn] - lam * o12[tn:2 * tn]
    return _rms(o, subln_w, SUBLN_EPS) * (1.0 - lam_init)


def _pattn_body(q_ref, k_ref, v_ref, slope_ref, lq1_ref, lk1_ref, lq2_ref, lk2_ref, sw_ref, o_ref,
                q12_s, m_s, l_s, acc_s, *, tq, dh, lam_init):
    qi = pl.program_id(2)
    ki = pl.program_id(3)

    @pl.when(ki == 0)
    def _():
        q12_s[...] = _stack_maps(q_ref[0], tq, dh)
        m_s[...] = jnp.full(m_s.shape, NEG_BIG, F32)
        l_s[...] = jnp.zeros(l_s.shape, F32)
        acc_s[...] = jnp.zeros(acc_s.shape, F32)

    def update(diagonal):
        k_bf = k_ref[0].astype(BF16)
        v_bf = v_ref[0].astype(BF16)
        s = lax.dot_general(q12_s[...], k_bf, NT_DIMS, preferred_element_type=F32)
        col = lax.broadcasted_iota(jnp.int32, (1, tq), 1)
        s = s + slope_ref[0][:, 0:1] * ((ki - qi) * tq + col).astype(F32)
        if diagonal:
            row = lax.broadcasted_iota(jnp.int32, (2 * tq, tq), 0)
            qrow = jnp.where(row >= tq, row - tq, row)
            s = jnp.where(qrow >= lax.broadcasted_iota(jnp.int32, (2 * tq, tq), 1), s, NEG_BIG)
        m_old = m_s[...]
        m_new = jnp.maximum(m_old, jnp.max(s, axis=-1, keepdims=True))
        alpha = jnp.exp2(m_old - m_new)
        p = jnp.exp2(s - _lane_tile(m_new, tq))
        l_s[...] = alpha * l_s[...] + jnp.sum(p, axis=-1, keepdims=True)
        acc_s[...] = alpha * acc_s[...] + jnp.dot(p.astype(BF16), v_bf, preferred_element_type=F32)
        m_s[...] = m_new

    @pl.when(ki < qi)
    def _():
        update(False)

    @pl.when(ki == qi)
    def _():
        update(True)
        lam = _lam_value(lq1_ref[...], lk1_ref[...], lq2_ref[...], lk2_ref[...], lam_init)
        o_ref[0] = _diff_finish(acc_s[...] / l_s[...], tq, lam, sw_ref[...], lam_init)


def _prompt_attention(q, k, v, slopes, lq1, lk1, lq2, lk2, subln_w, lam_init, n_heads):
    b, t, w = q.shape
    hw = w // n_heads
    assert hw == V7X_LANES, "softmax statistics are kept lane-replicated at the head width"
    dh = hw // 2
    tq = min(t, 512)
    nq = t // tq
    qspec = pl.BlockSpec((1, tq, hw), lambda bi, h, qi, ki: (bi, qi, h))
    kspec = pl.BlockSpec((1, tq, hw), lambda bi, h, qi, ki: (bi, jnp.minimum(ki, qi), h))
    small = lambda n: pl.BlockSpec((1, n), lambda bi, h, qi, ki: (0, 0))
    return pl.pallas_call(
        functools.partial(_pattn_body, tq=tq, dh=dh, lam_init=lam_init),
        grid=(b, n_heads, nq, nq),
        in_specs=[qspec, kspec, kspec,
                  pl.BlockSpec((1, 1, V7X_LANES), lambda bi, h, qi, ki: (h, 0, 0)),
                  small(dh), small(dh), small(dh), small(dh), small(hw)],
        out_specs=qspec,
        out_shape=jax.ShapeDtypeStruct((b, t, w), F32),
        scratch_shapes=[pltpu.VMEM((2 * tq, hw), BF16), pltpu.VMEM((2 * tq, hw), F32),
                        pltpu.VMEM((2 * tq, hw), F32), pltpu.VMEM((2 * tq, hw), F32)],
        compiler_params=_params(("parallel", "parallel", "parallel", "arbitrary")),
        name="prompt_attention",
    )(q, k, v, slopes, lq1, lk1, lq2, lk2, subln_w)


def _sattn_body(pt_ref, q_ref, kn_ref, vn_ref, slope_ref, lq1_ref, lk1_ref, lq2_ref, lk2_ref, sw_ref, *rest,
                ppb, page, n_heads, tn, dh, past_len, lam_init):
    k_refs = rest[:ppb]
    v_refs = rest[ppb:2 * ppb]
    o_ref, m_s, l_s, acc_s = rest[2 * ppb:]
    del pt_ref
    si = pl.program_id(1)
    hw = 2 * dh
    width = ppb * page
    heads = range(n_heads)

    @pl.when(si == 0)
    def _():
        m_s[...] = jnp.full(m_s.shape, NEG_BIG, F32)
        l_s[...] = jnp.zeros(l_s.shape, F32)
        acc_s[...] = jnp.zeros(acc_s.shape, F32)

    q = q_ref[0]
    qm = [q[:, m * dh:(m + 1) * dh].astype(BF16) for m in range(2 * n_heads)]

    def scores(h, key_of_map, dims):
        sa = lax.dot_general(qm[2 * h], key_of_map(2 * h), dims, preferred_element_type=F32)
        sb = lax.dot_general(qm[2 * h + 1], key_of_map(2 * h + 1), dims, preferred_element_type=F32)
        return jnp.concatenate([sa, sb], axis=0)

    kpos = si * width + lax.broadcasted_iota(jnp.int32, (1, width), 1) - past_len
    kpos = kpos.astype(F32)
    s_all = []
    for h in heads:
        parts = [scores(h, lambda m, j=j: k_refs[j][m].astype(BF16), NN_DIMS) for j in range(ppb)]
        s_all.append(jnp.concatenate(parts, axis=1) + slope_ref[h][:, 0:1] * kpos)
    alphas, ps = [], []
    for h in heads:
        m_old = m_s[h]
        m_new = jnp.maximum(m_old, jnp.max(s_all[h], axis=-1, keepdims=True))
        alpha = jnp.exp2(m_old - m_new)
        p = jnp.exp2(s_all[h] - _lane_tile(m_new, width))
        l_s[h] = alpha * l_s[h] + jnp.sum(p, axis=-1, keepdims=True)
        m_s[h] = m_new
        alphas.append(alpha)
        ps.append(p.astype(BF16))
    for h in heads:
        pv = None
        for j in range(ppb):
            t = jnp.dot(ps[h][:, j * page:(j + 1) * page],
                        v_refs[j][pl.ds(h, page, stride=n_heads), :].astype(BF16),
                        preferred_element_type=F32)
            pv = t if pv is None else pv + t
        acc_s[h] = alphas[h] * acc_s[h] + pv

    @pl.when(si == pl.num_programs(1) - 1)
    def _():
        lam = _lam_value(lq1_ref[...], lk1_ref[...], lq2_ref[...], lk2_ref[...], lam_init)
        kn = kn_ref[0]
        vn = vn_ref[0]
        row_n = lax.broadcasted_iota(jnp.int32, (2 * tn, tn), 0)
        col_n = lax.broadcasted_iota(jnp.int32, (2 * tn, tn), 1)
        causal = jnp.where(row_n >= tn, row_n - tn, row_n) >= col_n
        outs = []
        for h in heads:
            s = scores(h, lambda m: kn[:, m * dh:(m + 1) * dh].astype(BF16), NT_DIMS)
            s = jnp.where(causal, s + slope_ref[h][:, 0:1] * col_n.astype(F32), NEG_BIG)
            m_old = m_s[h]
            m_new = jnp.maximum(m_old, jnp.max(s, axis=-1, keepdims=True))
            alpha = jnp.exp2(m_old - m_new)
            p = jnp.exp2(s - m_new[:, 0:tn])
            l_fin = alpha * l_s[h] + jnp.sum(p, axis=-1, keepdims=True)
            acc = alpha * acc_s[h] + jnp.dot(p.astype(BF16), vn[:, h * hw:(h + 1) * hw].astype(BF16),
                                             preferred_element_type=F32)
            outs.append(_diff_finish(acc / l_fin, tn, lam, sw_ref[...], lam_init))
        o_ref[0] = jnp.concatenate(outs, axis=1)


def _sample_attention(q, k_new, v_new, cache_kt, cache_v, page_table, layer, slopes, lq1, lk1, lq2, lk2, subln_w,
                      lam_init):
    b, tn, w = q.shape
    _, _, maps, dh, page = cache_kt.shape
    n_heads = maps // 2
    hw = 2 * dh
    assert hw == V7X_LANES, "softmax statistics are kept lane-replicated at the head width"
    n_pages = page_table.shape[1]
    ppb = math.gcd(n_pages, 8)
    n_steps = n_pages // ppb
    pt_flat = page_table.reshape(-1)

    def page_index(j, nd):
        return lambda bi, si, pt: (layer, pt[bi * n_pages + si * ppb + j]) + (0,) * nd

    k_specs = [pl.BlockSpec((None, None, maps, dh, page), page_index(j, 3)) for j in range(ppb)]
    v_specs = [pl.BlockSpec((None, None, page * n_heads, hw), page_index(j, 2)) for j in range(ppb)]
    new_spec = pl.BlockSpec((1, tn, w), lambda bi, si, pt: (bi, 0, 0))
    small = lambda n: pl.BlockSpec((1, n), lambda bi, si, pt: (0, 0))
    grid_spec = pltpu.PrefetchScalarGridSpec(
        num_scalar_prefetch=1,
        grid=(b, n_steps),
        in_specs=[new_spec, new_spec, new_spec,
                  pl.BlockSpec((n_heads, 1, V7X_LANES), lambda bi, si, pt: (0, 0, 0)),
                  small(dh), small(dh), small(dh), small(dh), small(hw)] + k_specs + v_specs,
        out_specs=new_spec,
        scratch_shapes=[pltpu.VMEM((n_heads, 2 * tn, hw), F32), pltpu.VMEM((n_heads, 2 * tn, hw), F32),
                        pltpu.VMEM((n_heads, 2 * tn, hw), F32)],
    )
    return pl.pallas_call(
        functools.partial(_sattn_body, ppb=ppb, page=page, n_heads=n_heads, tn=tn, dh=dh,
                          past_len=n_pages * page, lam_init=lam_init),
        grid_spec=grid_spec,
        out_shape=jax.ShapeDtypeStruct((b, tn, w), F32),
        compiler_params=_params(("parallel", "arbitrary")),
        name="sample_attention",
    )(pt_flat, q, k_new, v_new, slopes, lq1, lk1, lq2, lk2, subln_w,
      *([cache_kt] * ppb), *([cache_v] * ppb))


def _merge_body(x_ref, ya_ref, yb_ref, yc_ref, gt_ref, wa_ref, wb_ref, wc_ref, wo_ref, o_ref, *, d):
    gates = gt_ref[...]
    merged = jax.nn.sigmoid(gates[:, 0:d]) * jnp.dot(ya_ref[...].astype(BF16), wa_ref[...], preferred_element_type=F32)
    merged = merged + jax.nn.sigmoid(gates[:, d:2 * d]) * jnp.dot(
        yb_ref[...].astype(BF16), wb_ref[...], preferred_element_type=F32)
    merged = merged + jax.nn.sigmoid(gates[:, 2 * d:3 * d]) * jnp.dot(
        yc_ref[...].astype(BF16), wc_ref[...], preferred_element_type=F32)
    o_ref[...] = x_ref[...] + jnp.dot(merged.astype(BF16), wo_ref[...], preferred_element_type=F32)


def _merge(x, ya, yb, yc, gates, wa, wb, wc, wo):
    m, d = x.shape
    tm = min(m, 512)
    tile = lambda n: pl.BlockSpec((tm, n), lambda i: (i, 0))
    return pl.pallas_call(
        functools.partial(_merge_body, d=d),
        grid=(m // tm,),
        in_specs=[tile(d), tile(ya.shape[1]), tile(yb.shape[1]), tile(yc.shape[1]), tile(3 * d),
                  _const_spec(wa.shape), _const_spec(wb.shape), _const_spec(wc.shape), _const_spec(wo.shape)],
        out_specs=tile(d),
        out_shape=jax.ShapeDtypeStruct((m, d), F32),
        compiler_params=_params(("parallel",)),
        name="merge",
    )(x, ya, yb, yc, gates, wa, wb, wc, wo)


def _prep_layer_weights(l, W, dims):
    d, d_ff, db, r_w, r_a = dims["d"], dims["d_ff"], dims["db"], dims["r_w"], dims["r_a"]
    fc = 256 if d_ff % 256 == 0 else V7X_LANES
    nck = d_ff // fc
    row = lambda a: a.reshape(1, -1)

    def ffn(prefix):
        wg = W[prefix + "_w_gate"][l].astype(BF16).reshape(d, nck, fc).transpose(1, 0, 2)
        wu = W[prefix + "_w_up"][l].astype(BF16).reshape(d, nck, fc).transpose(1, 0, 2)
        wd = W[prefix + "_w_down"][l].astype(BF16).reshape(nck, fc, d)
        return row(W[prefix + "_norm"][l]), wg, wu, wd

    zeros_w = jnp.zeros((r_a, db), BF16)
    zeros_a = jnp.zeros((r_w, db), BF16)
    head = jnp.arange(db) // dims["hd"]
    return dict(
        ffn1=ffn("ffn1"), ffn2=ffn("ffn2"),
        mix_norm=row(W["mix_norm"][l]), w_in=W["w_in"][l].astype(BF16),
        conv_w=W["conv_w"][l],
        mu=row(W["tm_mu"][l]),
        wpad=jnp.concatenate([W["w_up_w"][l].astype(BF16), zeros_w], axis=0),
        apad=jnp.concatenate([zeros_a, W["a_up_w"][l].astype(BF16)], axis=0),
        w0=row(W["w0"][l]), a0=row(W["a0"][l]), gup=W["g_up_w"][l].astype(BF16),
        k_k=row(W["k_k"][l]), k_a=row(W["k_a"][l]), r_k=row(W["r_k"][l]),
        lnx_w=row(W["lnx_w"][l]), lnx_b=row(W["lnx_b"][l]),
        ones_bd=(head[:, None] == head[None, :]).astype(BF16),
        lq1=row(W["lam_q1"][l]), lk1=row(W["lam_k1"][l]), lq2=row(W["lam_q2"][l]), lk2=row(W["lam_k2"][l]),
        subln_w=row(W["subln_w"][l]),
        wa=W["w_branch_a"][l].astype(BF16), wb=W["w_branch_b"][l].astype(BF16),
        wc=W["w_branch_c"][l].astype(BF16), wo=W["w_out"][l].astype(BF16),
    )


def _run_trunk(x, layers, final_w, conv0, shift0, wkv0, attend, dims):
    b, t, d = x.shape
    m = b * t
    da, db, ns = dims["da"], dims["db"], dims["ns"]
    qw, kw, dc = dims["qw"], dims["kw"], dims["dc"]
    widths = (3 * da, ns, qw, kw, dc, 3 * d)
    depth = len(layers)
    xf = x.reshape(m, d)
    k_rows, v_rows, wkv_out, shift_out, conv_out = [], [], [], [], []
    for l, lw in enumerate(layers):
        xf = _ffn(xf, *lw["ffn1"])
        cin, z, q, k, v, gates = _mixin(xf, lw["mix_norm"], lw["w_in"], widths, 2, dims["dh"] ** -0.5 * LOG2_E)
        y_a, conv_new = _short_conv(cin.reshape(b, t, 3 * da), conv0[l], lw["conv_w"])
        r_, lw_, k_, v_, a_, b_, g_, shift_new = _rwkv_pre(
            z.reshape(b, t, ns), shift0[l].reshape(b, 1, ns), lw["mu"], lw["wpad"], lw["w0"], lw["apad"], lw["a0"],
            lw["gup"], lw["k_k"], lw["k_a"], lw["ones_bd"], db)
        y_b, wkv_new = _rwkv_scan(r_, lw_, k_, v_, a_, b_, g_, wkv0[l], lw["r_k"], lw["lnx_w"], lw["lnx_b"])
        lam_init = 0.8 - 0.6 * math.exp(-0.3 * l)
        y_c = attend(l, q.reshape(b, t, qw), k.reshape(b, t, kw), v.reshape(b, t, dc), lw, lam_init)
        xf = _merge(xf, y_a.reshape(m, da), y_b.reshape(m, db), y_c.reshape(m, dc), gates,
                    lw["wa"], lw["wb"], lw["wc"], lw["wo"])
        xf = _ffn(xf, *lw["ffn2"], final_w=final_w if l == depth - 1 else None)
        k_rows.append(k.reshape(b, t, 2 * dims["hc"], dims["dh"]))
        v_rows.append(v.reshape(b, t, dims["hc"], 2 * dims["dh"]))
        wkv_out.append(wkv_new)
        shift_out.append(shift_new.reshape(b, ns))
        conv_out.append(conv_new)
    return (xf.reshape(b, t, d), jnp.stack(k_rows), jnp.stack(v_rows), jnp.stack(wkv_out),
            jnp.stack(shift_out), jnp.stack(conv_out))


def kernel(x_prompt, x_sample, cache_k, cache_v, state_wkv, state_shift, state_conv, page_table, ffn1_norm, ffn1_w_gate, ffn1_w_up, ffn1_w_down, mix_norm, w_in, conv_w, tm_mu, w_up_w, w0, a_up_w, a0, g_up_w, k_k, k_a, r_k, lnx_w, lnx_b, lam_q1, lam_k1, lam_q2, lam_k2, subln_w, w_branch_a, w_branch_b, w_branch_c, w_out, ffn2_norm, ffn2_w_gate, ffn2_w_up, ffn2_w_down, final_norm):
    W = dict(ffn1_norm=ffn1_norm, ffn1_w_gate=ffn1_w_gate, ffn1_w_up=ffn1_w_up, ffn1_w_down=ffn1_w_down,
             mix_norm=mix_norm, w_in=w_in, conv_w=conv_w, tm_mu=tm_mu, w_up_w=w_up_w, w0=w0,
             a_up_w=a_up_w, a0=a0, g_up_w=g_up_w, k_k=k_k, k_a=k_a, r_k=r_k, lnx_w=lnx_w, lnx_b=lnx_b,
             lam_q1=lam_q1, lam_k1=lam_k1, lam_q2=lam_q2, lam_k2=lam_k2, subln_w=subln_w,
             w_branch_a=w_branch_a, w_branch_b=w_branch_b, w_branch_c=w_branch_c, w_out=w_out,
             ffn2_norm=ffn2_norm, ffn2_w_gate=ffn2_w_gate, ffn2_w_up=ffn2_w_up, ffn2_w_down=ffn2_w_down)
    depth = w_in.shape[0]
    d = x_prompt.shape[-1]
    _, n_pool, page, maps, dh = cache_k.shape
    hc = maps // 2
    nh, hd = r_k.shape[1], r_k.shape[2]
    db = nh * hd
    da = conv_w.shape[-1]
    assert conv_w.shape[1] == 3, "short conv kernel is written for width 3"
    r_w, r_a, r_g = w_up_w.shape[1], a_up_w.shape[1], g_up_w.shape[1]
    ns = state_shift.shape[-1]
    assert ns == 3 * db + r_w + r_a + r_g
    assert (r_w + r_a) % V7X_LANES == 0 and r_g % V7X_LANES == 0
    dims = dict(d=d, d_ff=ffn1_w_gate.shape[-1], da=da, db=db, hd=hd, ns=ns, r_w=r_w, r_a=r_a,
                qw=maps * dh, kw=maps * dh, dc=hc * 2 * dh, hc=hc, dh=dh)
    layers = [_prep_layer_weights(l, W, dims) for l in range(depth)]
    final_w = final_norm.reshape(1, d)
    slopes = jnp.broadcast_to(
        jnp.asarray([LOG2_E * 2.0 ** (-8.0 * (h + 1) / hc) for h in range(hc)], F32)[:, None, None],
        (hc, 1, V7X_LANES))
    cache_kt = jnp.transpose(cache_k, (0, 1, 3, 4, 2))
    cache_v2 = cache_v.reshape(depth, n_pool, page * hc, 2 * dh)

    def attend_prompt(l, q, k, v, lw, lam_init):
        return _prompt_attention(q, k, v, slopes, lw["lq1"], lw["lk1"], lw["lq2"], lw["lk2"], lw["subln_w"],
                                 lam_init, hc)

    def attend_sample(l, q, k, v, lw, lam_init):
        return _sample_attention(q, k, v, cache_kt, cache_v2, page_table, l, slopes, lw["lq1"], lw["lk1"],
                                 lw["lq2"], lw["lk2"], lw["subln_w"], lam_init)

    bp = x_prompt.shape[0]
    conv0 = jnp.zeros((depth, bp, 2, da), F32)
    shift0 = jnp.zeros((depth, bp, ns), F32)
    wkv0 = jnp.zeros((depth, bp, nh, hd, hd), F32)
    y_p, k_p, v_p, wkv_p, shift_p, conv_p = _run_trunk(
        x_prompt, layers, final_w, conv0, shift0, wkv0, attend_prompt, dims)
    y_s, k_s, v_s, wkv_s, shift_s, conv_s = _run_trunk(
        x_sample, layers, final_w, state_conv, state_shift, state_wkv, attend_sample, dims)
    return (y_p, y_s, k_p, v_p, k_s, v_s, wkv_p, wkv_s, shift_p, shift_s, conv_p, conv_s)
```

```python
import functools
import math

import jax
import jax.numpy as jnp
from jax import lax
from jax.experimental import pallas as pl
from jax.experimental.pallas import tpu as pltpu

F32 = jnp.float32
BF16 = jnp.bfloat16

NORM_EPS = 1e-6
LNX_EPS = 64e-5
SUBLN_EPS = 1e-5
KK_EPS = 1e-24
NEG_BIG = -1e30
LOG2_E = math.log2(math.e)

V7X_LANES = 128
V7X_SUBLANES = 8
V7X_VMEM_LIMIT_BYTES = 56 * 1024 * 1024

NT_DIMS = (((1,), (1,)), ((), ()))
NN_DIMS = (((1,), (0,)), ((), ()))
TN_DIMS = (((0,), (0,)), ((), ()))


def _params(semantics):
    return pltpu.CompilerParams(dimension_semantics=semantics, vmem_limit_bytes=V7X_VMEM_LIMIT_BYTES)


def _const_spec(shape):
    nd = len(shape)
    return pl.BlockSpec(shape, lambda *_: (0,) * nd, pipeline_mode=pl.Buffered(1))


def _rms(x, w, eps):
    ms = jnp.mean(x * x, axis=-1, keepdims=True)
    return x * lax.rsqrt(ms + eps) * w


def _split_bf16(x, n):
    pieces = []
    rem = x
    for i in range(n):
        p = rem.astype(BF16)
        pieces.append(p)
        if i + 1 < n:
            rem = rem - p.astype(F32)
    return pieces


def _mm(a, b, dims=NN_DIMS, pa=1, pb=1):
    a_p = _split_bf16(a, pa)
    b_p = _split_bf16(b, pb)
    order = max(pa, pb)
    acc = None
    for i, ai in enumerate(a_p):
        for j, bj in enumerate(b_p):
            if i + j >= order:
                continue
            t = lax.dot_general(ai, bj, dims, preferred_element_type=F32)
            acc = t if acc is None else acc + t
    return acc


FFN_CHUNK = 256


def _ffn_body(x_ref, nw_ref, wg_ref, wu_ref, wd_ref, *rest, n_chunks, fc, final):
    if final:
        fw_ref, o_ref, acc_ref = rest
    else:
        o_ref, acc_ref = rest
    x = x_ref[...]
    h = _rms(x, nw_ref[...], NORM_EPS).astype(BF16)
    for c in range(n_chunks):
        cs = slice(c * fc, (c + 1) * fc)
        g = jnp.dot(h, wg_ref[:, cs], preferred_element_type=F32)
        u = jnp.dot(h, wu_ref[:, cs], preferred_element_type=F32)
        act = (g * jax.nn.sigmoid(g) * u).astype(BF16)
        d = jnp.dot(act, wd_ref[cs, :], preferred_element_type=F32)
        if c == 0:
            acc_ref[...] = d
        else:
            acc_ref[...] += d
    y = x + 0.5 * acc_ref[...]
    if final:
        y = _rms(y, fw_ref[...], NORM_EPS)
    o_ref[...] = y


def _ffn(x, norm_w, wg, wu, wd, final_w=None):
    m, d = x.shape
    d_ff = wg.shape[1]
    fc = FFN_CHUNK if d_ff % FFN_CHUNK == 0 else V7X_LANES
    n_chunks = d_ff // fc
    tm = min(m, 512)
    final = final_w is not None
    in_specs = [
        pl.BlockSpec((tm, d), lambda i: (i, 0)),
        _const_spec((1, d)),
        _const_spec((d, d_ff)),
        _const_spec((d, d_ff)),
        _const_spec((d_ff, d)),
    ]
    args = [x, norm_w, wg, wu, wd]
    if final:
        in_specs.append(_const_spec((1, d)))
        args.append(final_w)
    return pl.pallas_call(
        functools.partial(_ffn_body, n_chunks=n_chunks, fc=fc, final=final),
        grid=(m // tm,),
        in_specs=in_specs,
        out_specs=pl.BlockSpec((tm, d), lambda i: (i, 0)),
        out_shape=jax.ShapeDtypeStruct((m, d), F32),
        scratch_shapes=[pltpu.VMEM((tm, d), F32)],
        compiler_params=_params(("parallel",)),
        name="ffn",
    )(*args)


def _mixin_body(x_ref, nw_ref, w_ref, *out_refs, widths, q_index, q_scale, col_tile):
    h = _rms(x_ref[...], nw_ref[...], NORM_EPS).astype(BF16)
    start = 0
    for idx, (ref, width) in enumerate(zip(out_refs, widths)):
        for c0 in range(0, width, col_tile):
            wd = min(col_tile, width - c0)
            r = jnp.dot(h, w_ref[:, start + c0:start + c0 + wd], preferred_element_type=F32)
            if idx == q_index:
                r = r * q_scale
            ref[:, c0:c0 + wd] = r
        start += width


def _mixin(x, norm_w, w_in, widths, q_index, q_scale):
    m, d = x.shape
    n_in = w_in.shape[1]
    tm = min(m, 512)
    return pl.pallas_call(
        functools.partial(_mixin_body, widths=widths, q_index=q_index, q_scale=q_scale, col_tile=512),
        grid=(m // tm,),
        in_specs=[pl.BlockSpec((tm, d), lambda i: (i, 0)), _const_spec((1, d)), _const_spec((d, n_in))],
        out_specs=[pl.BlockSpec((tm, w), lambda i: (i, 0)) for w in widths],
        out_shape=[jax.ShapeDtypeStruct((m, w), F32) for w in widths],
        compiler_params=_params(("parallel",)),
        name="mixin",
    )(x, norm_w, w_in)


def _conv_body(cin_ref, st_ref, cw_ref, ya_ref, cn_ref, buf, *, tt, da):
    @pl.when(pl.program_id(1) == 0)
    def _():
        buf[pl.ds(6, 2), :] = st_ref[0]

    c = cin_ref[0]
    xin = c[:, 0:da]
    gb = c[:, da:2 * da]
    gc = c[:, 2 * da:3 * da]
    u = gc * xin
    buf[pl.ds(8, tt), :] = u
    cw = cw_ref[...]
    y = buf[pl.ds(6, tt), :] * cw[0:1, :]
    y = y + buf[pl.ds(7, tt), :] * cw[1:2, :]
    y = y + u * cw[2:3, :]
    ya_ref[0] = gb * y
    tail = buf[pl.ds(tt + 6, 2), :]
    cn_ref[0] = tail
    buf[pl.ds(6, 2), :] = tail


def _short_conv(cin, conv_state, conv_w):
    b, t, w3 = cin.shape
    da = w3 // 3
    tt = min(t, 512)
    return pl.pallas_call(
        functools.partial(_conv_body, tt=tt, da=da),
        grid=(b, t // tt),
        in_specs=[
            pl.BlockSpec((1, tt, w3), lambda i, j: (i, j, 0)),
            pl.BlockSpec((1, 2, da), lambda i, j: (i, 0, 0)),
            pl.BlockSpec((3, da), lambda i, j: (0, 0)),
        ],
        out_specs=[
            pl.BlockSpec((1, tt, da), lambda i, j: (i, j, 0)),
            pl.BlockSpec((1, 2, da), lambda i, j: (i, 0, 0)),
        ],
        out_shape=[jax.ShapeDtypeStruct((b, t, da), F32), jax.ShapeDtypeStruct((b, 2, da), F32)],
        scratch_shapes=[pltpu.VMEM((tt + 8, da), F32)],
        compiler_params=_params(("parallel", "arbitrary")),
        name="short_conv",
    )(cin, conv_state, conv_w)


def _pre_body(z_ref, ss_ref, mu_ref, wpad_ref, w0_ref, apad_ref, a0_ref, gup_ref, kk_ref, ka_ref, ones_ref,
              r_ref, lw_ref, k_ref, v_ref, a_ref, b_ref, g_ref, sn_ref, buf, *, tt, db, r_lo, r_g):
    @pl.when(pl.program_id(1) == 0)
    def _():
        buf[pl.ds(7, 1), :] = ss_ref[0]

    z = z_ref[0]
    buf[pl.ds(8, tt), :] = z
    zp = buf[pl.ds(7, tt), :]
    zm = z + (zp - z) * mu_ref[...]
    last = buf[pl.ds(tt + 7, 1), :]
    sn_ref[0] = last
    buf[pl.ds(7, 1), :] = last

    r = zm[:, 0:db]
    k = zm[:, db:2 * db]
    v = zm[:, 2 * db:3 * db]
    lo = zm[:, 3 * db:3 * db + r_lo]
    g_lo = zm[:, 3 * db + r_lo:3 * db + r_lo + r_g]
    w_part = jnp.dot(jnp.tanh(lo).astype(BF16), wpad_ref[...], preferred_element_type=F32)
    a_part = jnp.dot(lo.astype(BF16), apad_ref[...], preferred_element_type=F32)
    xw = -(w0_ref[...] + w_part)
    softplus = jnp.maximum(xw, 0.0) + jnp.log(1.0 + jnp.exp(-jnp.abs(xw)))
    w_log = -softplus - 0.5
    lw = -jnp.exp(w_log)
    a = jax.nn.sigmoid(a0_ref[...] + a_part)
    g = jnp.dot(jax.nn.sigmoid(g_lo).astype(BF16), gup_ref[...], preferred_element_type=F32)
    kk = k * kk_ref[...]
    ss = _mm(kk * kk, ones_ref[...].astype(F32), pa=2, pb=1)
    kkn = kk * lax.rsqrt(jnp.maximum(ss, KK_EPS))
    r_ref[0] = r
    lw_ref[0] = lw
    k_ref[0] = k * (1.0 + (a - 1.0) * ka_ref[...])
    v_ref[0] = v
    a_ref[0] = -kkn
    b_ref[0] = kkn * a
    g_ref[0] = g


def _rwkv_pre(z, shift_state, mu, wpad, w0, apad, a0, gup, k_k, k_a, ones_bd, db):
    b, t, ns = z.shape
    r_lo = wpad.shape[0]
    r_g = gup.shape[0]
    tt = min(t, 512)
    row = lambda n: pl.BlockSpec((1, n), lambda i, j: (0, 0))
    full = lambda s: pl.BlockSpec(s, lambda i, j: (0, 0))
    seq = lambda n: pl.BlockSpec((1, tt, n), lambda i, j: (i, j, 0))
    return pl.pallas_call(
        functools.partial(_pre_body, tt=tt, db=db, r_lo=r_lo, r_g=r_g),
        grid=(b, t // tt),
        in_specs=[
            seq(ns),
            pl.BlockSpec((1, 1, ns), lambda i, j: (i, 0, 0)),
            row(ns), full((r_lo, db)), row(db), full((r_lo, db)), row(db), full((r_g, db)),
            row(db), row(db), full((db, db)),
        ],
        out_specs=[seq(db)] * 7 + [pl.BlockSpec((1, 1, ns), lambda i, j: (i, 0, 0))],
        out_shape=[jax.ShapeDtypeStruct((b, t, db), F32)] * 7 + [jax.ShapeDtypeStruct((b, 1, ns), F32)],
        scratch_shapes=[pltpu.VMEM((tt + 8, ns), F32)],
        compiler_params=_params(("parallel", "arbitrary")),
        name="rwkv_pre",
    )(z, shift_state, mu, wpad, w0, apad, a0, gup, k_k, k_a, ones_bd)


SCAN_SUB = 16
SCAN_P = 1


def _mm_each(a_list, b_list, dims=NN_DIMS):
    return [_mm(a, b, dims=dims, pa=SCAN_P, pb=SCAN_P) for a, b in zip(a_list, b_list)]


def _neumann_inverse_each(l_list, n):
    c = l_list[0].shape[0]
    eye = (lax.broadcasted_iota(jnp.int32, (c, c), 0) == lax.broadcasted_iota(jnp.int32, (c, c), 1)).astype(F32)
    x = [eye + l for l in l_list]
    p = l_list
    span = 2
    while span < n:
        p = _mm_each(p, p)
        xp = _mm_each(x, p)
        x = [xi + d for xi, d in zip(x, xp)]
        span *= 2
    return x


def _unit_lower_inverse_each(l_list):
    c = l_list[0].shape[0]
    if c <= SCAN_SUB:
        return _neumann_inverse_each(l_list, c)
    rows = lax.broadcasted_iota(jnp.int32, (c, c), 0) // SCAN_SUB
    cols = lax.broadcasted_iota(jnp.int32, (c, c), 1) // SCAN_SUB
    same = rows == cols
    t_diag = _neumann_inverse_each([jnp.where(same, l, 0.0) for l in l_list], SCAN_SUB)
    z = _mm_each(t_diag, [jnp.where(same, 0.0, l) for l in l_list])
    nblk = c // SCAN_SUB
    factors = []
    span = 1
    while span < nblk:
        factors.append(z)
        span *= 2
        if span < nblk:
            z = _mm_each(z, z)
    out = t_diag
    for f in reversed(factors):
        d = _mm_each(f, out)
        out = [o + di for o, di in zip(out, d)]
    return out


def _scan_body(r_ref, lw_ref, k_ref, v_ref, a_ref, b_ref, g_ref, s0_ref, rk_ref, lnw_ref, lnb_ref,
               o_ref, sT_ref, st_ref, *, c, bb, nh, hd):
    ci = pl.program_id(1)
    units = [(bi, h) for bi in range(bb) for h in range(nh)]

    @pl.when(ci == 0)
    def _():
        for bi, h in units:
            st_ref[bi, h] = s0_ref[bi, h].T

    rows = lax.broadcasted_iota(jnp.int32, (c, c), 0)
    cols = lax.broadcasted_iota(jnp.int32, (c, c), 1)
    tri = (rows >= cols).astype(F32)
    rows2 = lax.broadcasted_iota(jnp.int32, (c, 2 * c), 0)
    cols2 = lax.broadcasted_iota(jnp.int32, (c, 2 * c), 1)
    cols2 = jnp.where(cols2 >= c, cols2 - c, cols2)
    strict2 = rows2 > cols2
    incl2 = rows2 >= cols2
    ones_cols = jnp.ones((c, hd), F32)
    zeros_cv = jnp.zeros((c, hd), F32)

    ar, bk, bk_rem, v_u, decay = [], [], [], [], []
    for bi in range(bb):
        r = r_ref[bi]
        lw = lw_ref[bi]
        k = k_ref[bi]
        a = a_ref[bi]
        b = b_ref[bi]
        v = v_ref[bi]
        cum = _mm(tri, lw, pa=1, pb=3)
        cum_last = cum[c - 1:c, :]
        e_neg = jnp.exp(-cum)
        e_rem = jnp.exp(cum_last - cum)
        at = a * jnp.exp(cum - lw)
        rt = r * jnp.exp(cum)
        bt = b * e_neg
        kt = k * e_neg
        bh = b * e_rem
        kh = k * e_rem
        cum_cols = _mm(lw, ones_cols, dims=TN_DIMS, pa=3, pb=1)
        for h in range(nh):
            sl = slice(h * hd, (h + 1) * hd)
            ar.append(jnp.concatenate([at[:, sl], rt[:, sl]], axis=0))
            bk.append(jnp.concatenate([bt[:, sl], kt[:, sl]], axis=0))
            bk_rem.append(jnp.concatenate([bh[:, sl], kh[:, sl]], axis=0))
            v_u.append(v[:, sl])
            decay.append(jnp.exp(cum_cols[h * hd:(h + 1) * hd, :]))

    st = [st_ref[bi, h] for bi, h in units]
    m4 = _mm_each(ar, bk, dims=NT_DIMS)
    m_a = [jnp.where(strict2, m[0:c], 0.0) for m in m4]
    m_r = [jnp.where(incl2, m[c:2 * c], 0.0) for m in m4]
    t_inv = _unit_lower_inverse_each([m[:, 0:c] for m in m_a])
    g1 = _mm_each([x[0:c] for x in ar], st)
    g2 = _mm_each(m_a, [jnp.concatenate([zeros_cv, vh], axis=0) for vh in v_u])
    u = _mm_each(t_inv, [x + y for x, y in zip(g1, g2)])
    uv = [jnp.concatenate([ui, vh], axis=0) for ui, vh in zip(u, v_u)]
    y1 = _mm_each([x[c:2 * c] for x in ar], st)
    y2 = _mm_each(m_r, uv)
    st_upd = _mm_each(bk_rem, uv, dims=TN_DIMS)
    for (bi, h), s_old, d, upd in zip(units, st, decay, st_upd):
        st_ref[bi, h] = s_old * d + upd

    rk = rk_ref[...]
    for bi in range(bb):
        r = r_ref[bi]
        k = k_ref[bi]
        rkk = r * k * rk
        yn_parts, bonus_parts = [], []
        for h in range(nh):
            i = bi * nh + h
            sl = slice(h * hd, (h + 1) * hd)
            y = y1[i] + y2[i]
            mean = jnp.mean(y, axis=-1, keepdims=True)
            yc = y - mean
            var = jnp.mean(yc * yc, axis=-1, keepdims=True)
            yn_parts.append(yc * lax.rsqrt(var + LNX_EPS))
            bonus_parts.append(jnp.sum(rkk[:, sl], axis=-1, keepdims=True) * v_u[i])
        yn_all = jnp.concatenate(yn_parts, axis=1)
        bonus_all = jnp.concatenate(bonus_parts, axis=1)
        o_ref[bi] = (yn_all * lnw_ref[...] + lnb_ref[...] + bonus_all) * g_ref[bi]

    @pl.when(ci == pl.num_programs(1) - 1)
    def _():
        for bi, h in units:
            sT_ref[bi, h] = st_ref[bi, h].T


def _rwkv_scan(r, lw, k, v, a, b, g, s0, r_k, lnx_w, lnx_b):
    bsz, t, db = r.shape
    _, nh, hd, _ = s0.shape
    c = min(t, 64)
    bb = 4 if bsz % 4 == 0 else 1
    seq = pl.BlockSpec((bb, c, db), lambda i, j: (i, j, 0))
    row = pl.BlockSpec((1, db), lambda i, j: (0, 0))
    st_spec = pl.BlockSpec((bb, nh, hd, hd), lambda i, j: (i, 0, 0, 0))
    return pl.pallas_call(
        functools.partial(_scan_body, c=c, bb=bb, nh=nh, hd=hd),
        grid=(bsz // bb, t // c),
        in_specs=[seq] * 7 + [st_spec, row, row, row],
        out_specs=[seq, st_spec],
        out_shape=[jax.ShapeDtypeStruct((bsz, t, db), F32), jax.ShapeDtypeStruct((bsz, nh, hd, hd), F32)],
        scratch_shapes=[pltpu.VMEM((bb, nh, hd, hd), F32)],
        compiler_params=_params(("parallel", "arbitrary")),
        name="rwkv_scan",
    )(r, lw, k, v, a, b, g, s0, r_k, lnx_w, lnx_b)


def _lam_value(lq1, lk1, lq2, lk2, lam_init):
    s1 = jnp.sum(lq1 * lk1, axis=-1, keepdims=True)
    s2 = jnp.sum(lq2 * lk2, axis=-1, keepdims=True)
    return jnp.exp(s1) - jnp.exp(s2) + lam_init


def _stack_maps(q, tn, dh):
    q2x = jnp.concatenate([q, q], axis=0)
    row = lax.broadcasted_iota(jnp.int32, q2x.shape, 0)
    lane = lax.broadcasted_iota(jnp.int32, q2x.shape, 1)
    keep = (row < tn) == (lane < dh)
    return jnp.where(keep, q2x, 0.0).astype(BF16)


def _lane_tile(x, width):
    reps = width // V7X_LANES
    return x if reps == 1 else jnp.concatenate([x] * reps, axis=1)


def _diff_finish(o12, tn, lam, subln_w, lam_init):
    o = o12[0:tn] - lam * o12[tn:2 * tn]
    return _rms(o, subln_w, SUBLN_EPS) * (1.0 - lam_init)


PATTN_COLS = 128
PATTN_KEYS = 256
BIAS_PIECES = 3


def _pattn_body(qi_ref, ki_ref, q_ref, k_ref, v_ref, ktab_ref, slope_ref, lq1_ref, lk1_ref, lq2_ref, lk2_ref, sw_ref,
                o_ref, q12_s, s_s, p_s, m_s, l_s, al_s, acc_s, *, tq, dh, hps, pk, lam_init):
    step = pl.program_id(2)
    qi = qi_ref[step]
    ki = ki_ref[step]
    nq2 = 2 * tq
    hw = 2 * dh
    heads = range(hps)

    @pl.when(ki == 0)
    def _():
        lane = lax.broadcasted_iota(jnp.int32, (nq2, V7X_LANES), 1)
        ones = jnp.where(lane < BIAS_PIECES, 1.0, 0.0).astype(BF16)
        for g in heads:
            q12_s[g] = jnp.concatenate([_stack_maps(q_ref[0][:, g * hw:(g + 1) * hw], tq, dh), ones], axis=1)
        m_s[...] = jnp.full(m_s.shape, NEG_BIG, F32)
        l_s[...] = jnp.zeros(l_s.shape, F32)
        acc_s[...] = jnp.zeros(acc_s.shape, F32)

    def update(diagonal):
        k_all = k_ref[0]
        v_all = v_ref[0]
        vt_bf = []
        for g in heads:
            k_aug = jnp.concatenate([k_all[:, g * hw:(g + 1) * hw].astype(BF16), ktab_ref[g]], axis=1)
            s_s[g] = lax.dot_general(k_aug, q12_s[g], NT_DIMS, preferred_element_type=F32)
            vt_bf.append(v_all[:, g * hw:(g + 1) * hw].T.astype(BF16))
        base = ((ki - qi) * tq).astype(F32)
        for k0 in range(0, tq, pk):
            for g in heads:
                shift = slope_ref[g][:, 0:1] * base
                for c0 in range(0, nq2, PATTN_COLS):
                    q0 = c0 % tq
                    cols = pl.ds(c0, PATTN_COLS)
                    if diagonal and k0 > q0 + PATTN_COLS - 1:
                        p_s[g, :, cols] = jnp.zeros((pk, PATTN_COLS), BF16)
                        al_s[g, :, cols] = jnp.ones((1, PATTN_COLS), F32)
                        continue
                    sv = s_s[g, pl.ds(k0, pk), cols]
                    if diagonal and k0 + pk - 1 > q0:
                        qpos = q0 + lax.broadcasted_iota(jnp.int32, (pk, PATTN_COLS), 1)
                        kpos = k0 + lax.broadcasted_iota(jnp.int32, (pk, PATTN_COLS), 0)
                        sv = jnp.where(qpos >= kpos, sv, NEG_BIG)
                    m_old = m_s[g, :, cols]
                    m_new = jnp.maximum(m_old, jnp.max(sv, axis=0, keepdims=True) + shift)
                    alpha = jnp.exp2(m_old - m_new)
                    p = jnp.exp2(sv - (m_new - shift))
                    m_s[g, :, cols] = m_new
                    l_s[g, :, cols] = alpha * l_s[g, :, cols] + jnp.sum(p, axis=0, keepdims=True)
                    al_s[g, :, cols] = alpha
                    p_s[g, :, cols] = p.astype(BF16)
            for g in heads:
                acc_s[g] = al_s[g] * acc_s[g] + jnp.dot(vt_bf[g][:, k0:k0 + pk], p_s[g],
                                                        preferred_element_type=F32)

    @pl.when(ki < qi)
    def _():
        update(False)

    @pl.when(ki == qi)
    def _():
        update(True)
        lam = _lam_value(lq1_ref[...], lk1_ref[...], lq2_ref[...], lk2_ref[...], lam_init)
        outs = []
        for g in heads:
            o12 = (acc_s[g] / l_s[g]).T
            outs.append(_diff_finish(o12, tq, lam, sw_ref[...], lam_init))
        o_ref[0] = outs[0] if hps == 1 else jnp.concatenate(outs, axis=1)


def _alibi_key_table(slopes, tq):
    rem = slopes[:, 0, 0:1] * jnp.arange(tq, dtype=F32)[None, :]
    pieces = []
    for _ in range(BIAS_PIECES):
        piece = lax.reduce_precision(rem, exponent_bits=8, mantissa_bits=7)
        pieces.append(piece.astype(BF16))
        rem = rem - piece
    tab = jnp.stack(pieces, axis=-1)
    return jnp.pad(tab, ((0, 0), (0, 0), (0, V7X_LANES - BIAS_PIECES)))


def _prompt_attention(q, k, v, slopes, lq1, lk1, lq2, lk2, subln_w, lam_init, n_heads):
    b, t, w = q.shape
    hw = w // n_heads
    assert hw == V7X_LANES, "one head's two maps fill exactly one lane tile"
    dh = hw // 2
    tq = min(t, 512)
    nq = t // tq
    hps = 2 if n_heads % 2 == 0 else 1
    pk = min(PATTN_KEYS, tq)
    ktab = _alibi_key_table(slopes, tq)
    pairs = [(i, j) for i in range(nq) for j in range(i + 1)]
    qi_tbl = jnp.asarray([p[0] for p in pairs], jnp.int32)
    ki_tbl = jnp.asarray([p[1] for p in pairs], jnp.int32)
    qspec = pl.BlockSpec((1, tq, hps * hw), lambda bi, h, s, qt, kt: (bi, qt[s], h))
    kspec = pl.BlockSpec((1, tq, hps * hw), lambda bi, h, s, qt, kt: (bi, kt[s], h))
    small = lambda n: pl.BlockSpec((1, n), lambda bi, h, s, qt, kt: (0, 0))
    row = pltpu.VMEM((hps, 1, 2 * tq), F32)
    grid_spec = pltpu.PrefetchScalarGridSpec(
        num_scalar_prefetch=2,
        grid=(b, n_heads // hps, len(pairs)),
        in_specs=[qspec, kspec, kspec,
                  pl.BlockSpec((hps, tq, V7X_LANES), lambda bi, h, s, qt, kt: (h, 0, 0)),
                  pl.BlockSpec((hps, 1, V7X_LANES), lambda bi, h, s, qt, kt: (h, 0, 0)),
                  small(dh), small(dh), small(dh), small(dh), small(hw)],
        out_specs=qspec,
        scratch_shapes=[pltpu.VMEM((hps, 2 * tq, 2 * hw), BF16), pltpu.VMEM((hps, tq, 2 * tq), F32),
                        pltpu.VMEM((hps, pk, 2 * tq), BF16), row, row, row,
                        pltpu.VMEM((hps, hw, 2 * tq), F32)],
    )
    return pl.pallas_call(
        functools.partial(_pattn_body, tq=tq, dh=dh, hps=hps, pk=pk, lam_init=lam_init),
        grid_spec=grid_spec,
        out_shape=jax.ShapeDtypeStruct((b, t, w), F32),
        compiler_params=_params(("parallel", "parallel", "arbitrary")),
        name="prompt_attention",
    )(qi_tbl, ki_tbl, q, k, v, ktab, slopes, lq1, lk1, lq2, lk2, subln_w)


def _sattn_body(pt_ref, q_ref, kn_ref, vn_ref, slope_ref, lq1_ref, lk1_ref, lq2_ref, lk2_ref, sw_ref, *rest,
                ppb, page, n_heads, tn, dh, past_len, lam_init):
    k_refs = rest[:ppb]
    v_refs = rest[ppb:2 * ppb]
    o_ref, m_s, l_s, acc_s = rest[2 * ppb:]
    del pt_ref
    si = pl.program_id(1)
    hw = 2 * dh
    width = ppb * page
    heads = range(n_heads)

    @pl.when(si == 0)
    def _():
        m_s[...] = jnp.full(m_s.shape, NEG_BIG, F32)
        l_s[...] = jnp.zeros(l_s.shape, F32)
        acc_s[...] = jnp.zeros(acc_s.shape, F32)

    q = q_ref[0]
    qm = [q[:, m * dh:(m + 1) * dh].astype(BF16) for m in range(2 * n_heads)]

    def scores(h, key_of_map, dims):
        sa = lax.dot_general(qm[2 * h], key_of_map(2 * h), dims, preferred_element_type=F32)
        sb = lax.dot_general(qm[2 * h + 1], key_of_map(2 * h + 1), dims, preferred_element_type=F32)
        return jnp.concatenate([sa, sb], axis=0)

    kpos = si * width + lax.broadcasted_iota(jnp.int32, (1, width), 1) - past_len
    kpos = kpos.astype(F32)
    s_all = []
    for h in heads:
        parts = [scores(h, lambda m, j=j: k_refs[j][m].astype(BF16), NN_DIMS) for j in range(ppb)]
        s_all.append(jnp.concatenate(parts, axis=1) + slope_ref[h][:, 0:1] * kpos)
    alphas, ps = [], []
    for h in heads:
        m_old = m_s[h]
        m_new = jnp.maximum(m_old, jnp.max(s_all[h], axis=-1, keepdims=True))
        alpha = jnp.exp2(m_old - m_new)
        p = jnp.exp2(s_all[h] - _lane_tile(m_new, width))
        l_s[h] = alpha * l_s[h] + jnp.sum(p, axis=-1, keepdims=True)
        m_s[h] = m_new
        alphas.append(alpha)
        ps.append(p.astype(BF16))
    for h in heads:
        pv = None
        for j in range(ppb):
            t = jnp.dot(ps[h][:, j * page:(j + 1) * page],
                        v_refs[j][pl.ds(h, page, stride=n_heads), :].astype(BF16),
                        preferred_element_type=F32)
            pv = t if pv is None else pv + t
        acc_s[h] = alphas[h] * acc_s[h] + pv

    @pl.when(si == pl.num_programs(1) - 1)
    def _():
        lam = _lam_value(lq1_ref[...], lk1_ref[...], lq2_ref[...], lk2_ref[...], lam_init)
        kn = kn_ref[0]
        vn = vn_ref[0]
        row_n = lax.broadcasted_iota(jnp.int32, (2 * tn, tn), 0)
        col_n = lax.broadcasted_iota(jnp.int32, (2 * tn, tn), 1)
        causal = jnp.where(row_n >= tn, row_n - tn, row_n) >= col_n
        outs = []
        for h in heads:
            s = scores(h, lambda m: kn[:, m * dh:(m + 1) * dh].astype(BF16), NT_DIMS)
            s = jnp.where(causal, s + slope_ref[h][:, 0:1] * col_n.astype(F32), NEG_BIG)
            m_old = m_s[h]
            m_new = jnp.maximum(m_old, jnp.max(s, axis=-1, keepdims=True))
            alpha = jnp.exp2(m_old - m_new)
            p = jnp.exp2(s - m_new[:, 0:tn])
            l_fin = alpha * l_s[h] + jnp.sum(p, axis=-1, keepdims=True)
            acc = alpha * acc_s[h] + jnp.dot(p.astype(BF16), vn[:, h * hw:(h + 1) * hw].astype(BF16),
                                             preferred_element_type=F32)
            outs.append(_diff_finish(acc / l_fin, tn, lam, sw_ref[...], lam_init))
        o_ref[0] = jnp.concatenate(outs, axis=1)


def _sample_attention(q, k_new, v_new, cache_kt, cache_v, page_table, layer, slopes, lq1, lk1, lq2, lk2, subln_w,
                      lam_init):
    b, tn, w = q.shape
    _, _, maps, dh, page = cache_kt.shape
    n_heads = maps // 2
    hw = 2 * dh
    assert hw == V7X_LANES, "softmax statistics are kept lane-replicated at the head width"
    n_pages = page_table.shape[1]
    ppb = math.gcd(n_pages, 8)
    n_steps = n_pages // ppb
    pt_flat = page_table.reshape(-1)

    def page_index(j, nd):
        return lambda bi, si, pt: (layer, pt[bi * n_pages + si * ppb + j]) + (0,) * nd

    k_specs = [pl.BlockSpec((None, None, maps, dh, page), page_index(j, 3)) for j in range(ppb)]
    v_specs = [pl.BlockSpec((None, None, page * n_heads, hw), page_index(j, 2)) for j in range(ppb)]
    new_spec = pl.BlockSpec((1, tn, w), lambda bi, si, pt: (bi, 0, 0))
    small = lambda n: pl.BlockSpec((1, n), lambda bi, si, pt: (0, 0))
    grid_spec = pltpu.PrefetchScalarGridSpec(
        num_scalar_prefetch=1,
        grid=(b, n_steps),
        in_specs=[new_spec, new_spec, new_spec,
                  pl.BlockSpec((n_heads, 1, V7X_LANES), lambda bi, si, pt: (0, 0, 0)),
                  small(dh), small(dh), small(dh), small(dh), small(hw)] + k_specs + v_specs,
        out_specs=new_spec,
        scratch_shapes=[pltpu.VMEM((n_heads, 2 * tn, hw), F32), pltpu.VMEM((n_heads, 2 * tn, hw), F32),
                        pltpu.VMEM((n_heads, 2 * tn, hw), F32)],
    )
    return pl.pallas_call(
        functools.partial(_sattn_body, ppb=ppb, page=page, n_heads=n_heads, tn=tn, dh=dh,
                          past_len=n_pages * page, lam_init=lam_init),
        grid_spec=grid_spec,
        out_shape=jax.ShapeDtypeStruct((b, tn, w), F32),
        compiler_params=_params(("parallel", "arbitrary")),
        name="sample_attention",
    )(pt_flat, q, k_new, v_new, slopes, lq1, lk1, lq2, lk2, subln_w,
      *([cache_kt] * ppb), *([cache_v] * ppb))


def _merge_body(x_ref, ya_ref, yb_ref, yc_ref, gt_ref, wa_ref, wb_ref, wc_ref, wo_ref, o_ref, *, d):
    gates = gt_ref[...]
    merged = jax.nn.sigmoid(gates[:, 0:d]) * jnp.dot(ya_ref[...].astype(BF16), wa_ref[...], preferred_element_type=F32)
    merged = merged + jax.nn.sigmoid(gates[:, d:2 * d]) * jnp.dot(
        yb_ref[...].astype(BF16), wb_ref[...], preferred_element_type=F32)
    merged = merged + jax.nn.sigmoid(gates[:, 2 * d:3 * d]) * jnp.dot(
        yc_ref[...].astype(BF16), wc_ref[...], preferred_element_type=F32)
    o_ref[...] = x_ref[...] + jnp.dot(merged.astype(BF16), wo_ref[...], preferred_element_type=F32)


def _merge(x, ya, yb, yc, gates, wa, wb, wc, wo):
    m, d = x.shape
    tm = min(m, 512)
    tile = lambda n: pl.BlockSpec((tm, n), lambda i: (i, 0))
    return pl.pallas_call(
        functools.partial(_merge_body, d=d),
        grid=(m // tm,),
        in_specs=[tile(d), tile(ya.shape[1]), tile(yb.shape[1]), tile(yc.shape[1]), tile(3 * d),
                  _const_spec(wa.shape), _const_spec(wb.shape), _const_spec(wc.shape), _const_spec(wo.shape)],
        out_specs=tile(d),
        out_shape=jax.ShapeDtypeStruct((m, d), F32),
        compiler_params=_params(("parallel",)),
        name="merge",
    )(x, ya, yb, yc, gates, wa, wb, wc, wo)


def _prep_layer_weights(l, W, dims):
    db, r_w, r_a = dims["db"], dims["r_w"], dims["r_a"]
    row = lambda a: a.reshape(1, -1)

    def ffn(prefix):
        return (row(W[prefix + "_norm"][l]), W[prefix + "_w_gate"][l].astype(BF16),
                W[prefix + "_w_up"][l].astype(BF16), W[prefix + "_w_down"][l].astype(BF16))

    zeros_w = jnp.zeros((r_a, db), BF16)
    zeros_a = jnp.zeros((r_w, db), BF16)
    head = jnp.arange(db) // dims["hd"]
    return dict(
        ffn1=ffn("ffn1"), ffn2=ffn("ffn2"),
        mix_norm=row(W["mix_norm"][l]), w_in=W["w_in"][l].astype(BF16),
        conv_w=W["conv_w"][l],
        mu=row(W["tm_mu"][l]),
        wpad=jnp.concatenate([W["w_up_w"][l].astype(BF16), zeros_w], axis=0),
        apad=jnp.concatenate([zeros_a, W["a_up_w"][l].astype(BF16)], axis=0),
        w0=row(W["w0"][l]), a0=row(W["a0"][l]), gup=W["g_up_w"][l].astype(BF16),
        k_k=row(W["k_k"][l]), k_a=row(W["k_a"][l]), r_k=row(W["r_k"][l]),
        lnx_w=row(W["lnx_w"][l]), lnx_b=row(W["lnx_b"][l]),
        ones_bd=(head[:, None] == head[None, :]).astype(BF16),
        lq1=row(W["lam_q1"][l]), lk1=row(W["lam_k1"][l]), lq2=row(W["lam_q2"][l]), lk2=row(W["lam_k2"][l]),
        subln_w=row(W["subln_w"][l]),
        wa=W["w_branch_a"][l].astype(BF16), wb=W["w_branch_b"][l].astype(BF16),
        wc=W["w_branch_c"][l].astype(BF16), wo=W["w_out"][l].astype(BF16),
    )


def _run_trunk(x, layers, final_w, conv0, shift0, wkv0, attend, dims):
    b, t, d = x.shape
    m = b * t
    da, db, ns = dims["da"], dims["db"], dims["ns"]
    qw, kw, dc = dims["qw"], dims["kw"], dims["dc"]
    widths = (3 * da, ns, qw, kw, dc, 3 * d)
    depth = len(layers)
    xf = x.reshape(m, d)
    k_rows, v_rows, wkv_out, shift_out, conv_out = [], [], [], [], []
    for l, lw in enumerate(layers):
        xf = _ffn(xf, *lw["ffn1"])
        cin, z, q, k, v, gates = _mixin(xf, lw["mix_norm"], lw["w_in"], widths, 2, dims["dh"] ** -0.5 * LOG2_E)
        y_a, conv_new = _short_conv(cin.reshape(b, t, 3 * da), conv0[l], lw["conv_w"])
        r_, lw_, k_, v_, a_, b_, g_, shift_new = _rwkv_pre(
            z.reshape(b, t, ns), shift0[l].reshape(b, 1, ns), lw["mu"], lw["wpad"], lw["w0"], lw["apad"], lw["a0"],
            lw["gup"], lw["k_k"], lw["k_a"], lw["ones_bd"], db)
        y_b, wkv_new = _rwkv_scan(r_, lw_, k_, v_, a_, b_, g_, wkv0[l], lw["r_k"], lw["lnx_w"], lw["lnx_b"])
        lam_init = 0.8 - 0.6 * math.exp(-0.3 * l)
        y_c = attend(l, q.reshape(b, t, qw), k.reshape(b, t, kw), v.reshape(b, t, dc), lw, lam_init)
        xf = _merge(xf, y_a.reshape(m, da), y_b.reshape(m, db), y_c.reshape(m, dc), gates,
                    lw["wa"], lw["wb"], lw["wc"], lw["wo"])
        xf = _ffn(xf, *lw["ffn2"], final_w=final_w if l == depth - 1 else None)
        k_rows.append(k.reshape(b, t, 2 * dims["hc"], dims["dh"]))
        v_rows.append(v.reshape(b, t, dims["hc"], 2 * dims["dh"]))
        wkv_out.append(wkv_new)
        shift_out.append(shift_new.reshape(b, ns))
        conv_out.append(conv_new)
    return (xf.reshape(b, t, d), jnp.stack(k_rows), jnp.stack(v_rows), jnp.stack(wkv_out),
            jnp.stack(shift_out), jnp.stack(conv_out))


def kernel(x_prompt, x_sample, cache_k, cache_v, state_wkv, state_shift, state_conv, page_table, ffn1_norm, ffn1_w_gate, ffn1_w_up, ffn1_w_down, mix_norm, w_in, conv_w, tm_mu, w_up_w, w0, a_up_w, a0, g_up_w, k_k, k_a, r_k, lnx_w, lnx_b, lam_q1, lam_k1, lam_q2, lam_k2, subln_w, w_branch_a, w_branch_b, w_branch_c, w_out, ffn2_norm, ffn2_w_gate, ffn2_w_up, ffn2_w_down, final_norm):
    W = dict(ffn1_norm=ffn1_norm, ffn1_w_gate=ffn1_w_gate, ffn1_w_up=ffn1_w_up, ffn1_w_down=ffn1_w_down,
             mix_norm=mix_norm, w_in=w_in, conv_w=conv_w, tm_mu=tm_mu, w_up_w=w_up_w, w0=w0,
             a_up_w=a_up_w, a0=a0, g_up_w=g_up_w, k_k=k_k, k_a=k_a, r_k=r_k, lnx_w=lnx_w, lnx_b=lnx_b,
             lam_q1=lam_q1, lam_k1=lam_k1, lam_q2=lam_q2, lam_k2=lam_k2, subln_w=subln_w,
             w_branch_a=w_branch_a, w_branch_b=w_branch_b, w_branch_c=w_branch_c, w_out=w_out,
             ffn2_norm=ffn2_norm, ffn2_w_gate=ffn2_w_gate, ffn2_w_up=ffn2_w_up, ffn2_w_down=ffn2_w_down)
    depth = w_in.shape[0]
    d = x_prompt.shape[-1]
    _, n_pool, page, maps, dh = cache_k.shape
    hc = maps // 2
    nh, hd = r_k.shape[1], r_k.shape[2]
    db = nh * hd
    da = conv_w.shape[-1]
    assert conv_w.shape[1] == 3, "short conv kernel is written for width 3"
    r_w, r_a, r_g = w_up_w.shape[1], a_up_w.shape[1], g_up_w.shape[1]
    ns = state_shift.shape[-1]
    assert ns == 3 * db + r_w + r_a + r_g
    assert (r_w + r_a) % V7X_LANES == 0 and r_g % V7X_LANES == 0
    dims = dict(d=d, d_ff=ffn1_w_gate.shape[-1], da=da, db=db, hd=hd, ns=ns, r_w=r_w, r_a=r_a,
                qw=maps * dh, kw=maps * dh, dc=hc * 2 * dh, hc=hc, dh=dh)
    layers = [_prep_layer_weights(l, W, dims) for l in range(depth)]
    final_w = final_norm.reshape(1, d)
    slopes = jnp.broadcast_to(
        jnp.asarray([LOG2_E * 2.0 ** (-8.0 * (h + 1) / hc) for h in range(hc)], F32)[:, None, None],
        (hc, 1, V7X_LANES))
    cache_kt = jnp.transpose(cache_k, (0, 1, 3, 4, 2))
    cache_v2 = cache_v.reshape(depth, n_pool, page * hc, 2 * dh)

    def attend_prompt(l, q, k, v, lw, lam_init):
        return _prompt_attention(q, k, v, slopes, lw["lq1"], lw["lk1"], lw["lq2"], lw["lk2"], lw["subln_w"],
                                 lam_init, hc)

    def attend_sample(l, q, k, v, lw, lam_init):
        return _sample_attention(q, k, v, cache_kt, cache_v2, page_table, l, slopes, lw["lq1"], lw["lk1"],
                                 lw["lq2"], lw["lk2"], lw["subln_w"], lam_init)

    bp = x_prompt.shape[0]
    conv0 = jnp.zeros((depth, bp, 2, da), F32)
    shift0 = jnp.zeros((depth, bp, ns), F32)
    wkv0 = jnp.zeros((depth, bp, nh, hd, hd), F32)
    y_p, k_p, v_p, wkv_p, shift_p, conv_p = _run_trunk(
        x_prompt, layers, final_w, conv0, shift0, wkv0, attend_prompt, dims)
    y_s, k_s, v_s, wkv_s, shift_s, conv_s = _run_trunk(
        x_sample, layers, final_w, state_conv, state_shift, state_wkv, attend_sample, dims)
    return (y_p, y_s, k_p, v_p, k_s, v_s, wkv_p, wkv_s, shift_p, shift_s, conv_p, conv_s)
```

```python
import functools
import math

import jax
import jax.numpy as jnp
from jax import lax
from jax.experimental import pallas as pl
from jax.experimental.pallas import tpu as pltpu

F32 = jnp.float32
BF16 = jnp.bfloat16

NORM_EPS = 1e-6
LNX_EPS = 64e-5
SUBLN_EPS = 1e-5
KK_EPS = 1e-24
NEG_BIG = -1e30
LOG2_E = math.log2(math.e)

V7X_LANES = 128
V7X_SUBLANES = 8
V7X_VMEM_LIMIT_BYTES = 56 * 1024 * 1024

NT_DIMS = (((1,), (1,)), ((), ()))
NN_DIMS = (((1,), (0,)), ((), ()))
TN_DIMS = (((0,), (0,)), ((), ()))


def _params(semantics):
    return pltpu.CompilerParams(dimension_semantics=semantics, vmem_limit_bytes=V7X_VMEM_LIMIT_BYTES)


def _const_spec(shape):
    nd = len(shape)
    return pl.BlockSpec(shape, lambda *_: (0,) * nd, pipeline_mode=pl.Buffered(1))


def _rms(x, w, eps):
    ms = jnp.mean(x * x, axis=-1, keepdims=True)
    return x * lax.rsqrt(ms + eps) * w


def _split_bf16(x, n):
    pieces = []
    rem = x
    for i in range(n):
        p = rem.astype(BF16)
        pieces.append(p)
        if i + 1 < n:
            rem = rem - p.astype(F32)
    return pieces


def _mm(a, b, dims=NN_DIMS, pa=1, pb=1):
    a_p = _split_bf16(a, pa)
    b_p = _split_bf16(b, pb)
    order = max(pa, pb)
    acc = None
    for i, ai in enumerate(a_p):
        for j, bj in enumerate(b_p):
            if i + j >= order:
                continue
            t = lax.dot_general(ai, bj, dims, preferred_element_type=F32)
            acc = t if acc is None else acc + t
    return acc


FFN_CHUNK = 256


def _ffn_body(x_ref, nw_ref, wg_ref, wu_ref, wd_ref, *rest, n_chunks, fc, final):
    if final:
        fw_ref, o_ref, acc_ref = rest
    else:
        o_ref, acc_ref = rest
    x = x_ref[...]
    h = _rms(x, nw_ref[...], NORM_EPS).astype(BF16)
    for c in range(n_chunks):
        cs = slice(c * fc, (c + 1) * fc)
        g = jnp.dot(h, wg_ref[:, cs], preferred_element_type=F32)
        u = jnp.dot(h, wu_ref[:, cs], preferred_element_type=F32)
        act = (g * jax.nn.sigmoid(g) * u).astype(BF16)
        d = jnp.dot(act, wd_ref[cs, :], preferred_element_type=F32)
        if c == 0:
            acc_ref[...] = d
        else:
            acc_ref[...] += d
    y = x + 0.5 * acc_ref[...]
    if final:
        y = _rms(y, fw_ref[...], NORM_EPS)
    o_ref[...] = y


def _ffn(x, norm_w, wg, wu, wd, final_w=None):
    m, d = x.shape
    d_ff = wg.shape[1]
    fc = FFN_CHUNK if d_ff % FFN_CHUNK == 0 else V7X_LANES
    n_chunks = d_ff // fc
    tm = min(m, 512)
    final = final_w is not None
    in_specs = [
        pl.BlockSpec((tm, d), lambda i: (i, 0)),
        _const_spec((1, d)),
        _const_spec((d, d_ff)),
        _const_spec((d, d_ff)),
        _const_spec((d_ff, d)),
    ]
    args = [x, norm_w, wg, wu, wd]
    if final:
        in_specs.append(_const_spec((1, d)))
        args.append(final_w)
    return pl.pallas_call(
        functools.partial(_ffn_body, n_chunks=n_chunks, fc=fc, final=final),
        grid=(m // tm,),
        in_specs=in_specs,
        out_specs=pl.BlockSpec((tm, d), lambda i: (i, 0)),
        out_shape=jax.ShapeDtypeStruct((m, d), F32),
        scratch_shapes=[pltpu.VMEM((tm, d), F32)],
        compiler_params=_params(("parallel",)),
        name="ffn",
    )(*args)


def _mixin_body(x_ref, nw_ref, w_ref, *out_refs, widths, q_index, q_scale, col_tile):
    h = _rms(x_ref[...], nw_ref[...], NORM_EPS).astype(BF16)
    start = 0
    for idx, (ref, width) in enumerate(zip(out_refs, widths)):
        for c0 in range(0, width, col_tile):
            wd = min(col_tile, width - c0)
            r = jnp.dot(h, w_ref[:, start + c0:start + c0 + wd], preferred_element_type=F32)
            if idx == q_index:
                r = r * q_scale
            ref[:, c0:c0 + wd] = r
        start += width


def _mixin(x, norm_w, w_in, widths, q_index, q_scale):
    m, d = x.shape
    n_in = w_in.shape[1]
    tm = min(m, 512)
    return pl.pallas_call(
        functools.partial(_mixin_body, widths=widths, q_index=q_index, q_scale=q_scale, col_tile=512),
        grid=(m // tm,),
        in_specs=[pl.BlockSpec((tm, d), lambda i: (i, 0)), _const_spec((1, d)), _const_spec((d, n_in))],
        out_specs=[pl.BlockSpec((tm, w), lambda i: (i, 0)) for w in widths],
        out_shape=[jax.ShapeDtypeStruct((m, w), F32) for w in widths],
        compiler_params=_params(("parallel",)),
        name="mixin",
    )(x, norm_w, w_in)


def _conv_body(cin_ref, st_ref, cw_ref, ya_ref, cn_ref, buf, *, tt, da):
    @pl.when(pl.program_id(1) == 0)
    def _():
        buf[pl.ds(6, 2), :] = st_ref[0]

    c = cin_ref[0]
    xin = c[:, 0:da]
    gb = c[:, da:2 * da]
    gc = c[:, 2 * da:3 * da]
    u = gc * xin
    buf[pl.ds(8, tt), :] = u
    cw = cw_ref[...]
    y = buf[pl.ds(6, tt), :] * cw[0:1, :]
    y = y + buf[pl.ds(7, tt), :] * cw[1:2, :]
    y = y + u * cw[2:3, :]
    ya_ref[0] = gb * y
    tail = buf[pl.ds(tt + 6, 2), :]
    cn_ref[0] = tail
    buf[pl.ds(6, 2), :] = tail


def _short_conv(cin, conv_state, conv_w):
    b, t, w3 = cin.shape
    da = w3 // 3
    tt = min(t, 512)
    return pl.pallas_call(
        functools.partial(_conv_body, tt=tt, da=da),
        grid=(b, t // tt),
        in_specs=[
            pl.BlockSpec((1, tt, w3), lambda i, j: (i, j, 0)),
            pl.BlockSpec((1, 2, da), lambda i, j: (i, 0, 0)),
            pl.BlockSpec((3, da), lambda i, j: (0, 0)),
        ],
        out_specs=[
            pl.BlockSpec((1, tt, da), lambda i, j: (i, j, 0)),
            pl.BlockSpec((1, 2, da), lambda i, j: (i, 0, 0)),
        ],
        out_shape=[jax.ShapeDtypeStruct((b, t, da), F32), jax.ShapeDtypeStruct((b, 2, da), F32)],
        scratch_shapes=[pltpu.VMEM((tt + 8, da), F32)],
        compiler_params=_params(("parallel", "arbitrary")),
        name="short_conv",
    )(cin, conv_state, conv_w)


def _pre_body(z_ref, ss_ref, mu_ref, wpad_ref, w0_ref, apad_ref, a0_ref, gup_ref, kk_ref, ka_ref, ones_ref,
              r_ref, lw_ref, k_ref, v_ref, a_ref, b_ref, g_ref, sn_ref, buf, *, tt, db, r_lo, r_g):
    @pl.when(pl.program_id(1) == 0)
    def _():
        buf[pl.ds(7, 1), :] = ss_ref[0]

    z = z_ref[0]
    buf[pl.ds(8, tt), :] = z
    zp = buf[pl.ds(7, tt), :]
    zm = z + (zp - z) * mu_ref[...]
    last = buf[pl.ds(tt + 7, 1), :]
    sn_ref[0] = last
    buf[pl.ds(7, 1), :] = last

    r = zm[:, 0:db]
    k = zm[:, db:2 * db]
    v = zm[:, 2 * db:3 * db]
    lo = zm[:, 3 * db:3 * db + r_lo]
    g_lo = zm[:, 3 * db + r_lo:3 * db + r_lo + r_g]
    w_part = jnp.dot(jnp.tanh(lo).astype(BF16), wpad_ref[...], preferred_element_type=F32)
    a_part = jnp.dot(lo.astype(BF16), apad_ref[...], preferred_element_type=F32)
    xw = -(w0_ref[...] + w_part)
    softplus = jnp.maximum(xw, 0.0) + jnp.log(1.0 + jnp.exp(-jnp.abs(xw)))
    w_log = -softplus - 0.5
    lw = -jnp.exp(w_log)
    a = jax.nn.sigmoid(a0_ref[...] + a_part)
    g = jnp.dot(jax.nn.sigmoid(g_lo).astype(BF16), gup_ref[...], preferred_element_type=F32)
    kk = k * kk_ref[...]
    ss = _mm(kk * kk, ones_ref[...].astype(F32), pa=2, pb=1)
    kkn = kk * lax.rsqrt(jnp.maximum(ss, KK_EPS))
    r_ref[0] = r
    lw_ref[0] = lw
    k_ref[0] = k * (1.0 + (a - 1.0) * ka_ref[...])
    v_ref[0] = v
    a_ref[0] = -kkn
    b_ref[0] = kkn * a
    g_ref[0] = g


def _rwkv_pre(z, shift_state, mu, wpad, w0, apad, a0, gup, k_k, k_a, ones_bd, db):
    b, t, ns = z.shape
    r_lo = wpad.shape[0]
    r_g = gup.shape[0]
    tt = min(t, 512)
    row = lambda n: pl.BlockSpec((1, n), lambda i, j: (0, 0))
    full = lambda s: pl.BlockSpec(s, lambda i, j: (0, 0))
    seq = lambda n: pl.BlockSpec((1, tt, n), lambda i, j: (i, j, 0))
    return pl.pallas_call(
        functools.partial(_pre_body, tt=tt, db=db, r_lo=r_lo, r_g=r_g),
        grid=(b, t // tt),
        in_specs=[
            seq(ns),
            pl.BlockSpec((1, 1, ns), lambda i, j: (i, 0, 0)),
            row(ns), full((r_lo, db)), row(db), full((r_lo, db)), row(db), full((r_g, db)),
            row(db), row(db), full((db, db)),
        ],
        out_specs=[seq(db)] * 7 + [pl.BlockSpec((1, 1, ns), lambda i, j: (i, 0, 0))],
        out_shape=[jax.ShapeDtypeStruct((b, t, db), F32)] * 7 + [jax.ShapeDtypeStruct((b, 1, ns), F32)],
        scratch_shapes=[pltpu.VMEM((tt + 8, ns), F32)],
        compiler_params=_params(("parallel", "arbitrary")),
        name="rwkv_pre",
    )(z, shift_state, mu, wpad, w0, apad, a0, gup, k_k, k_a, ones_bd)


SCAN_SUB = 16
SCAN_P = 1


def _mm_each(a_list, b_list, dims=NN_DIMS):
    return [_mm(a, b, dims=dims, pa=SCAN_P, pb=SCAN_P) for a, b in zip(a_list, b_list)]


def _neumann_inverse_each(l_list, n):
    c = l_list[0].shape[0]
    eye = (lax.broadcasted_iota(jnp.int32, (c, c), 0) == lax.broadcasted_iota(jnp.int32, (c, c), 1)).astype(F32)
    x = [eye + l for l in l_list]
    p = l_list
    span = 2
    while span < n:
        p = _mm_each(p, p)
        xp = _mm_each(x, p)
        x = [xi + d for xi, d in zip(x, xp)]
        span *= 2
    return x


def _unit_lower_inverse_each(l_list):
    c = l_list[0].shape[0]
    if c <= SCAN_SUB:
        return _neumann_inverse_each(l_list, c)
    rows = lax.broadcasted_iota(jnp.int32, (c, c), 0) // SCAN_SUB
    cols = lax.broadcasted_iota(jnp.int32, (c, c), 1) // SCAN_SUB
    same = rows == cols
    t_diag = _neumann_inverse_each([jnp.where(same, l, 0.0) for l in l_list], SCAN_SUB)
    z = _mm_each(t_diag, [jnp.where(same, 0.0, l) for l in l_list])
    nblk = c // SCAN_SUB
    factors = []
    span = 1
    while span < nblk:
        factors.append(z)
        span *= 2
        if span < nblk:
            z = _mm_each(z, z)
    out = t_diag
    for f in reversed(factors):
        d = _mm_each(f, out)
        out = [o + di for o, di in zip(out, d)]
    return out


def _scan_body(r_ref, lw_ref, k_ref, v_ref, a_ref, b_ref, g_ref, s0_ref, rk_ref, lnw_ref, lnb_ref,
               o_ref, sT_ref, st_ref, *, c, bb, nh, hd):
    ci = pl.program_id(1)
    units = [(bi, h) for bi in range(bb) for h in range(nh)]

    @pl.when(ci == 0)
    def _():
        for bi, h in units:
            st_ref[bi, h] = s0_ref[bi, h].T

    rows = lax.broadcasted_iota(jnp.int32, (c, c), 0)
    cols = lax.broadcasted_iota(jnp.int32, (c, c), 1)
    tri = (rows >= cols).astype(F32)
    rows2 = lax.broadcasted_iota(jnp.int32, (c, 2 * c), 0)
    cols2 = lax.broadcasted_iota(jnp.int32, (c, 2 * c), 1)
    cols2 = jnp.where(cols2 >= c, cols2 - c, cols2)
    strict2 = rows2 > cols2
    incl2 = rows2 >= cols2
    ones_cols = jnp.ones((c, hd), F32)
    zeros_cv = jnp.zeros((c, hd), F32)

    ar, bk, bk_rem, v_u, decay = [], [], [], [], []
    for bi in range(bb):
        r = r_ref[bi]
        lw = lw_ref[bi]
        k = k_ref[bi]
        a = a_ref[bi]
        b = b_ref[bi]
        v = v_ref[bi]
        cum = _mm(tri, lw, pa=1, pb=3)
        cum_last = cum[c - 1:c, :]
        e_neg = jnp.exp(-cum)
        e_rem = jnp.exp(cum_last - cum)
        at = a * jnp.exp(cum - lw)
        rt = r * jnp.exp(cum)
        bt = b * e_neg
        kt = k * e_neg
        bh = b * e_rem
        kh = k * e_rem
        cum_cols = _mm(lw, ones_cols, dims=TN_DIMS, pa=3, pb=1)
        for h in range(nh):
            sl = slice(h * hd, (h + 1) * hd)
            ar.append(jnp.concatenate([at[:, sl], rt[:, sl]], axis=0))
            bk.append(jnp.concatenate([bt[:, sl], kt[:, sl]], axis=0))
            bk_rem.append(jnp.concatenate([bh[:, sl], kh[:, sl]], axis=0))
            v_u.append(v[:, sl])
            decay.append(jnp.exp(cum_cols[h * hd:(h + 1) * hd, :]))

    st = [st_ref[bi, h] for bi, h in units]
    m4 = _mm_each(ar, bk, dims=NT_DIMS)
    m_a = [jnp.where(strict2, m[0:c], 0.0) for m in m4]
    m_r = [jnp.where(incl2, m[c:2 * c], 0.0) for m in m4]
    t_inv = _unit_lower_inverse_each([m[:, 0:c] for m in m_a])
    g1 = _mm_each([x[0:c] for x in ar], st)
    g2 = _mm_each(m_a, [jnp.concatenate([zeros_cv, vh], axis=0) for vh in v_u])
    u = _mm_each(t_inv, [x + y for x, y in zip(g1, g2)])
    uv = [jnp.concatenate([ui, vh], axis=0) for ui, vh in zip(u, v_u)]
    y1 = _mm_each([x[c:2 * c] for x in ar], st)
    y2 = _mm_each(m_r, uv)
    st_upd = _mm_each(bk_rem, uv, dims=TN_DIMS)
    for (bi, h), s_old, d, upd in zip(units, st, decay, st_upd):
        st_ref[bi, h] = s_old * d + upd

    rk = rk_ref[...]
    for bi in range(bb):
        r = r_ref[bi]
        k = k_ref[bi]
        rkk = r * k * rk
        yn_parts, bonus_parts = [], []
        for h in range(nh):
            i = bi * nh + h
            sl = slice(h * hd, (h + 1) * hd)
            y = y1[i] + y2[i]
            mean = jnp.mean(y, axis=-1, keepdims=True)
            yc = y - mean
            var = jnp.mean(yc * yc, axis=-1, keepdims=True)
            yn_parts.append(yc * lax.rsqrt(var + LNX_EPS))
            bonus_parts.append(jnp.sum(rkk[:, sl], axis=-1, keepdims=True) * v_u[i])
        yn_all = jnp.concatenate(yn_parts, axis=1)
        bonus_all = jnp.concatenate(bonus_parts, axis=1)
        o_ref[bi] = (yn_all * lnw_ref[...] + lnb_ref[...] + bonus_all) * g_ref[bi]

    @pl.when(ci == pl.num_programs(1) - 1)
    def _():
        for bi, h in units:
            sT_ref[bi, h] = st_ref[bi, h].T


def _rwkv_scan(r, lw, k, v, a, b, g, s0, r_k, lnx_w, lnx_b):
    bsz, t, db = r.shape
    _, nh, hd, _ = s0.shape
    c = min(t, 64)
    bb = 4 if bsz % 4 == 0 else 1
    seq = pl.BlockSpec((bb, c, db), lambda i, j: (i, j, 0))
    row = pl.BlockSpec((1, db), lambda i, j: (0, 0))
    st_spec = pl.BlockSpec((bb, nh, hd, hd), lambda i, j: (i, 0, 0, 0))
    return pl.pallas_call(
        functools.partial(_scan_body, c=c, bb=bb, nh=nh, hd=hd),
        grid=(bsz // bb, t // c),
        in_specs=[seq] * 7 + [st_spec, row, row, row],
        out_specs=[seq, st_spec],
        out_shape=[jax.ShapeDtypeStruct((bsz, t, db), F32), jax.ShapeDtypeStruct((bsz, nh, hd, hd), F32)],
        scratch_shapes=[pltpu.VMEM((bb, nh, hd, hd), F32)],
        compiler_params=_params(("parallel", "arbitrary")),
        name="rwkv_scan",
    )(r, lw, k, v, a, b, g, s0, r_k, lnx_w, lnx_b)


def _lam_value(lq1, lk1, lq2, lk2, lam_init):
    s1 = jnp.sum(lq1 * lk1, axis=-1, keepdims=True)
    s2 = jnp.sum(lq2 * lk2, axis=-1, keepdims=True)
    return jnp.exp(s1) - jnp.exp(s2) + lam_init


def _stack_maps(q, tn, dh):
    q2x = jnp.concatenate([q, q], axis=0)
    row = lax.broadcasted_iota(jnp.int32, q2x.shape, 0)
    lane = lax.broadcasted_iota(jnp.int32, q2x.shape, 1)
    keep = (row < tn) == (lane < dh)
    return jnp.where(keep, q2x, 0.0).astype(BF16)


def _lane_tile(x, width):
    reps = width // V7X_LANES
    return x if reps == 1 else jnp.concatenate([x] * reps, axis=1)


def _diff_finish(o12, tn, lam, subln_w, lam_init):
    o = o12[0:tn] - lam * o12[tn:2 * tn]
    return _rms(o, subln_w, SUBLN_EPS) * (1.0 - lam_init)


PATTN_COLS = 128
PATTN_KEYS = 256
PATTN_HEADS = 4
BIAS_PIECES = 3


def _pattn_body(qi_ref, ki_ref, q_ref, k_ref, v_ref, slope_ref, lq1_ref, lk1_ref, lq2_ref, lk2_ref, sw_ref,
                o_ref, q12_s, ktab_s, s_s, p_s, m_s, l_s, al_s, acc_s, *, tq, dh, hps, pk, lam_init):
    step = pl.program_id(2)
    qi = qi_ref[step]
    ki = ki_ref[step]
    nq2 = 2 * tq
    hw = 2 * dh
    heads = range(hps)

    @pl.when(ki == 0)
    def _():
        lane = lax.broadcasted_iota(jnp.int32, (nq2, V7X_LANES), 1)
        ones = jnp.where(lane < BIAS_PIECES, 1.0, 0.0).astype(BF16)
        klane = lax.broadcasted_iota(jnp.int32, (tq, V7X_LANES), 1)
        kpos = lax.broadcasted_iota(jnp.int32, (tq, V7X_LANES), 0).astype(F32)
        for g in heads:
            q12_s[g] = jnp.concatenate([_stack_maps(q_ref[0][:, g * hw:(g + 1) * hw], tq, dh), ones], axis=1)
            rem = slope_ref[g][:, 0:1] * kpos
            tab = jnp.zeros((tq, V7X_LANES), F32)
            for i in range(BIAS_PIECES):
                piece = rem.astype(BF16).astype(F32)
                tab = jnp.where(klane == i, piece, tab)
                rem = rem - piece
            ktab_s[g] = tab.astype(BF16)
        m_s[...] = jnp.full(m_s.shape, NEG_BIG, F32)
        l_s[...] = jnp.zeros(l_s.shape, F32)
        acc_s[...] = jnp.zeros(acc_s.shape, F32)

    def update(diagonal):
        k_all = k_ref[0]
        v_all = v_ref[0]
        vt_bf = []
        for g in heads:
            k_aug = jnp.concatenate([k_all[:, g * hw:(g + 1) * hw].astype(BF16), ktab_s[g]], axis=1)
            s_s[g] = lax.dot_general(k_aug, q12_s[g], NT_DIMS, preferred_element_type=F32)
            vt_bf.append(v_all[:, g * hw:(g + 1) * hw].T.astype(BF16))
        base = ((ki - qi) * tq).astype(F32)
        for k0 in range(0, tq, pk):
            for g in heads:
                shift = slope_ref[g][:, 0:1] * base
                for c0 in range(0, nq2, PATTN_COLS):
                    q0 = c0 % tq
                    cols = pl.ds(c0, PATTN_COLS)
                    if diagonal and k0 > q0 + PATTN_COLS - 1:
                        p_s[g, :, cols] = jnp.zeros((pk, PATTN_COLS), BF16)
                        al_s[g, :, cols] = jnp.ones((1, PATTN_COLS), F32)
                        continue
                    sv = s_s[g, pl.ds(k0, pk), cols]
                    if diagonal and k0 + pk - 1 > q0:
                        qpos = q0 + lax.broadcasted_iota(jnp.int32, (pk, PATTN_COLS), 1)
                        kpos = k0 + lax.broadcasted_iota(jnp.int32, (pk, PATTN_COLS), 0)
                        sv = jnp.where(qpos >= kpos, sv, NEG_BIG)
                    m_old = m_s[g, :, cols]
                    m_new = jnp.maximum(m_old, jnp.max(sv, axis=0, keepdims=True) + shift)
                    alpha = jnp.exp2(m_old - m_new)
                    p = jnp.exp2(sv - (m_new - shift))
                    m_s[g, :, cols] = m_new
                    l_s[g, :, cols] = alpha * l_s[g, :, cols] + jnp.sum(p, axis=0, keepdims=True)
                    al_s[g, :, cols] = alpha
                    p_s[g, :, cols] = p.astype(BF16)
            for g in heads:
                acc_s[g] = al_s[g] * acc_s[g] + jnp.dot(vt_bf[g][:, k0:k0 + pk], p_s[g],
                                                        preferred_element_type=F32)

    @pl.when(ki < qi)
    def _():
        update(False)

    @pl.when(ki == qi)
    def _():
        update(True)
        lam = _lam_value(lq1_ref[...], lk1_ref[...], lq2_ref[...], lk2_ref[...], lam_init)
        outs = []
        for g in heads:
            o12 = (acc_s[g] / l_s[g]).T
            outs.append(_diff_finish(o12, tq, lam, sw_ref[...], lam_init))
        o_ref[0] = outs[0] if hps == 1 else jnp.concatenate(outs, axis=1)


def _prompt_attention(q, k, v, slopes, lq1, lk1, lq2, lk2, subln_w, lam_init, n_heads):
    b, t, w = q.shape
    hw = w // n_heads
    assert hw == V7X_LANES, "one head's two maps fill exactly one lane tile"
    dh = hw // 2
    tq = min(t, 512)
    nq = t // tq
    hps = math.gcd(n_heads, PATTN_HEADS)
    pk = min(PATTN_KEYS, tq)
    pairs = [(i, j) for i in range(nq) for j in range(i + 1)]
    qi_tbl = jnp.asarray([p[0] for p in pairs], jnp.int32)
    ki_tbl = jnp.asarray([p[1] for p in pairs], jnp.int32)
    qspec = pl.BlockSpec((1, tq, hps * hw), lambda bi, h, s, qt, kt: (bi, qt[s], h))
    kspec = pl.BlockSpec((1, tq, hps * hw), lambda bi, h, s, qt, kt: (bi, kt[s], h))
    small = lambda n: pl.BlockSpec((1, n), lambda bi, h, s, qt, kt: (0, 0))
    row = pltpu.VMEM((hps, 1, 2 * tq), F32)
    grid_spec = pltpu.PrefetchScalarGridSpec(
        num_scalar_prefetch=2,
        grid=(b, n_heads // hps, len(pairs)),
        in_specs=[qspec, kspec, kspec,
                  pl.BlockSpec((hps, 1, V7X_LANES), lambda bi, h, s, qt, kt: (h, 0, 0)),
                  small(dh), small(dh), small(dh), small(dh), small(hw)],
        out_specs=qspec,
        scratch_shapes=[pltpu.VMEM((hps, 2 * tq, 2 * hw), BF16), pltpu.VMEM((hps, tq, V7X_LANES), BF16),
                        pltpu.VMEM((hps, tq, 2 * tq), F32),
                        pltpu.VMEM((hps, pk, 2 * tq), BF16), row, row, row,
                        pltpu.VMEM((hps, hw, 2 * tq), F32)],
    )
    return pl.pallas_call(
        functools.partial(_pattn_body, tq=tq, dh=dh, hps=hps, pk=pk, lam_init=lam_init),
        grid_spec=grid_spec,
        out_shape=jax.ShapeDtypeStruct((b, t, w), F32),
        compiler_params=_params(("parallel", "parallel", "arbitrary")),
        name="prompt_attention",
    )(qi_tbl, ki_tbl, q, k, v, slopes, lq1, lk1, lq2, lk2, subln_w)


SATTN_PAGES = 32


def _sattn_body(pt_ref, q_ref, kn_ref, vn_ref, slope_ref, lq1_ref, lk1_ref, lq2_ref, lk2_ref, sw_ref, *rest,
                ppb, page, n_heads, tn, dh, past_len, lam_init):
    k_refs = rest[:ppb]
    v_refs = rest[ppb:2 * ppb]
    o_ref, m_s, l_s, acc_s = rest[2 * ppb:]
    del pt_ref
    si = pl.program_id(1)
    hw = 2 * dh
    width = ppb * page
    heads = range(n_heads)

    @pl.when(si == 0)
    def _():
        m_s[...] = jnp.full(m_s.shape, NEG_BIG, F32)
        l_s[...] = jnp.zeros(l_s.shape, F32)
        acc_s[...] = jnp.zeros(acc_s.shape, F32)

    q = q_ref[0]
    qm = [q[:, m * dh:(m + 1) * dh].astype(BF16) for m in range(2 * n_heads)]

    def scores(h, key_of_map, dims):
        sa = lax.dot_general(qm[2 * h], key_of_map(2 * h), dims, preferred_element_type=F32)
        sb = lax.dot_general(qm[2 * h + 1], key_of_map(2 * h + 1), dims, preferred_element_type=F32)
        return jnp.concatenate([sa, sb], axis=0)

    kpos = si * width + lax.broadcasted_iota(jnp.int32, (1, width), 1) - past_len
    kpos = kpos.astype(F32)
    s_all = []
    for h in heads:
        parts = [scores(h, lambda m, j=j: k_refs[j][m].astype(BF16), NN_DIMS) for j in range(ppb)]
        s_all.append(jnp.concatenate(parts, axis=1) + slope_ref[h][:, 0:1] * kpos)
    alphas, ps = [], []
    for h in heads:
        m_old = m_s[h]
        m_new = jnp.maximum(m_old, jnp.max(s_all[h], axis=-1, keepdims=True))
        alpha = jnp.exp2(m_old - m_new)
        p = jnp.exp2(s_all[h] - _lane_tile(m_new, width))
        l_s[h] = alpha * l_s[h] + jnp.sum(p, axis=-1, keepdims=True)
        m_s[h] = m_new
        alphas.append(alpha)
        ps.append(p.astype(BF16))
    for h in heads:
        pv = None
        for j in range(ppb):
            t = jnp.dot(ps[h][:, j * page:(j + 1) * page],
                        v_refs[j][pl.ds(h, page, stride=n_heads), :].astype(BF16),
                        preferred_element_type=F32)
            pv = t if pv is None else pv + t
        acc_s[h] = alphas[h] * acc_s[h] + pv

    @pl.when(si == pl.num_programs(1) - 1)
    def _():
        lam = _lam_value(lq1_ref[...], lk1_ref[...], lq2_ref[...], lk2_ref[...], lam_init)
        kn = kn_ref[0]
        vn = vn_ref[0]
        row_n = lax.broadcasted_iota(jnp.int32, (2 * tn, tn), 0)
        col_n = lax.broadcasted_iota(jnp.int32, (2 * tn, tn), 1)
        causal = jnp.where(row_n >= tn, row_n - tn, row_n) >= col_n
        outs = []
        for h in heads:
            s = scores(h, lambda m: kn[:, m * dh:(m + 1) * dh].astype(BF16), NT_DIMS)
            s = jnp.where(causal, s + slope_ref[h][:, 0:1] * col_n.astype(F32), NEG_BIG)
            m_old = m_s[h]
            m_new = jnp.maximum(m_old, jnp.max(s, axis=-1, keepdims=True))
            alpha = jnp.exp2(m_old - m_new)
            p = jnp.exp2(s - m_new[:, 0:tn])
            l_fin = alpha * l_s[h] + jnp.sum(p, axis=-1, keepdims=True)
            acc = alpha * acc_s[h] + jnp.dot(p.astype(BF16), vn[:, h * hw:(h + 1) * hw].astype(BF16),
                                             preferred_element_type=F32)
            outs.append(_diff_finish(acc / l_fin, tn, lam, sw_ref[...], lam_init))
        o_ref[0] = jnp.concatenate(outs, axis=1)


def _sample_attention(q, k_new, v_new, cache_kt, cache_v, page_table, layer, slopes, lq1, lk1, lq2, lk2, subln_w,
                      lam_init):
    b, tn, w = q.shape
    _, _, maps, dh, page = cache_kt.shape
    n_heads = maps // 2
    hw = 2 * dh
    assert hw == V7X_LANES, "softmax statistics are kept lane-replicated at the head width"
    n_pages = page_table.shape[1]
    ppb = math.gcd(n_pages, SATTN_PAGES)
    n_steps = n_pages // ppb
    pt_flat = page_table.reshape(-1)

    def page_index(j, nd):
        return lambda bi, si, pt: (layer, pt[bi * n_pages + si * ppb + j]) + (0,) * nd

    k_specs = [pl.BlockSpec((None, None, maps, dh, page), page_index(j, 3)) for j in range(ppb)]
    v_specs = [pl.BlockSpec((None, None, page * n_heads, hw), page_index(j, 2)) for j in range(ppb)]
    new_spec = pl.BlockSpec((1, tn, w), lambda bi, si, pt: (bi, 0, 0))
    small = lambda n: pl.BlockSpec((1, n), lambda bi, si, pt: (0, 0))
    grid_spec = pltpu.PrefetchScalarGridSpec(
        num_scalar_prefetch=1,
        grid=(b, n_steps),
        in_specs=[new_spec, new_spec, new_spec,
                  pl.BlockSpec((n_heads, 1, V7X_LANES), lambda bi, si, pt: (0, 0, 0)),
                  small(dh), small(dh), small(dh), small(dh), small(hw)] + k_specs + v_specs,
        out_specs=new_spec,
        scratch_shapes=[pltpu.VMEM((n_heads, 2 * tn, hw), F32), pltpu.VMEM((n_heads, 2 * tn, hw), F32),
                        pltpu.VMEM((n_heads, 2 * tn, hw), F32)],
    )
    return pl.pallas_call(
        functools.partial(_sattn_body, ppb=ppb, page=page, n_heads=n_heads, tn=tn, dh=dh,
                          past_len=n_pages * page, lam_init=lam_init),
        grid_spec=grid_spec,
        out_shape=jax.ShapeDtypeStruct((b, tn, w), F32),
        compiler_params=_params(("parallel", "arbitrary")),
        name="sample_attention",
    )(pt_flat, q, k_new, v_new, slopes, lq1, lk1, lq2, lk2, subln_w,
      *([cache_kt] * ppb), *([cache_v] * ppb))


def _merge_body(x_ref, ya_ref, yb_ref, yc_ref, gt_ref, wa_ref, wb_ref, wc_ref, wo_ref, o_ref, *, d):
    gates = gt_ref[...]
    merged = jax.nn.sigmoid(gates[:, 0:d]) * jnp.dot(ya_ref[...].astype(BF16), wa_ref[...], preferred_element_type=F32)
    merged = merged + jax.nn.sigmoid(gates[:, d:2 * d]) * jnp.dot(
        yb_ref[...].astype(BF16), wb_ref[...], preferred_element_type=F32)
    merged = merged + jax.nn.sigmoid(gates[:, 2 * d:3 * d]) * jnp.dot(
        yc_ref[...].astype(BF16), wc_ref[...], preferred_element_type=F32)
    o_ref[...] = x_ref[...] + jnp.dot(merged.astype(BF16), wo_ref[...], preferred_element_type=F32)


def _merge(x, ya, yb, yc, gates, wa, wb, wc, wo):
    m, d = x.shape
    tm = min(m, 512)
    tile = lambda n: pl.BlockSpec((tm, n), lambda i: (i, 0))
    return pl.pallas_call(
        functools.partial(_merge_body, d=d),
        grid=(m // tm,),
        in_specs=[tile(d), tile(ya.shape[1]), tile(yb.shape[1]), tile(yc.shape[1]), tile(3 * d),
                  _const_spec(wa.shape), _const_spec(wb.shape), _const_spec(wc.shape), _const_spec(wo.shape)],
        out_specs=tile(d),
        out_shape=jax.ShapeDtypeStruct((m, d), F32),
        compiler_params=_params(("parallel",)),
        name="merge",
    )(x, ya, yb, yc, gates, wa, wb, wc, wo)


def _prep_layer_weights(l, W, dims):
    db, r_w, r_a = dims["db"], dims["r_w"], dims["r_a"]
    row = lambda a: a.reshape(1, -1)

    def ffn(prefix):
        return (row(W[prefix + "_norm"][l]), W[prefix + "_w_gate"][l].astype(BF16),
                W[prefix + "_w_up"][l].astype(BF16), W[prefix + "_w_down"][l].astype(BF16))

    zeros_w = jnp.zeros((r_a, db), BF16)
    zeros_a = jnp.zeros((r_w, db), BF16)
    head = jnp.arange(db) // dims["hd"]
    return dict(
        ffn1=ffn("ffn1"), ffn2=ffn("ffn2"),
        mix_norm=row(W["mix_norm"][l]), w_in=W["w_in"][l].astype(BF16),
        conv_w=W["conv_w"][l],
        mu=row(W["tm_mu"][l]),
        wpad=jnp.concatenate([W["w_up_w"][l].astype(BF16), zeros_w], axis=0),
        apad=jnp.concatenate([zeros_a, W["a_up_w"][l].astype(BF16)], axis=0),
        w0=row(W["w0"][l]), a0=row(W["a0"][l]), gup=W["g_up_w"][l].astype(BF16),
        k_k=row(W["k_k"][l]), k_a=row(W["k_a"][l]), r_k=row(W["r_k"][l]),
        lnx_w=row(W["lnx_w"][l]), lnx_b=row(W["lnx_b"][l]),
        ones_bd=(head[:, None] == head[None, :]).astype(BF16),
        lq1=row(W["lam_q1"][l]), lk1=row(W["lam_k1"][l]), lq2=row(W["lam_q2"][l]), lk2=row(W["lam_k2"][l]),
        subln_w=row(W["subln_w"][l]),
        wa=W["w_branch_a"][l].astype(BF16), wb=W["w_branch_b"][l].astype(BF16),
        wc=W["w_branch_c"][l].astype(BF16), wo=W["w_out"][l].astype(BF16),
    )


def _run_trunk(x, layers, final_w, conv0, shift0, wkv0, attend, dims):
    b, t, d = x.shape
    m = b * t
    da, db, ns = dims["da"], dims["db"], dims["ns"]
    qw, kw, dc = dims["qw"], dims["kw"], dims["dc"]
    widths = (3 * da, ns, qw, kw, dc, 3 * d)
    depth = len(layers)
    xf = x.reshape(m, d)
    k_rows, v_rows, wkv_out, shift_out, conv_out = [], [], [], [], []
    for l, lw in enumerate(layers):
        xf = _ffn(xf, *lw["ffn1"])
        cin, z, q, k, v, gates = _mixin(xf, lw["mix_norm"], lw["w_in"], widths, 2, dims["dh"] ** -0.5 * LOG2_E)
        y_a, conv_new = _short_conv(cin.reshape(b, t, 3 * da), conv0[l], lw["conv_w"])
        r_, lw_, k_, v_, a_, b_, g_, shift_new = _rwkv_pre(
            z.reshape(b, t, ns), shift0[l].reshape(b, 1, ns), lw["mu"], lw["wpad"], lw["w0"], lw["apad"], lw["a0"],
            lw["gup"], lw["k_k"], lw["k_a"], lw["ones_bd"], db)
        y_b, wkv_new = _rwkv_scan(r_, lw_, k_, v_, a_, b_, g_, wkv0[l], lw["r_k"], lw["lnx_w"], lw["lnx_b"])
        lam_init = 0.8 - 0.6 * math.exp(-0.3 * l)
        y_c = attend(l, q.reshape(b, t, qw), k.reshape(b, t, kw), v.reshape(b, t, dc), lw, lam_init)
        xf = _merge(xf, y_a.reshape(m, da), y_b.reshape(m, db), y_c.reshape(m, dc), gates,
                    lw["wa"], lw["wb"], lw["wc"], lw["wo"])
        xf = _ffn(xf, *lw["ffn2"], final_w=final_w if l == depth - 1 else None)
        k_rows.append(k.reshape(b, t, 2 * dims["hc"], dims["dh"]))
        v_rows.append(v.reshape(b, t, dims["hc"], 2 * dims["dh"]))
        wkv_out.append(wkv_new)
        shift_out.append(shift_new.reshape(b, ns))
        conv_out.append(conv_new)
    return (xf.reshape(b, t, d), jnp.stack(k_rows), jnp.stack(v_rows), jnp.stack(wkv_out),
            jnp.stack(shift_out), jnp.stack(conv_out))


def kernel(x_prompt, x_sample, cache_k, cache_v, state_wkv, state_shift, state_conv, page_table, ffn1_norm, ffn1_w_gate, ffn1_w_up, ffn1_w_down, mix_norm, w_in, conv_w, tm_mu, w_up_w, w0, a_up_w, a0, g_up_w, k_k, k_a, r_k, lnx_w, lnx_b, lam_q1, lam_k1, lam_q2, lam_k2, subln_w, w_branch_a, w_branch_b, w_branch_c, w_out, ffn2_norm, ffn2_w_gate, ffn2_w_up, ffn2_w_down, final_norm):
    W = dict(ffn1_norm=ffn1_norm, ffn1_w_gate=ffn1_w_gate, ffn1_w_up=ffn1_w_up, ffn1_w_down=ffn1_w_down,
             mix_norm=mix_norm, w_in=w_in, conv_w=conv_w, tm_mu=tm_mu, w_up_w=w_up_w, w0=w0,
             a_up_w=a_up_w, a0=a0, g_up_w=g_up_w, k_k=k_k, k_a=k_a, r_k=r_k, lnx_w=lnx_w, lnx_b=lnx_b,
             lam_q1=lam_q1, lam_k1=lam_k1, lam_q2=lam_q2, lam_k2=lam_k2, subln_w=subln_w,
             w_branch_a=w_branch_a, w_branch_b=w_branch_b, w_branch_c=w_branch_c, w_out=w_out,
             ffn2_norm=ffn2_norm, ffn2_w_gate=ffn2_w_gate, ffn2_w_up=ffn2_w_up, ffn2_w_down=ffn2_w_down)
    depth = w_in.shape[0]
    d = x_prompt.shape[-1]
    _, n_pool, page, maps, dh = cache_k.shape
    hc = maps // 2
    nh, hd = r_k.shape[1], r_k.shape[2]
    db = nh * hd
    da = conv_w.shape[-1]
    assert conv_w.shape[1] == 3, "short conv kernel is written for width 3"
    r_w, r_a, r_g = w_up_w.shape[1], a_up_w.shape[1], g_up_w.shape[1]
    ns = state_shift.shape[-1]
    assert ns == 3 * db + r_w + r_a + r_g
    assert (r_w + r_a) % V7X_LANES == 0 and r_g % V7X_LANES == 0
    dims = dict(d=d, d_ff=ffn1_w_gate.shape[-1], da=da, db=db, hd=hd, ns=ns, r_w=r_w, r_a=r_a,
                qw=maps * dh, kw=maps * dh, dc=hc * 2 * dh, hc=hc, dh=dh)
    layers = [_prep_layer_weights(l, W, dims) for l in range(depth)]
    final_w = final_norm.reshape(1, d)
    slopes = jnp.broadcast_to(
        jnp.asarray([LOG2_E * 2.0 ** (-8.0 * (h + 1) / hc) for h in range(hc)], F32)[:, None, None],
        (hc, 1, V7X_LANES))
    cache_kt = jnp.transpose(cache_k, (0, 1, 3, 4, 2))
    cache_v2 = cache_v.reshape(depth, n_pool, page * hc, 2 * dh)

    def attend_prompt(l, q, k, v, lw, lam_init):
        return _prompt_attention(q, k, v, slopes, lw["lq1"], lw["lk1"], lw["lq2"], lw["lk2"], lw["subln_w"],
                                 lam_init, hc)

    def attend_sample(l, q, k, v, lw, lam_init):
        return _sample_attention(q, k, v, cache_kt, cache_v2, page_table, l, slopes, lw["lq1"], lw["lk1"],
                                 lw["lq2"], lw["lk2"], lw["subln_w"], lam_init)

    bp = x_prompt.shape[0]
    conv0 = jnp.zeros((depth, bp, 2, da), F32)
    shift0 = jnp.zeros((depth, bp, ns), F32)
    wkv0 = jnp.zeros((depth, bp, nh, hd, hd), F32)
    y_p, k_p, v_p, wkv_p, shift_p, conv_p = _run_trunk(
        x_prompt, layers, final_w, conv0, shift0, wkv0, attend_prompt, dims)
    y_s, k_s, v_s, wkv_s, shift_s, conv_s = _run_trunk(
        x_sample, layers, final_w, state_conv, state_shift, state_wkv, attend_sample, dims)
    return (y_p, y_s, k_p, v_p, k_s, v_s, wkv_p, wkv_s, shift_p, shift_s, conv_p, conv_s)
```

```python
import functools
import math

import jax
import jax.numpy as jnp
from jax import lax
from jax.experimental import pallas as pl
from jax.experimental.pallas import tpu as pltpu

F32 = jnp.float32
BF16 = jnp.bfloat16

NORM_EPS = 1e-6
LNX_EPS = 64e-5
SUBLN_EPS = 1e-5
KK_EPS = 1e-24
NEG_BIG = -1e30
LOG2_E = math.log2(math.e)

V7X_LANES = 128
V7X_SUBLANES = 8
V7X_VMEM_LIMIT_BYTES = 56 * 1024 * 1024

NT_DIMS = (((1,), (1,)), ((), ()))
NN_DIMS = (((1,), (0,)), ((), ()))
TN_DIMS = (((0,), (0,)), ((), ()))


def _params(semantics):
    return pltpu.CompilerParams(dimension_semantics=semantics, vmem_limit_bytes=V7X_VMEM_LIMIT_BYTES)


def _const_spec(shape):
    nd = len(shape)
    return pl.BlockSpec(shape, lambda *_: (0,) * nd, pipeline_mode=pl.Buffered(1))


def _layer_spec(w, layer):
    nd = w.ndim - 1
    return pl.BlockSpec((None,) + w.shape[1:], lambda *_: (layer,) + (0,) * nd, pipeline_mode=pl.Buffered(1))


def _rms(x, w, eps):
    ms = jnp.mean(x * x, axis=-1, keepdims=True)
    return x * lax.rsqrt(ms + eps) * w


def _split_bf16(x, n):
    pieces = []
    rem = x
    for i in range(n):
        p = rem.astype(BF16)
        pieces.append(p)
        if i + 1 < n:
            rem = rem - p.astype(F32)
    return pieces


def _mm(a, b, dims=NN_DIMS, pa=1, pb=1):
    a_p = _split_bf16(a, pa)
    b_p = _split_bf16(b, pb)
    order = max(pa, pb)
    acc = None
    for i, ai in enumerate(a_p):
        for j, bj in enumerate(b_p):
            if i + j >= order:
                continue
            t = lax.dot_general(ai, bj, dims, preferred_element_type=F32)
            acc = t if acc is None else acc + t
    return acc


FFN_CHUNK = 256


def _ffn_body(x_ref, nw_ref, wg_ref, wu_ref, wd_ref, *rest, n_chunks, fc, final):
    if final:
        fw_ref, o_ref, acc_ref = rest
    else:
        o_ref, acc_ref = rest
    x = x_ref[...]
    h = _rms(x, nw_ref[...], NORM_EPS).astype(BF16)
    for c in range(n_chunks):
        cs = slice(c * fc, (c + 1) * fc)
        g = jnp.dot(h, wg_ref[:, cs], preferred_element_type=F32)
        u = jnp.dot(h, wu_ref[:, cs], preferred_element_type=F32)
        act = (g * jax.nn.sigmoid(g) * u).astype(BF16)
        d = jnp.dot(act, wd_ref[cs, :], preferred_element_type=F32)
        if c == 0:
            acc_ref[...] = d
        else:
            acc_ref[...] += d
    y = x + 0.5 * acc_ref[...]
    if final:
        y = _rms(y, fw_ref[...], NORM_EPS)
    o_ref[...] = y


def _ffn(x, norm_w, wg, wu, wd, layer, final_w=None):
    m, d = x.shape
    d_ff = wg.shape[2]
    fc = FFN_CHUNK if d_ff % FFN_CHUNK == 0 else V7X_LANES
    n_chunks = d_ff // fc
    tm = min(m, 512)
    final = final_w is not None
    in_specs = [
        pl.BlockSpec((tm, d), lambda i: (i, 0)),
        _const_spec((1, d)),
        _layer_spec(wg, layer),
        _layer_spec(wu, layer),
        _layer_spec(wd, layer),
    ]
    args = [x, norm_w, wg, wu, wd]
    if final:
        in_specs.append(_const_spec((1, d)))
        args.append(final_w)
    return pl.pallas_call(
        functools.partial(_ffn_body, n_chunks=n_chunks, fc=fc, final=final),
        grid=(m // tm,),
        in_specs=in_specs,
        out_specs=pl.BlockSpec((tm, d), lambda i: (i, 0)),
        out_shape=jax.ShapeDtypeStruct((m, d), F32),
        scratch_shapes=[pltpu.VMEM((tm, d), F32)],
        compiler_params=_params(("parallel",)),
        name="ffn",
    )(*args)


def _mixin_body(x_ref, nw_ref, w_ref, *out_refs, widths, q_index, q_scale, v_index, v_heads, col_tile, tm):
    h = _rms(x_ref[...], nw_ref[...], NORM_EPS).astype(BF16)
    vil_ref = out_refs[len(widths)]
    start = 0
    for idx, (ref, width) in enumerate(zip(out_refs, widths)):
        for c0 in range(0, width, col_tile):
            wd = min(col_tile, width - c0)
            r = jnp.dot(h, w_ref[:, start + c0:start + c0 + wd], preferred_element_type=F32)
            if idx == q_index:
                r = r * q_scale
            ref[:, c0:c0 + wd] = r
            if idx == v_index:
                hw = width // v_heads
                for hh in range(c0 // hw, (c0 + wd) // hw):
                    vil_ref[pl.ds(hh, tm, stride=v_heads), :] = r[:, hh * hw - c0:(hh + 1) * hw - c0]
        start += width


def _mixin(x, norm_w, w_in, layer, widths, q_index, q_scale, v_index, v_heads):
    m, d = x.shape
    tm = min(m, 512)
    hw = widths[v_index] // v_heads
    return pl.pallas_call(
        functools.partial(_mixin_body, widths=widths, q_index=q_index, q_scale=q_scale, v_index=v_index,
                          v_heads=v_heads, col_tile=512, tm=tm),
        grid=(m // tm,),
        in_specs=[pl.BlockSpec((tm, d), lambda i: (i, 0)), _const_spec((1, d)), _layer_spec(w_in, layer)],
        out_specs=[pl.BlockSpec((tm, w), lambda i: (i, 0)) for w in widths]
                  + [pl.BlockSpec((tm * v_heads, hw), lambda i: (i, 0))],
        out_shape=[jax.ShapeDtypeStruct((m, w), F32) for w in widths]
                  + [jax.ShapeDtypeStruct((m * v_heads, hw), F32)],
        compiler_params=_params(("parallel",)),
        name="mixin",
    )(x, norm_w, w_in)


def _conv_body(cin_ref, st_ref, cw_ref, ya_ref, cn_ref, buf, *, tt, da):
    @pl.when(pl.program_id(1) == 0)
    def _():
        buf[pl.ds(6, 2), :] = st_ref[0]

    c = cin_ref[0]
    xin = c[:, 0:da]
    gb = c[:, da:2 * da]
    gc = c[:, 2 * da:3 * da]
    u = gc * xin
    buf[pl.ds(8, tt), :] = u
    cw = cw_ref[...]
    y = buf[pl.ds(6, tt), :] * cw[0:1, :]
    y = y + buf[pl.ds(7, tt), :] * cw[1:2, :]
    y = y + u * cw[2:3, :]
    ya_ref[0] = gb * y
    tail = buf[pl.ds(tt + 6, 2), :]
    cn_ref[0] = tail
    buf[pl.ds(6, 2), :] = tail


def _short_conv(cin, conv_state, conv_w):
    b, t, w3 = cin.shape
    da = w3 // 3
    tt = min(t, 512)
    return pl.pallas_call(
        functools.partial(_conv_body, tt=tt, da=da),
        grid=(b, t // tt),
        in_specs=[
            pl.BlockSpec((1, tt, w3), lambda i, j: (i, j, 0)),
            pl.BlockSpec((1, 2, da), lambda i, j: (i, 0, 0)),
            pl.BlockSpec((3, da), lambda i, j: (0, 0)),
        ],
        out_specs=[
            pl.BlockSpec((1, tt, da), lambda i, j: (i, j, 0)),
            pl.BlockSpec((1, 2, da), lambda i, j: (i, 0, 0)),
        ],
        out_shape=[jax.ShapeDtypeStruct((b, t, da), F32), jax.ShapeDtypeStruct((b, 2, da), F32)],
        scratch_shapes=[pltpu.VMEM((tt + 8, da), F32)],
        compiler_params=_params(("parallel", "arbitrary")),
        name="short_conv",
    )(cin, conv_state, conv_w)


def _pre_body(z_ref, ss_ref, mu_ref, wpad_ref, w0_ref, apad_ref, a0_ref, gup_ref, kk_ref, ka_ref, ones_ref,
              r_ref, lw_ref, k_ref, v_ref, a_ref, b_ref, g_ref, sn_ref, buf, *, tt, db, r_lo, r_g):
    @pl.when(pl.program_id(1) == 0)
    def _():
        buf[pl.ds(7, 1), :] = ss_ref[0]

    z = z_ref[0]
    buf[pl.ds(8, tt), :] = z
    zp = buf[pl.ds(7, tt), :]
    zm = z + (zp - z) * mu_ref[...]
    last = buf[pl.ds(tt + 7, 1), :]
    sn_ref[0] = last
    buf[pl.ds(7, 1), :] = last

    r = zm[:, 0:db]
    k = zm[:, db:2 * db]
    v = zm[:, 2 * db:3 * db]
    lo = zm[:, 3 * db:3 * db + r_lo]
    g_lo = zm[:, 3 * db + r_lo:3 * db + r_lo + r_g]
    w_part = jnp.dot(jnp.tanh(lo).astype(BF16), wpad_ref[...], preferred_element_type=F32)
    a_part = jnp.dot(lo.astype(BF16), apad_ref[...], preferred_element_type=F32)
    xw = -(w0_ref[...] + w_part)
    softplus = jnp.maximum(xw, 0.0) + jnp.log(1.0 + jnp.exp(-jnp.abs(xw)))
    w_log = -softplus - 0.5
    lw = -jnp.exp(w_log)
    a = jax.nn.sigmoid(a0_ref[...] + a_part)
    g = jnp.dot(jax.nn.sigmoid(g_lo).astype(BF16), gup_ref[...], preferred_element_type=F32)
    kk = k * kk_ref[...]
    ss = _mm(kk * kk, ones_ref[...].astype(F32), pa=2, pb=1)
    kkn = kk * lax.rsqrt(jnp.maximum(ss, KK_EPS))
    r_ref[0] = r
    lw_ref[0] = lw
    k_ref[0] = k * (1.0 + (a - 1.0) * ka_ref[...])
    v_ref[0] = v
    a_ref[0] = -kkn
    b_ref[0] = kkn * a
    g_ref[0] = g


def _rwkv_pre(z, shift_state, mu, wpad, w0, apad, a0, gup, k_k, k_a, ones_bd, db):
    b, t, ns = z.shape
    r_lo = wpad.shape[0]
    r_g = gup.shape[0]
    tt = min(t, 512)
    row = lambda n: pl.BlockSpec((1, n), lambda i, j: (0, 0))
    full = lambda s: pl.BlockSpec(s, lambda i, j: (0, 0))
    seq = lambda n: pl.BlockSpec((1, tt, n), lambda i, j: (i, j, 0))
    return pl.pallas_call(
        functools.partial(_pre_body, tt=tt, db=db, r_lo=r_lo, r_g=r_g),
        grid=(b, t // tt),
        in_specs=[
            seq(ns),
            pl.BlockSpec((1, 1, ns), lambda i, j: (i, 0, 0)),
            row(ns), full((r_lo, db)), row(db), full((r_lo, db)), row(db), full((r_g, db)),
            row(db), row(db), full((db, db)),
        ],
        out_specs=[seq(db)] * 7 + [pl.BlockSpec((1, 1, ns), lambda i, j: (i, 0, 0))],
        out_shape=[jax.ShapeDtypeStruct((b, t, db), F32)] * 7 + [jax.ShapeDtypeStruct((b, 1, ns), F32)],
        scratch_shapes=[pltpu.VMEM((tt + 8, ns), F32)],
        compiler_params=_params(("parallel", "arbitrary")),
        name="rwkv_pre",
    )(z, shift_state, mu, wpad, w0, apad, a0, gup, k_k, k_a, ones_bd)


SCAN_SUB = 16
SCAN_P = 1


def _mm_each(a_list, b_list, dims=NN_DIMS):
    return [_mm(a, b, dims=dims, pa=SCAN_P, pb=SCAN_P) for a, b in zip(a_list, b_list)]


def _neumann_inverse_each(l_list, n):
    c = l_list[0].shape[0]
    eye = (lax.broadcasted_iota(jnp.int32, (c, c), 0) == lax.broadcasted_iota(jnp.int32, (c, c), 1)).astype(F32)
    x = [eye + l for l in l_list]
    p = l_list
    span = 2
    while span < n:
        p = _mm_each(p, p)
        xp = _mm_each(x, p)
        x = [xi + d for xi, d in zip(x, xp)]
        span *= 2
    return x


def _unit_lower_inverse_each(l_list):
    c = l_list[0].shape[0]
    if c <= SCAN_SUB:
        return _neumann_inverse_each(l_list, c)
    rows = lax.broadcasted_iota(jnp.int32, (c, c), 0) // SCAN_SUB
    cols = lax.broadcasted_iota(jnp.int32, (c, c), 1) // SCAN_SUB
    same = rows == cols
    t_diag = _neumann_inverse_each([jnp.where(same, l, 0.0) for l in l_list], SCAN_SUB)
    z = _mm_each(t_diag, [jnp.where(same, 0.0, l) for l in l_list])
    nblk = c // SCAN_SUB
    factors = []
    span = 1
    while span < nblk:
        factors.append(z)
        span *= 2
        if span < nblk:
            z = _mm_each(z, z)
    out = t_diag
    for f in reversed(factors):
        d = _mm_each(f, out)
        out = [o + di for o, di in zip(out, d)]
    return out


def _scan_body(r_ref, lw_ref, k_ref, v_ref, a_ref, b_ref, g_ref, s0_ref, rk_ref, lnw_ref, lnb_ref,
               o_ref, sT_ref, st_ref, *, c, bb, nh, hd):
    ci = pl.program_id(1)
    units = [(bi, h) for bi in range(bb) for h in range(nh)]

    @pl.when(ci == 0)
    def _():
        for bi, h in units:
            st_ref[bi, h] = s0_ref[bi, h].T

    rows = lax.broadcasted_iota(jnp.int32, (c, c), 0)
    cols = lax.broadcasted_iota(jnp.int32, (c, c), 1)
    tri = (rows >= cols).astype(F32)
    rows2 = lax.broadcasted_iota(jnp.int32, (c, 2 * c), 0)
    cols2 = lax.broadcasted_iota(jnp.int32, (c, 2 * c), 1)
    cols2 = jnp.where(cols2 >= c, cols2 - c, cols2)
    strict2 = rows2 > cols2
    incl2 = rows2 >= cols2
    ones_cols = jnp.ones((c, hd), F32)
    zeros_cv = jnp.zeros((c, hd), F32)

    ar, bk, bk_rem, v_u, decay = [], [], [], [], []
    for bi in range(bb):
        r = r_ref[bi]
        lw = lw_ref[bi]
        k = k_ref[bi]
        a = a_ref[bi]
        b = b_ref[bi]
        v = v_ref[bi]
        cum = _mm(tri, lw, pa=1, pb=3)
        cum_last = cum[c - 1:c, :]
        e_neg = jnp.exp(-cum)
        e_rem = jnp.exp(cum_last - cum)
        at = a * jnp.exp(cum - lw)
        rt = r * jnp.exp(cum)
        bt = b * e_neg
        kt = k * e_neg
        bh = b * e_rem
        kh = k * e_rem
        cum_cols = _mm(lw, ones_cols, dims=TN_DIMS, pa=3, pb=1)
        for h in range(nh):
            sl = slice(h * hd, (h + 1) * hd)
            ar.append(jnp.concatenate([at[:, sl], rt[:, sl]], axis=0))
            bk.append(jnp.concatenate([bt[:, sl], kt[:, sl]], axis=0))
            bk_rem.append(jnp.concatenate([bh[:, sl], kh[:, sl]], axis=0))
            v_u.append(v[:, sl])
            decay.append(jnp.exp(cum_cols[h * hd:(h + 1) * hd, :]))

    st = [st_ref[bi, h] for bi, h in units]
    m4 = _mm_each(ar, bk, dims=NT_DIMS)
    m_a = [jnp.where(strict2, m[0:c], 0.0) for m in m4]
    m_r = [jnp.where(incl2, m[c:2 * c], 0.0) for m in m4]
    t_inv = _unit_lower_inverse_each([m[:, 0:c] for m in m_a])
    g1 = _mm_each([x[0:c] for x in ar], st)
    g2 = _mm_each(m_a, [jnp.concatenate([zeros_cv, vh], axis=0) for vh in v_u])
    u = _mm_each(t_inv, [x + y for x, y in zip(g1, g2)])
    uv = [jnp.concatenate([ui, vh], axis=0) for ui, vh in zip(u, v_u)]
    y1 = _mm_each([x[c:2 * c] for x in ar], st)
    y2 = _mm_each(m_r, uv)
    st_upd = _mm_each(bk_rem, uv, dims=TN_DIMS)
    for (bi, h), s_old, d, upd in zip(units, st, decay, st_upd):
        st_ref[bi, h] = s_old * d + upd

    rk = rk_ref[...]
    for bi in range(bb):
        r = r_ref[bi]
        k = k_ref[bi]
        rkk = r * k * rk
        yn_parts, bonus_parts = [], []
        for h in range(nh):
            i = bi * nh + h
            sl = slice(h * hd, (h + 1) * hd)
            y = y1[i] + y2[i]
            mean = jnp.mean(y, axis=-1, keepdims=True)
            yc = y - mean
            var = jnp.mean(yc * yc, axis=-1, keepdims=True)
            yn_parts.append(yc * lax.rsqrt(var + LNX_EPS))
            bonus_parts.append(jnp.sum(rkk[:, sl], axis=-1, keepdims=True) * v_u[i])
        yn_all = jnp.concatenate(yn_parts, axis=1)
        bonus_all = jnp.concatenate(bonus_parts, axis=1)
        o_ref[bi] = (yn_all * lnw_ref[...] + lnb_ref[...] + bonus_all) * g_ref[bi]

    @pl.when(ci == pl.num_programs(1) - 1)
    def _():
        for bi, h in units:
            sT_ref[bi, h] = st_ref[bi, h].T


def _rwkv_scan(r, lw, k, v, a, b, g, s0, r_k, lnx_w, lnx_b):
    bsz, t, db = r.shape
    _, nh, hd, _ = s0.shape
    c = min(t, 64)
    bb = 4 if bsz % 4 == 0 else 1
    seq = pl.BlockSpec((bb, c, db), lambda i, j: (i, j, 0))
    row = pl.BlockSpec((1, db), lambda i, j: (0, 0))
    st_spec = pl.BlockSpec((bb, nh, hd, hd), lambda i, j: (i, 0, 0, 0))
    return pl.pallas_call(
        functools.partial(_scan_body, c=c, bb=bb, nh=nh, hd=hd),
        grid=(bsz // bb, t // c),
        in_specs=[seq] * 7 + [st_spec, row, row, row],
        out_specs=[seq, st_spec],
        out_shape=[jax.ShapeDtypeStruct((bsz, t, db), F32), jax.ShapeDtypeStruct((bsz, nh, hd, hd), F32)],
        scratch_shapes=[pltpu.VMEM((bb, nh, hd, hd), F32)],
        compiler_params=_params(("parallel", "arbitrary")),
        name="rwkv_scan",
    )(r, lw, k, v, a, b, g, s0, r_k, lnx_w, lnx_b)


def _lam_value(lq1, lk1, lq2, lk2, lam_init):
    s1 = jnp.sum(lq1 * lk1, axis=-1, keepdims=True)
    s2 = jnp.sum(lq2 * lk2, axis=-1, keepdims=True)
    return jnp.exp(s1) - jnp.exp(s2) + lam_init


def _stack_maps(q, tn, dh):
    q2x = jnp.concatenate([q, q], axis=0)
    row = lax.broadcasted_iota(jnp.int32, q2x.shape, 0)
    lane = lax.broadcasted_iota(jnp.int32, q2x.shape, 1)
    keep = (row < tn) == (lane < dh)
    return jnp.where(keep, q2x, 0.0).astype(BF16)


def _lane_tile(x, width):
    reps = width // V7X_LANES
    return x if reps == 1 else jnp.concatenate([x] * reps, axis=1)


def _diff_finish(o12, tn, lam, subln_w, lam_init):
    o = o12[0:tn] - lam * o12[tn:2 * tn]
    return _rms(o, subln_w, SUBLN_EPS) * (1.0 - lam_init)


PATTN_COLS = 128
PATTN_KEYS = 256
PATTN_HEADS = 4
BIAS_PIECES = 3


def _pattn_body(qi_ref, ki_ref, q_ref, k_ref, v_ref, slope_ref, lq1_ref, lk1_ref, lq2_ref, lk2_ref, sw_ref,
                o_ref, q12_s, ktab_s, s_s, p_s, m_s, l_s, al_s, acc_s, *, tq, dh, hps, n_heads, pk, lam_init):
    step = pl.program_id(2)
    qi = qi_ref[step]
    ki = ki_ref[step]
    nq2 = 2 * tq
    hw = 2 * dh
    heads = range(hps)

    @pl.when(ki == 0)
    def _():
        lane = lax.broadcasted_iota(jnp.int32, (nq2, V7X_LANES), 1)
        ones = jnp.where(lane < BIAS_PIECES, 1.0, 0.0).astype(BF16)
        klane = lax.broadcasted_iota(jnp.int32, (tq, V7X_LANES), 1)
        kpos = lax.broadcasted_iota(jnp.int32, (tq, V7X_LANES), 0).astype(F32)
        for g in heads:
            q12_s[g] = jnp.concatenate([_stack_maps(q_ref[0][:, g * hw:(g + 1) * hw], tq, dh), ones], axis=1)
            rem = slope_ref[g][:, 0:1] * kpos
            tab = jnp.zeros((tq, V7X_LANES), F32)
            for i in range(BIAS_PIECES):
                piece = rem.astype(BF16).astype(F32)
                tab = jnp.where(klane == i, piece, tab)
                rem = rem - piece
            ktab_s[g] = tab.astype(BF16)
        m_s[...] = jnp.full(m_s.shape, NEG_BIG, F32)
        l_s[...] = jnp.zeros(l_s.shape, F32)
        acc_s[...] = jnp.zeros(acc_s.shape, F32)

    def update(diagonal):
        k_all = k_ref[0]
        v_all = v_ref[0]
        vt_bf = []
        for g in heads:
            k_aug = jnp.concatenate([k_all[:, g * hw:(g + 1) * hw].astype(BF16), ktab_s[g]], axis=1)
            s_s[g] = lax.dot_general(k_aug, q12_s[g], NT_DIMS, preferred_element_type=F32)
            vt_bf.append(v_all[:, g * hw:(g + 1) * hw].T.astype(BF16))
        base = ((ki - qi) * tq).astype(F32)
        for k0 in range(0, tq, pk):
            for g in heads:
                shift = slope_ref[g][:, 0:1] * base
                for c0 in range(0, nq2, PATTN_COLS):
                    q0 = c0 % tq
                    cols = pl.ds(c0, PATTN_COLS)
                    if diagonal and k0 > q0 + PATTN_COLS - 1:
                        p_s[g, :, cols] = jnp.zeros((pk, PATTN_COLS), BF16)
                        al_s[g, :, cols] = jnp.ones((1, PATTN_COLS), F32)
                        continue
                    sv = s_s[g, pl.ds(k0, pk), cols]
                    if diagonal and k0 + pk - 1 > q0:
                        qpos = q0 + lax.broadcasted_iota(jnp.int32, (pk, PATTN_COLS), 1)
                        kpos = k0 + lax.broadcasted_iota(jnp.int32, (pk, PATTN_COLS), 0)
                        sv = jnp.where(qpos >= kpos, sv, NEG_BIG)
                    m_old = m_s[g, :, cols]
                    m_new = jnp.maximum(m_old, jnp.max(sv, axis=0, keepdims=True) + shift)
                    alpha = jnp.exp2(m_old - m_new)
                    p = jnp.exp2(sv - (m_new - shift))
                    m_s[g, :, cols] = m_new
                    l_s[g, :, cols] = alpha * l_s[g, :, cols] + jnp.sum(p, axis=0, keepdims=True)
                    al_s[g, :, cols] = alpha
                    p_s[g, :, cols] = p.astype(BF16)
            for g in heads:
                acc_s[g] = al_s[g] * acc_s[g] + jnp.dot(vt_bf[g][:, k0:k0 + pk], p_s[g],
                                                        preferred_element_type=F32)

    @pl.when(ki < qi)
    def _():
        update(False)

    @pl.when(ki == qi)
    def _():
        update(True)
        lam = _lam_value(lq1_ref[...], lk1_ref[...], lq2_ref[...], lk2_ref[...], lam_init)
        outs = []
        for g in heads:
            o12 = (acc_s[g] / l_s[g]).T
            outs.append(_diff_finish(o12, tq, lam, sw_ref[...], lam_init))
        o_ref[0] = outs[0] if hps == 1 else jnp.concatenate(outs, axis=1)


def _prompt_attention(q, k, v, slopes, lq1, lk1, lq2, lk2, subln_w, lam_init, n_heads):
    b, t, w = q.shape
    hw = w // n_heads
    assert hw == V7X_LANES, "one head's two maps fill exactly one lane tile"
    dh = hw // 2
    tq = min(t, 512)
    nq = t // tq
    hps = math.gcd(n_heads, PATTN_HEADS)
    pk = min(PATTN_KEYS, tq)
    pairs = [(i, j) for i in range(nq) for j in range(i + 1)]
    qi_tbl = jnp.asarray([p[0] for p in pairs], jnp.int32)
    ki_tbl = jnp.asarray([p[1] for p in pairs], jnp.int32)
    qspec = pl.BlockSpec((1, tq, hps * hw), lambda bi, h, s, qt, kt: (bi, qt[s], h))
    kspec = pl.BlockSpec((1, tq, hps * hw), lambda bi, h, s, qt, kt: (bi, kt[s], h))
    small = lambda n: pl.BlockSpec((1, n), lambda bi, h, s, qt, kt: (0, 0))
    row = pltpu.VMEM((hps, 1, 2 * tq), F32)
    grid_spec = pltpu.PrefetchScalarGridSpec(
        num_scalar_prefetch=2,
        grid=(b, n_heads // hps, len(pairs)),
        in_specs=[qspec, kspec, kspec,
                  pl.BlockSpec((hps, 1, V7X_LANES), lambda bi, h, s, qt, kt: (h, 0, 0)),
                  small(dh), small(dh), small(dh), small(dh), small(hw)],
        out_specs=qspec,
        scratch_shapes=[pltpu.VMEM((hps, 2 * tq, 2 * hw), BF16), pltpu.VMEM((hps, tq, V7X_LANES), BF16),
                        pltpu.VMEM((hps, tq, 2 * tq), F32),
                        pltpu.VMEM((hps, pk, 2 * tq), BF16), row, row, row,
                        pltpu.VMEM((hps, hw, 2 * tq), F32)],
    )
    return pl.pallas_call(
        functools.partial(_pattn_body, tq=tq, dh=dh, hps=hps, n_heads=n_heads, pk=pk, lam_init=lam_init),
        grid_spec=grid_spec,
        out_shape=jax.ShapeDtypeStruct((b, t, w), F32),
        compiler_params=_params(("parallel", "parallel", "arbitrary")),
        name="prompt_attention",
    )(qi_tbl, ki_tbl, q, k, v, slopes, lq1, lk1, lq2, lk2, subln_w)


SATTN_PAGES = 32


def _sattn_body(pt_ref, q_ref, kn_ref, vn_ref, slope_ref, lq1_ref, lk1_ref, lq2_ref, lk2_ref, sw_ref, *rest,
                ppb, page, n_heads, tn, dh, past_len, lam_init):
    k_refs = rest[:ppb]
    v_refs = rest[ppb:2 * ppb]
    o_ref, m_s, l_s, acc_s = rest[2 * ppb:]
    del pt_ref
    si = pl.program_id(1)
    hw = 2 * dh
    width = ppb * page
    heads = range(n_heads)

    @pl.when(si == 0)
    def _():
        m_s[...] = jnp.full(m_s.shape, NEG_BIG, F32)
        l_s[...] = jnp.zeros(l_s.shape, F32)
        acc_s[...] = jnp.zeros(acc_s.shape, F32)

    q = q_ref[0]
    qm = [q[:, m * dh:(m + 1) * dh].astype(BF16) for m in range(2 * n_heads)]

    def scores(h, key_of_map, dims):
        sa = lax.dot_general(qm[2 * h], key_of_map(2 * h), dims, preferred_element_type=F32)
        sb = lax.dot_general(qm[2 * h + 1], key_of_map(2 * h + 1), dims, preferred_element_type=F32)
        return jnp.concatenate([sa, sb], axis=0)

    kpos = si * width + lax.broadcasted_iota(jnp.int32, (1, width), 1) - past_len
    kpos = kpos.astype(F32)
    s_all = []
    for h in heads:
        parts = [scores(h, lambda m, j=j: k_refs[j][m].astype(BF16), NN_DIMS) for j in range(ppb)]
        s_all.append(jnp.concatenate(parts, axis=1) + slope_ref[h][:, 0:1] * kpos)
    alphas, ps = [], []
    for h in heads:
        m_old = m_s[h]
        m_new = jnp.maximum(m_old, jnp.max(s_all[h], axis=-1, keepdims=True))
        alpha = jnp.exp2(m_old - m_new)
        p = jnp.exp2(s_all[h] - _lane_tile(m_new, width))
        l_s[h] = alpha * l_s[h] + jnp.sum(p, axis=-1, keepdims=True)
        m_s[h] = m_new
        alphas.append(alpha)
        ps.append(p.astype(BF16))
    for h in heads:
        pv = None
        for j in range(ppb):
            t = jnp.dot(ps[h][:, j * page:(j + 1) * page],
                        v_refs[j][pl.ds(h, page, stride=n_heads), :].astype(BF16),
                        preferred_element_type=F32)
            pv = t if pv is None else pv + t
        acc_s[h] = alphas[h] * acc_s[h] + pv

    @pl.when(si == pl.num_programs(1) - 1)
    def _():
        lam = _lam_value(lq1_ref[...], lk1_ref[...], lq2_ref[...], lk2_ref[...], lam_init)
        kn = kn_ref[0]
        vn = vn_ref[0]
        row_n = lax.broadcasted_iota(jnp.int32, (2 * tn, tn), 0)
        col_n = lax.broadcasted_iota(jnp.int32, (2 * tn, tn), 1)
        causal = jnp.where(row_n >= tn, row_n - tn, row_n) >= col_n
        outs = []
        for h in heads:
            s = scores(h, lambda m: kn[:, m * dh:(m + 1) * dh].astype(BF16), NT_DIMS)
            s = jnp.where(causal, s + slope_ref[h][:, 0:1] * col_n.astype(F32), NEG_BIG)
            m_old = m_s[h]
            m_new = jnp.maximum(m_old, jnp.max(s, axis=-1, keepdims=True))
            alpha = jnp.exp2(m_old - m_new)
            p = jnp.exp2(s - m_new[:, 0:tn])
            l_fin = alpha * l_s[h] + jnp.sum(p, axis=-1, keepdims=True)
            acc = alpha * acc_s[h] + jnp.dot(p.astype(BF16), vn[:, h * hw:(h + 1) * hw].astype(BF16),
                                             preferred_element_type=F32)
            outs.append(_diff_finish(acc / l_fin, tn, lam, sw_ref[...], lam_init))
        o_ref[0] = jnp.concatenate(outs, axis=1)


def _sample_attention(q, k_new, v_new, cache_kt, cache_v, page_table, layer, slopes, lq1, lk1, lq2, lk2, subln_w,
                      lam_init):
    b, tn, w = q.shape
    _, _, maps, dh, page = cache_kt.shape
    n_heads = maps // 2
    hw = 2 * dh
    assert hw == V7X_LANES, "softmax statistics are kept lane-replicated at the head width"
    n_pages = page_table.shape[1]
    ppb = math.gcd(n_pages, SATTN_PAGES)
    n_steps = n_pages // ppb
    pt_flat = page_table.reshape(-1)

    def page_index(j, nd):
        return lambda bi, si, pt: (layer, pt[bi * n_pages + si * ppb + j]) + (0,) * nd

    k_specs = [pl.BlockSpec((None, None, maps, dh, page), page_index(j, 3)) for j in range(ppb)]
    v_specs = [pl.BlockSpec((None, None, page * n_heads, hw), page_index(j, 2)) for j in range(ppb)]
    new_spec = pl.BlockSpec((1, tn, w), lambda bi, si, pt: (bi, 0, 0))
    small = lambda n: pl.BlockSpec((1, n), lambda bi, si, pt: (0, 0))
    grid_spec = pltpu.PrefetchScalarGridSpec(
        num_scalar_prefetch=1,
        grid=(b, n_steps),
        in_specs=[new_spec, new_spec, new_spec,
                  pl.BlockSpec((n_heads, 1, V7X_LANES), lambda bi, si, pt: (0, 0, 0)),
                  small(dh), small(dh), small(dh), small(dh), small(hw)] + k_specs + v_specs,
        out_specs=new_spec,
        scratch_shapes=[pltpu.VMEM((n_heads, 2 * tn, hw), F32), pltpu.VMEM((n_heads, 2 * tn, hw), F32),
                        pltpu.VMEM((n_heads, 2 * tn, hw), F32)],
    )
    return pl.pallas_call(
        functools.partial(_sattn_body, ppb=ppb, page=page, n_heads=n_heads, tn=tn, dh=dh,
                          past_len=n_pages * page, lam_init=lam_init),
        grid_spec=grid_spec,
        out_shape=jax.ShapeDtypeStruct((b, tn, w), F32),
        compiler_params=_params(("parallel", "arbitrary")),
        name="sample_attention",
    )(pt_flat, q, k_new, v_new, slopes, lq1, lk1, lq2, lk2, subln_w,
      *([cache_kt] * ppb), *([cache_v] * ppb))


def _merge_body(x_ref, ya_ref, yb_ref, yc_ref, gt_ref, wa_ref, wb_ref, wc_ref, wo_ref, o_ref, *, d):
    gates = gt_ref[...]
    merged = jax.nn.sigmoid(gates[:, 0:d]) * jnp.dot(ya_ref[...].astype(BF16), wa_ref[...], preferred_element_type=F32)
    merged = merged + jax.nn.sigmoid(gates[:, d:2 * d]) * jnp.dot(
        yb_ref[...].astype(BF16), wb_ref[...], preferred_element_type=F32)
    merged = merged + jax.nn.sigmoid(gates[:, 2 * d:3 * d]) * jnp.dot(
        yc_ref[...].astype(BF16), wc_ref[...], preferred_element_type=F32)
    o_ref[...] = x_ref[...] + jnp.dot(merged.astype(BF16), wo_ref[...], preferred_element_type=F32)


def _merge(x, ya, yb, yc, gates, wa, wb, wc, wo, layer):
    m, d = x.shape
    tm = min(m, 512)
    tile = lambda n: pl.BlockSpec((tm, n), lambda i: (i, 0))
    return pl.pallas_call(
        functools.partial(_merge_body, d=d),
        grid=(m // tm,),
        in_specs=[tile(d), tile(ya.shape[1]), tile(yb.shape[1]), tile(yc.shape[1]), tile(3 * d),
                  _layer_spec(wa, layer), _layer_spec(wb, layer), _layer_spec(wc, layer), _layer_spec(wo, layer)],
        out_specs=tile(d),
        out_shape=jax.ShapeDtypeStruct((m, d), F32),
        compiler_params=_params(("parallel",)),
        name="merge",
    )(x, ya, yb, yc, gates, wa, wb, wc, wo)


STACKED_BF16 = ("ffn1_w_gate", "ffn1_w_up", "ffn1_w_down", "ffn2_w_gate", "ffn2_w_up", "ffn2_w_down", "w_in",
                "w_branch_a", "w_branch_b", "w_branch_c", "w_out")


def _prep_layer_weights(l, W, big, dims):
    db, r_w, r_a = dims["db"], dims["r_w"], dims["r_a"]
    row = lambda a: a.reshape(1, -1)

    def ffn(prefix):
        return (row(W[prefix + "_norm"][l]), big[prefix + "_w_gate"], big[prefix + "_w_up"], big[prefix + "_w_down"])

    zeros_w = jnp.zeros((r_a, db), BF16)
    zeros_a = jnp.zeros((r_w, db), BF16)
    head = jnp.arange(db) // dims["hd"]
    return dict(
        ffn1=ffn("ffn1"), ffn2=ffn("ffn2"),
        mix_norm=row(W["mix_norm"][l]), w_in=big["w_in"],
        conv_w=W["conv_w"][l],
        mu=row(W["tm_mu"][l]),
        wpad=jnp.concatenate([W["w_up_w"][l].astype(BF16), zeros_w], axis=0),
        apad=jnp.concatenate([zeros_a, W["a_up_w"][l].astype(BF16)], axis=0),
        w0=row(W["w0"][l]), a0=row(W["a0"][l]), gup=W["g_up_w"][l].astype(BF16),
        k_k=row(W["k_k"][l]), k_a=row(W["k_a"][l]), r_k=row(W["r_k"][l]),
        lnx_w=row(W["lnx_w"][l]), lnx_b=row(W["lnx_b"][l]),
        ones_bd=(head[:, None] == head[None, :]).astype(BF16),
        lq1=row(W["lam_q1"][l]), lk1=row(W["lam_k1"][l]), lq2=row(W["lam_q2"][l]), lk2=row(W["lam_k2"][l]),
        subln_w=row(W["subln_w"][l]),
        wa=big["w_branch_a"], wb=big["w_branch_b"], wc=big["w_branch_c"], wo=big["w_out"],
    )


def _run_trunk(x, layers, final_w, conv0, shift0, wkv0, attend, dims):
    b, t, d = x.shape
    m = b * t
    da, db, ns = dims["da"], dims["db"], dims["ns"]
    qw, kw, dc = dims["qw"], dims["kw"], dims["dc"]
    widths = (3 * da, ns, qw, kw, dc, 3 * d)
    Q_OUT, V_OUT = 2, 4
    depth = len(layers)
    xf = x.reshape(m, d)
    k_rows, v_rows, wkv_out, shift_out, conv_out = [], [], [], [], []
    for l, lw in enumerate(layers):
        xf = _ffn(xf, *lw["ffn1"], l)
        cin, z, q, k, v, gates, v_il = _mixin(xf, lw["mix_norm"], lw["w_in"], l, widths, Q_OUT,
                                              dims["dh"] ** -0.5 * LOG2_E, V_OUT, dims["hc"])
        y_a, conv_new = _short_conv(cin.reshape(b, t, 3 * da), conv0[l], lw["conv_w"])
        r_, lw_, k_, v_, a_, b_, g_, shift_new = _rwkv_pre(
            z.reshape(b, t, ns), shift0[l].reshape(b, 1, ns), lw["mu"], lw["wpad"], lw["w0"], lw["apad"], lw["a0"],
            lw["gup"], lw["k_k"], lw["k_a"], lw["ones_bd"], db)
        y_b, wkv_new = _rwkv_scan(r_, lw_, k_, v_, a_, b_, g_, wkv0[l], lw["r_k"], lw["lnx_w"], lw["lnx_b"])
        lam_init = 0.8 - 0.6 * math.exp(-0.3 * l)
        y_c = attend(l, q.reshape(b, t, qw), k.reshape(b, t, kw), v.reshape(b, t, dc), lw, lam_init)
        xf = _merge(xf, y_a.reshape(m, da), y_b.reshape(m, db), y_c.reshape(m, dc), gates,
                    lw["wa"], lw["wb"], lw["wc"], lw["wo"], l)
        xf = _ffn(xf, *lw["ffn2"], l, final_w=final_w if l == depth - 1 else None)
        k_rows.append(k.reshape(b, t, 2 * dims["hc"], dims["dh"]))
        v_rows.append(v_il.reshape(b, t, dims["hc"], 2 * dims["dh"]))
        wkv_out.append(wkv_new)
        shift_out.append(shift_new.reshape(b, ns))
        conv_out.append(conv_new)
    return (xf.reshape(b, t, d), jnp.stack(k_rows), jnp.stack(v_rows), jnp.stack(wkv_out),
            jnp.stack(shift_out), jnp.stack(conv_out))


def kernel(x_prompt, x_sample, cache_k, cache_v, state_wkv, state_shift, state_conv, page_table, ffn1_norm, ffn1_w_gate, ffn1_w_up, ffn1_w_down, mix_norm, w_in, conv_w, tm_mu, w_up_w, w0, a_up_w, a0, g_up_w, k_k, k_a, r_k, lnx_w, lnx_b, lam_q1, lam_k1, lam_q2, lam_k2, subln_w, w_branch_a, w_branch_b, w_branch_c, w_out, ffn2_norm, ffn2_w_gate, ffn2_w_up, ffn2_w_down, final_norm):
    W = dict(ffn1_norm=ffn1_norm, ffn1_w_gate=ffn1_w_gate, ffn1_w_up=ffn1_w_up, ffn1_w_down=ffn1_w_down,
             mix_norm=mix_norm, w_in=w_in, conv_w=conv_w, tm_mu=tm_mu, w_up_w=w_up_w, w0=w0,
             a_up_w=a_up_w, a0=a0, g_up_w=g_up_w, k_k=k_k, k_a=k_a, r_k=r_k, lnx_w=lnx_w, lnx_b=lnx_b,
             lam_q1=lam_q1, lam_k1=lam_k1, lam_q2=lam_q2, lam_k2=lam_k2, subln_w=subln_w,
             w_branch_a=w_branch_a, w_branch_b=w_branch_b, w_branch_c=w_branch_c, w_out=w_out,
             ffn2_norm=ffn2_norm, ffn2_w_gate=ffn2_w_gate, ffn2_w_up=ffn2_w_up, ffn2_w_down=ffn2_w_down)
    depth = w_in.shape[0]
    d = x_prompt.shape[-1]
    _, n_pool, page, maps, dh = cache_k.shape
    hc = maps // 2
    nh, hd = r_k.shape[1], r_k.shape[2]
    db = nh * hd
    da = conv_w.shape[-1]
    assert conv_w.shape[1] == 3, "short conv kernel is written for width 3"
    r_w, r_a, r_g = w_up_w.shape[1], a_up_w.shape[1], g_up_w.shape[1]
    ns = state_shift.shape[-1]
    assert ns == 3 * db + r_w + r_a + r_g
    assert (r_w + r_a) % V7X_LANES == 0 and r_g % V7X_LANES == 0
    dims = dict(d=d, d_ff=ffn1_w_gate.shape[-1], da=da, db=db, hd=hd, ns=ns, r_w=r_w, r_a=r_a,
                qw=maps * dh, kw=maps * dh, dc=hc * 2 * dh, hc=hc, dh=dh)
    big = {name: W[name].astype(BF16) for name in STACKED_BF16}
    layers = [_prep_layer_weights(l, W, big, dims) for l in range(depth)]
    final_w = final_norm.reshape(1, d)
    slopes = jnp.broadcast_to(
        jnp.asarray([LOG2_E * 2.0 ** (-8.0 * (h + 1) / hc) for h in range(hc)], F32)[:, None, None],
        (hc, 1, V7X_LANES))
    cache_kt = jnp.transpose(cache_k, (0, 1, 3, 4, 2))
    cache_v2 = cache_v.reshape(depth, n_pool, page * hc, 2 * dh)

    def attend_prompt(l, q, k, v, lw, lam_init):
        return _prompt_attention(q, k, v, slopes, lw["lq1"], lw["lk1"], lw["lq2"], lw["lk2"], lw["subln_w"],
                                 lam_init, hc)

    def attend_sample(l, q, k, v, lw, lam_init):
        return _sample_attention(q, k, v, cache_kt, cache_v2, page_table, l, slopes, lw["lq1"], lw["lk1"],
                                 lw["lq2"], lw["lk2"], lw["subln_w"], lam_init)

    bp = x_prompt.shape[0]
    conv0 = jnp.zeros((depth, bp, 2, da), F32)
    shift0 = jnp.zeros((depth, bp, ns), F32)
    wkv0 = jnp.zeros((depth, bp, nh, hd, hd), F32)
    y_p, k_p, v_p, wkv_p, shift_p, conv_p = _run_trunk(
        x_prompt, layers, final_w, conv0, shift0, wkv0, attend_prompt, dims)
    y_s, k_s, v_s, wkv_s, shift_s, conv_s = _run_trunk(
        x_sample, layers, final_w, state_conv, state_shift, state_wkv, attend_sample, dims)
    return (y_p, y_s, k_p, v_p, k_s, v_s, wkv_p, wkv_s, shift_p, shift_s, conv_p, conv_s)
```

```python
import functools
import math

import jax
import jax.numpy as jnp
from jax import lax
from jax.experimental import pallas as pl
from jax.experimental.pallas import tpu as pltpu

F32 = jnp.float32
BF16 = jnp.bfloat16

NORM_EPS = 1e-6
LNX_EPS = 64e-5
SUBLN_EPS = 1e-5
KK_EPS = 1e-24
NEG_BIG = -1e30
LOG2_E = math.log2(math.e)

V7X_LANES = 128
V7X_SUBLANES = 8
V7X_VMEM_LIMIT_BYTES = 56 * 1024 * 1024

NT_DIMS = (((1,), (1,)), ((), ()))
NN_DIMS = (((1,), (0,)), ((), ()))
TN_DIMS = (((0,), (0,)), ((), ()))


def _params(semantics):
    return pltpu.CompilerParams(dimension_semantics=semantics, vmem_limit_bytes=V7X_VMEM_LIMIT_BYTES)


def _const_spec(shape):
    nd = len(shape)
    return pl.BlockSpec(shape, lambda *_: (0,) * nd, pipeline_mode=pl.Buffered(1))


def _layer_spec(w, layer):
    nd = w.ndim - 1
    return pl.BlockSpec((None,) + w.shape[1:], lambda *_: (layer,) + (0,) * nd, pipeline_mode=pl.Buffered(1))


def _rms(x, w, eps):
    ms = jnp.mean(x * x, axis=-1, keepdims=True)
    return x * lax.rsqrt(ms + eps) * w


def _split_bf16(x, n):
    pieces = []
    rem = x
    for i in range(n):
        p = rem.astype(BF16)
        pieces.append(p)
        if i + 1 < n:
            rem = rem - p.astype(F32)
    return pieces


def _mm(a, b, dims=NN_DIMS, pa=1, pb=1):
    a_p = _split_bf16(a, pa)
    b_p = _split_bf16(b, pb)
    order = max(pa, pb)
    acc = None
    for i, ai in enumerate(a_p):
        for j, bj in enumerate(b_p):
            if i + j >= order:
                continue
            t = lax.dot_general(ai, bj, dims, preferred_element_type=F32)
            acc = t if acc is None else acc + t
    return acc


FFN_CHUNK = 256


def _ffn_body(x_ref, nw_ref, wg_ref, wu_ref, wd_ref, *rest, n_chunks, fc, final):
    if final:
        fw_ref, o_ref, acc_ref = rest
    else:
        o_ref, acc_ref = rest
    x = x_ref[...]
    h = _rms(x, nw_ref[...], NORM_EPS).astype(BF16)
    for c in range(n_chunks):
        cs = slice(c * fc, (c + 1) * fc)
        g = jnp.dot(h, wg_ref[:, cs], preferred_element_type=F32)
        u = jnp.dot(h, wu_ref[:, cs], preferred_element_type=F32)
        act = (g * jax.nn.sigmoid(g) * u).astype(BF16)
        d = jnp.dot(act, wd_ref[cs, :], preferred_element_type=F32)
        if c == 0:
            acc_ref[...] = d
        else:
            acc_ref[...] += d
    y = x + 0.5 * acc_ref[...]
    if final:
        y = _rms(y, fw_ref[...], NORM_EPS)
    o_ref[...] = y


def _ffn(x, norm_w, wg, wu, wd, layer, final_w=None):
    m, d = x.shape
    d_ff = wg.shape[2]
    fc = FFN_CHUNK if d_ff % FFN_CHUNK == 0 else V7X_LANES
    n_chunks = d_ff // fc
    tm = min(m, 512)
    final = final_w is not None
    in_specs = [
        pl.BlockSpec((tm, d), lambda i: (i, 0)),
        _const_spec((1, d)),
        _layer_spec(wg, layer),
        _layer_spec(wu, layer),
        _layer_spec(wd, layer),
    ]
    args = [x, norm_w, wg, wu, wd]
    if final:
        in_specs.append(_const_spec((1, d)))
        args.append(final_w)
    return pl.pallas_call(
        functools.partial(_ffn_body, n_chunks=n_chunks, fc=fc, final=final),
        grid=(m // tm,),
        in_specs=in_specs,
        out_specs=pl.BlockSpec((tm, d), lambda i: (i, 0)),
        out_shape=jax.ShapeDtypeStruct((m, d), F32),
        scratch_shapes=[pltpu.VMEM((tm, d), F32)],
        compiler_params=_params(("parallel",)),
        name="ffn",
    )(*args)


def _mixin_body(x_ref, nw_ref, w_ref, *refs, widths, n_carried, q_index, q_scale, k_index, v_index, v_heads,
                col_tile, tm):
    out_refs = refs[n_carried:]
    vil_ref = out_refs[len(widths)]
    kt_ref = out_refs[len(widths) + 1] if len(out_refs) > len(widths) + 1 else None
    h = _rms(x_ref[...], nw_ref[...], NORM_EPS).astype(BF16)
    start = 0
    for idx, (ref, width) in enumerate(zip(out_refs, widths)):
        for c0 in range(0, width, col_tile):
            wd = min(col_tile, width - c0)
            r = jnp.dot(h, w_ref[:, start + c0:start + c0 + wd], preferred_element_type=F32)
            if idx == q_index:
                r = r * q_scale
            ref[:, c0:c0 + wd] = r
            if idx == v_index:
                hw = width // v_heads
                for hh in range(c0 // hw, (c0 + wd) // hw):
                    vil_ref[pl.ds(hh, tm, stride=v_heads), :] = r[:, hh * hw - c0:(hh + 1) * hw - c0]
            if idx == k_index and kt_ref is not None:
                kt_ref[c0:c0 + wd, :] = r.T
        start += width


def _mixin(x, norm_w, w_in, layer, widths, q_index, q_scale, k_index, v_index, v_heads, seq_len, carried):
    m, d = x.shape
    depth = w_in.shape[0]
    tm = min(m, 512)
    hw = widths[v_index] // v_heads
    kw = widths[k_index]
    emit_kt = seq_len % tm == 0
    tiles_per_seq = seq_len // tm if emit_kt else 1
    stack_specs = [pl.BlockSpec((None, tm * v_heads, hw), lambda i: (layer, i, 0))]
    stack_shapes = [jax.ShapeDtypeStruct((depth, m * v_heads, hw), F32)]
    if emit_kt:
        stack_specs.append(pl.BlockSpec((None, None, kw, tm),
                                        lambda i: (layer, i // tiles_per_seq, 0, i % tiles_per_seq)))
        stack_shapes.append(jax.ShapeDtypeStruct((depth, m // seq_len, kw, seq_len), F32))
    carried = list(carried) if carried is not None else []
    n_in = 3
    return pl.pallas_call(
        functools.partial(_mixin_body, widths=widths, n_carried=len(carried), q_index=q_index, q_scale=q_scale,
                          k_index=k_index, v_index=v_index, v_heads=v_heads, col_tile=512, tm=tm),
        grid=(m // tm,),
        in_specs=[pl.BlockSpec((tm, d), lambda i: (i, 0)), _const_spec((1, d)), _layer_spec(w_in, layer)]
                 + [pl.BlockSpec(memory_space=pl.ANY)] * len(carried),
        out_specs=[pl.BlockSpec((tm, w), lambda i: (i, 0)) for w in widths] + stack_specs,
        out_shape=[jax.ShapeDtypeStruct((m, w), F32) for w in widths] + stack_shapes,
        input_output_aliases={n_in + j: len(widths) + j for j in range(len(carried))},
        compiler_params=_params(("parallel",)),
        name="mixin",
    )(x, norm_w, w_in, *carried)


def _conv_body(cin_ref, st_ref, cw_ref, ya_ref, cn_ref, buf, *, tt, da):
    @pl.when(pl.program_id(1) == 0)
    def _():
        buf[pl.ds(6, 2), :] = st_ref[0]

    c = cin_ref[0]
    xin = c[:, 0:da]
    gb = c[:, da:2 * da]
    gc = c[:, 2 * da:3 * da]
    u = gc * xin
    buf[pl.ds(8, tt), :] = u
    cw = cw_ref[...]
    y = buf[pl.ds(6, tt), :] * cw[0:1, :]
    y = y + buf[pl.ds(7, tt), :] * cw[1:2, :]
    y = y + u * cw[2:3, :]
    ya_ref[0] = gb * y
    tail = buf[pl.ds(tt + 6, 2), :]
    cn_ref[0] = tail
    buf[pl.ds(6, 2), :] = tail


def _short_conv(cin, conv_state, conv_w):
    b, t, w3 = cin.shape
    da = w3 // 3
    tt = min(t, 512)
    return pl.pallas_call(
        functools.partial(_conv_body, tt=tt, da=da),
        grid=(b, t // tt),
        in_specs=[
            pl.BlockSpec((1, tt, w3), lambda i, j: (i, j, 0)),
            pl.BlockSpec((1, 2, da), lambda i, j: (i, 0, 0)),
            pl.BlockSpec((3, da), lambda i, j: (0, 0)),
        ],
        out_specs=[
            pl.BlockSpec((1, tt, da), lambda i, j: (i, j, 0)),
            pl.BlockSpec((1, 2, da), lambda i, j: (i, 0, 0)),
        ],
        out_shape=[jax.ShapeDtypeStruct((b, t, da), F32), jax.ShapeDtypeStruct((b, 2, da), F32)],
        scratch_shapes=[pltpu.VMEM((tt + 8, da), F32)],
        compiler_params=_params(("parallel", "arbitrary")),
        name="short_conv",
    )(cin, conv_state, conv_w)


def _pre_body(z_ref, ss_ref, mu_ref, wpad_ref, w0_ref, apad_ref, a0_ref, gup_ref, kk_ref, ka_ref, ones_ref,
              r_ref, lw_ref, k_ref, v_ref, a_ref, b_ref, g_ref, sn_ref, buf, *, tt, db, r_lo, r_g):
    @pl.when(pl.program_id(1) == 0)
    def _():
        buf[pl.ds(7, 1), :] = ss_ref[0]

    z = z_ref[0]
    buf[pl.ds(8, tt), :] = z
    zp = buf[pl.ds(7, tt), :]
    zm = z + (zp - z) * mu_ref[...]
    last = buf[pl.ds(tt + 7, 1), :]
    sn_ref[0] = last
    buf[pl.ds(7, 1), :] = last

    r = zm[:, 0:db]
    k = zm[:, db:2 * db]
    v = zm[:, 2 * db:3 * db]
    lo = zm[:, 3 * db:3 * db + r_lo]
    g_lo = zm[:, 3 * db + r_lo:3 * db + r_lo + r_g]
    w_part = jnp.dot(jnp.tanh(lo).astype(BF16), wpad_ref[...], preferred_element_type=F32)
    a_part = jnp.dot(lo.astype(BF16), apad_ref[...], preferred_element_type=F32)
    xw = -(w0_ref[...] + w_part)
    softplus = jnp.maximum(xw, 0.0) + jnp.log(1.0 + jnp.exp(-jnp.abs(xw)))
    w_log = -softplus - 0.5
    lw = -jnp.exp(w_log)
    a = jax.nn.sigmoid(a0_ref[...] + a_part)
    g = jnp.dot(jax.nn.sigmoid(g_lo).astype(BF16), gup_ref[...], preferred_element_type=F32)
    kk = k * kk_ref[...]
    ss = _mm(kk * kk, ones_ref[...].astype(F32), pa=2, pb=1)
    kkn = kk * lax.rsqrt(jnp.maximum(ss, KK_EPS))
    r_ref[0] = r
    lw_ref[0] = lw
    k_ref[0] = k * (1.0 + (a - 1.0) * ka_ref[...])
    v_ref[0] = v
    a_ref[0] = -kkn
    b_ref[0] = kkn * a
    g_ref[0] = g


def _rwkv_pre(z, shift_state, mu, wpad, w0, apad, a0, gup, k_k, k_a, ones_bd, db):
    b, t, ns = z.shape
    r_lo = wpad.shape[0]
    r_g = gup.shape[0]
    tt = min(t, 512)
    row = lambda n: pl.BlockSpec((1, n), lambda i, j: (0, 0))
    full = lambda s: pl.BlockSpec(s, lambda i, j: (0, 0))
    seq = lambda n: pl.BlockSpec((1, tt, n), lambda i, j: (i, j, 0))
    return pl.pallas_call(
        functools.partial(_pre_body, tt=tt, db=db, r_lo=r_lo, r_g=r_g),
        grid=(b, t // tt),
        in_specs=[
            seq(ns),
            pl.BlockSpec((1, 1, ns), lambda i, j: (i, 0, 0)),
            row(ns), full((r_lo, db)), row(db), full((r_lo, db)), row(db), full((r_g, db)),
            row(db), row(db), full((db, db)),
        ],
        out_specs=[seq(db)] * 7 + [pl.BlockSpec((1, 1, ns), lambda i, j: (i, 0, 0))],
        out_shape=[jax.ShapeDtypeStruct((b, t, db), F32)] * 7 + [jax.ShapeDtypeStruct((b, 1, ns), F32)],
        scratch_shapes=[pltpu.VMEM((tt + 8, ns), F32)],
        compiler_params=_params(("parallel", "arbitrary")),
        name="rwkv_pre",
    )(z, shift_state, mu, wpad, w0, apad, a0, gup, k_k, k_a, ones_bd)


SCAN_SUB = 16
SCAN_P = 1


def _mm_each(a_list, b_list, dims=NN_DIMS):
    return [_mm(a, b, dims=dims, pa=SCAN_P, pb=SCAN_P) for a, b in zip(a_list, b_list)]


def _neumann_inverse_each(l_list, n):
    c = l_list[0].shape[0]
    eye = (lax.broadcasted_iota(jnp.int32, (c, c), 0) == lax.broadcasted_iota(jnp.int32, (c, c), 1)).astype(F32)
    x = [eye + l for l in l_list]
    p = l_list
    span = 2
    while span < n:
        p = _mm_each(p, p)
        xp = _mm_each(x, p)
        x = [xi + d for xi, d in zip(x, xp)]
        span *= 2
    return x


def _unit_lower_inverse_each(l_list):
    c = l_list[0].shape[0]
    if c <= SCAN_SUB:
        return _neumann_inverse_each(l_list, c)
    rows = lax.broadcasted_iota(jnp.int32, (c, c), 0) // SCAN_SUB
    cols = lax.broadcasted_iota(jnp.int32, (c, c), 1) // SCAN_SUB
    same = rows == cols
    t_diag = _neumann_inverse_each([jnp.where(same, l, 0.0) for l in l_list], SCAN_SUB)
    z = _mm_each(t_diag, [jnp.where(same, 0.0, l) for l in l_list])
    nblk = c // SCAN_SUB
    factors = []
    span = 1
    while span < nblk:
        factors.append(z)
        span *= 2
        if span < nblk:
            z = _mm_each(z, z)
    out = t_diag
    for f in reversed(factors):
        d = _mm_each(f, out)
        out = [o + di for o, di in zip(out, d)]
    return out


def _scan_body(r_ref, lw_ref, k_ref, v_ref, a_ref, b_ref, g_ref, s0_ref, rk_ref, lnw_ref, lnb_ref,
               o_ref, sT_ref, st_ref, *, c, bb, nh, hd):
    ci = pl.program_id(1)
    units = [(bi, h) for bi in range(bb) for h in range(nh)]

    @pl.when(ci == 0)
    def _():
        for bi, h in units:
            st_ref[bi, h] = s0_ref[bi, h].T

    rows = lax.broadcasted_iota(jnp.int32, (c, c), 0)
    cols = lax.broadcasted_iota(jnp.int32, (c, c), 1)
    tri = (rows >= cols).astype(F32)
    rows2 = lax.broadcasted_iota(jnp.int32, (c, 2 * c), 0)
    cols2 = lax.broadcasted_iota(jnp.int32, (c, 2 * c), 1)
    cols2 = jnp.where(cols2 >= c, cols2 - c, cols2)
    strict2 = rows2 > cols2
    incl2 = rows2 >= cols2
    ones_cols = jnp.ones((c, hd), F32)
    zeros_cv = jnp.zeros((c, hd), F32)

    ar, bk, bk_rem, v_u, decay = [], [], [], [], []
    for bi in range(bb):
        r = r_ref[bi]
        lw = lw_ref[bi]
        k = k_ref[bi]
        a = a_ref[bi]
        b = b_ref[bi]
        v = v_ref[bi]
        cum = _mm(tri, lw, pa=1, pb=3)
        cum_last = cum[c - 1:c, :]
        e_neg = jnp.exp(-cum)
        e_rem = jnp.exp(cum_last - cum)
        at = a * jnp.exp(cum - lw)
        rt = r * jnp.exp(cum)
        bt = b * e_neg
        kt = k * e_neg
        bh = b * e_rem
        kh = k * e_rem
        cum_cols = _mm(lw, ones_cols, dims=TN_DIMS, pa=3, pb=1)
        for h in range(nh):
            sl = slice(h * hd, (h + 1) * hd)
            ar.append(jnp.concatenate([at[:, sl], rt[:, sl]], axis=0))
            bk.append(jnp.concatenate([bt[:, sl], kt[:, sl]], axis=0))
            bk_rem.append(jnp.concatenate([bh[:, sl], kh[:, sl]], axis=0))
            v_u.append(v[:, sl])
            decay.append(jnp.exp(cum_cols[h * hd:(h + 1) * hd, :]))

    st = [st_ref[bi, h] for bi, h in units]
    m4 = _mm_each(ar, bk, dims=NT_DIMS)
    m_a = [jnp.where(strict2, m[0:c], 0.0) for m in m4]
    m_r = [jnp.where(incl2, m[c:2 * c], 0.0) for m in m4]
    t_inv = _unit_lower_inverse_each([m[:, 0:c] for m in m_a])
    g1 = _mm_each([x[0:c] for x in ar], st)
    g2 = _mm_each(m_a, [jnp.concatenate([zeros_cv, vh], axis=0) for vh in v_u])
    u = _mm_each(t_inv, [x + y for x, y in zip(g1, g2)])
    uv = [jnp.concatenate([ui, vh], axis=0) for ui, vh in zip(u, v_u)]
    y1 = _mm_each([x[c:2 * c] for x in ar], st)
    y2 = _mm_each(m_r, uv)
    st_upd = _mm_each(bk_rem, uv, dims=TN_DIMS)
    for (bi, h), s_old, d, upd in zip(units, st, decay, st_upd):
        st_ref[bi, h] = s_old * d + upd

    rk = rk_ref[...]
    for bi in range(bb):
        r = r_ref[bi]
        k = k_ref[bi]
        rkk = r * k * rk
        yn_parts, bonus_parts = [], []
        for h in range(nh):
            i = bi * nh + h
            sl = slice(h * hd, (h + 1) * hd)
            y = y1[i] + y2[i]
            mean = jnp.mean(y, axis=-1, keepdims=True)
            yc = y - mean
            var = jnp.mean(yc * yc, axis=-1, keepdims=True)
            yn_parts.append(yc * lax.rsqrt(var + LNX_EPS))
            bonus_parts.append(jnp.sum(rkk[:, sl], axis=-1, keepdims=True) * v_u[i])
        yn_all = jnp.concatenate(yn_parts, axis=1)
        bonus_all = jnp.concatenate(bonus_parts, axis=1)
        o_ref[bi] = (yn_all * lnw_ref[...] + lnb_ref[...] + bonus_all) * g_ref[bi]

    @pl.when(ci == pl.num_programs(1) - 1)
    def _():
        for bi, h in units:
            sT_ref[bi, h] = st_ref[bi, h].T


def _rwkv_scan(r, lw, k, v, a, b, g, s0, r_k, lnx_w, lnx_b):
    bsz, t, db = r.shape
    _, nh, hd, _ = s0.shape
    c = min(t, 64)
    bb = 4 if bsz % 4 == 0 else 1
    seq = pl.BlockSpec((bb, c, db), lambda i, j: (i, j, 0))
    row = pl.BlockSpec((1, db), lambda i, j: (0, 0))
    st_spec = pl.BlockSpec((bb, nh, hd, hd), lambda i, j: (i, 0, 0, 0))
    return pl.pallas_call(
        functools.partial(_scan_body, c=c, bb=bb, nh=nh, hd=hd),
        grid=(bsz // bb, t // c),
        in_specs=[seq] * 7 + [st_spec, row, row, row],
        out_specs=[seq, st_spec],
        out_shape=[jax.ShapeDtypeStruct((bsz, t, db), F32), jax.ShapeDtypeStruct((bsz, nh, hd, hd), F32)],
        scratch_shapes=[pltpu.VMEM((bb, nh, hd, hd), F32)],
        compiler_params=_params(("parallel", "arbitrary")),
        name="rwkv_scan",
    )(r, lw, k, v, a, b, g, s0, r_k, lnx_w, lnx_b)


def _lam_value(lq1, lk1, lq2, lk2, lam_init):
    s1 = jnp.sum(lq1 * lk1, axis=-1, keepdims=True)
    s2 = jnp.sum(lq2 * lk2, axis=-1, keepdims=True)
    return jnp.exp(s1) - jnp.exp(s2) + lam_init


def _stack_maps(q, tn, dh):
    q2x = jnp.concatenate([q, q], axis=0)
    row = lax.broadcasted_iota(jnp.int32, q2x.shape, 0)
    lane = lax.broadcasted_iota(jnp.int32, q2x.shape, 1)
    keep = (row < tn) == (lane < dh)
    return jnp.where(keep, q2x, 0.0).astype(BF16)


def _lane_tile(x, width):
    reps = width // V7X_LANES
    return x if reps == 1 else jnp.concatenate([x] * reps, axis=1)


def _diff_finish(o12, tn, lam, subln_w, lam_init):
    o = o12[0:tn] - lam * o12[tn:2 * tn]
    return _rms(o, subln_w, SUBLN_EPS) * (1.0 - lam_init)


PATTN_COLS = 128
PATTN_KEYS = 256
PATTN_HEADS = 4
BIAS_PIECES = 3


def _pattn_body(qi_ref, ki_ref, q_ref, k_ref, v_ref, slope_ref, lq1_ref, lk1_ref, lq2_ref, lk2_ref, sw_ref,
                o_ref, q12_s, ktab_s, s_s, p_s, m_s, l_s, al_s, acc_s, *, tq, dh, hps, n_heads, pk, lam_init):
    step = pl.program_id(2)
    qi = qi_ref[step]
    ki = ki_ref[step]
    nq2 = 2 * tq
    hw = 2 * dh
    heads = range(hps)

    @pl.when(ki == 0)
    def _():
        lane = lax.broadcasted_iota(jnp.int32, (nq2, V7X_LANES), 1)
        ones = jnp.where(lane < BIAS_PIECES, 1.0, 0.0).astype(BF16)
        klane = lax.broadcasted_iota(jnp.int32, (tq, V7X_LANES), 1)
        kpos = lax.broadcasted_iota(jnp.int32, (tq, V7X_LANES), 0).astype(F32)
        for g in heads:
            q12_s[g] = jnp.concatenate([_stack_maps(q_ref[0][:, g * hw:(g + 1) * hw], tq, dh), ones], axis=1)
            rem = slope_ref[g][:, 0:1] * kpos
            tab = jnp.zeros((tq, V7X_LANES), F32)
            for i in range(BIAS_PIECES):
                piece = rem.astype(BF16).astype(F32)
                tab = jnp.where(klane == i, piece, tab)
                rem = rem - piece
            ktab_s[g] = tab.astype(BF16)
        m_s[...] = jnp.full(m_s.shape, NEG_BIG, F32)
        l_s[...] = jnp.zeros(l_s.shape, F32)
        acc_s[...] = jnp.zeros(acc_s.shape, F32)

    def update(diagonal):
        k_all = k_ref[0]
        v_all = v_ref[0]
        vt_bf = []
        for g in heads:
            k_aug = jnp.concatenate([k_all[:, g * hw:(g + 1) * hw].astype(BF16), ktab_s[g]], axis=1)
            s_s[g] = lax.dot_general(k_aug, q12_s[g], NT_DIMS, preferred_element_type=F32)
            vt_bf.append(v_all[:, g * hw:(g + 1) * hw].T.astype(BF16))
        base = ((ki - qi) * tq).astype(F32)
        for k0 in range(0, tq, pk):
            for g in heads:
                shift = slope_ref[g][:, 0:1] * base
                for c0 in range(0, nq2, PATTN_COLS):
                    q0 = c0 % tq
                    cols = pl.ds(c0, PATTN_COLS)
                    if diagonal and k0 > q0 + PATTN_COLS - 1:
                        p_s[g, :, cols] = jnp.zeros((pk, PATTN_COLS), BF16)
                        al_s[g, :, cols] = jnp.ones((1, PATTN_COLS), F32)
                        continue
                    sv = s_s[g, pl.ds(k0, pk), cols]
                    if diagonal and k0 + pk - 1 > q0:
                        qpos = q0 + lax.broadcasted_iota(jnp.int32, (pk, PATTN_COLS), 1)
                        kpos = k0 + lax.broadcasted_iota(jnp.int32, (pk, PATTN_COLS), 0)
                        sv = jnp.where(qpos >= kpos, sv, NEG_BIG)
                    m_old = m_s[g, :, cols]
                    m_new = jnp.maximum(m_old, jnp.max(sv, axis=0, keepdims=True) + shift)
                    alpha = jnp.exp2(m_old - m_new)
                    p = jnp.exp2(sv - (m_new - shift))
                    m_s[g, :, cols] = m_new
                    l_s[g, :, cols] = alpha * l_s[g, :, cols] + jnp.sum(p, axis=0, keepdims=True)
                    al_s[g, :, cols] = alpha
                    p_s[g, :, cols] = p.astype(BF16)
            for g in heads:
                acc_s[g] = al_s[g] * acc_s[g] + jnp.dot(vt_bf[g][:, k0:k0 + pk], p_s[g],
                                                        preferred_element_type=F32)

    @pl.when(ki < qi)
    def _():
        update(False)

    @pl.when(ki == qi)
    def _():
        update(True)
        lam = _lam_value(lq1_ref[...], lk1_ref[...], lq2_ref[...], lk2_ref[...], lam_init)
        outs = []
        for g in heads:
            o12 = (acc_s[g] / l_s[g]).T
            outs.append(_diff_finish(o12, tq, lam, sw_ref[...], lam_init))
        o_ref[0] = outs[0] if hps == 1 else jnp.concatenate(outs, axis=1)


def _prompt_attention(q, k, v, slopes, lq1, lk1, lq2, lk2, subln_w, lam_init, n_heads):
    b, t, w = q.shape
    hw = w // n_heads
    assert hw == V7X_LANES, "one head's two maps fill exactly one lane tile"
    dh = hw // 2
    tq = min(t, 512)
    nq = t // tq
    hps = math.gcd(n_heads, PATTN_HEADS)
    pk = min(PATTN_KEYS, tq)
    pairs = [(i, j) for i in range(nq) for j in range(i + 1)]
    qi_tbl = jnp.asarray([p[0] for p in pairs], jnp.int32)
    ki_tbl = jnp.asarray([p[1] for p in pairs], jnp.int32)
    qspec = pl.BlockSpec((1, tq, hps * hw), lambda bi, h, s, qt, kt: (bi, qt[s], h))
    kspec = pl.BlockSpec((1, tq, hps * hw), lambda bi, h, s, qt, kt: (bi, kt[s], h))
    small = lambda n: pl.BlockSpec((1, n), lambda bi, h, s, qt, kt: (0, 0))
    row = pltpu.VMEM((hps, 1, 2 * tq), F32)
    grid_spec = pltpu.PrefetchScalarGridSpec(
        num_scalar_prefetch=2,
        grid=(b, n_heads // hps, len(pairs)),
        in_specs=[qspec, kspec, kspec,
                  pl.BlockSpec((hps, 1, V7X_LANES), lambda bi, h, s, qt, kt: (h, 0, 0)),
                  small(dh), small(dh), small(dh), small(dh), small(hw)],
        out_specs=qspec,
        scratch_shapes=[pltpu.VMEM((hps, 2 * tq, 2 * hw), BF16), pltpu.VMEM((hps, tq, V7X_LANES), BF16),
                        pltpu.VMEM((hps, tq, 2 * tq), F32),
                        pltpu.VMEM((hps, pk, 2 * tq), BF16), row, row, row,
                        pltpu.VMEM((hps, hw, 2 * tq), F32)],
    )
    return pl.pallas_call(
        functools.partial(_pattn_body, tq=tq, dh=dh, hps=hps, n_heads=n_heads, pk=pk, lam_init=lam_init),
        grid_spec=grid_spec,
        out_shape=jax.ShapeDtypeStruct((b, t, w), F32),
        compiler_params=_params(("parallel", "parallel", "arbitrary")),
        name="prompt_attention",
    )(qi_tbl, ki_tbl, q, k, v, slopes, lq1, lk1, lq2, lk2, subln_w)


SATTN_PAGES = 32


def _sattn_body(pt_ref, q_ref, kn_ref, vn_ref, slope_ref, lq1_ref, lk1_ref, lq2_ref, lk2_ref, sw_ref, *rest,
                ppb, page, n_heads, tn, dh, past_len, lam_init):
    k_refs = rest[:ppb]
    v_refs = rest[ppb:2 * ppb]
    o_ref, m_s, l_s, acc_s = rest[2 * ppb:]
    del pt_ref
    si = pl.program_id(1)
    hw = 2 * dh
    width = ppb * page
    heads = range(n_heads)

    @pl.when(si == 0)
    def _():
        m_s[...] = jnp.full(m_s.shape, NEG_BIG, F32)
        l_s[...] = jnp.zeros(l_s.shape, F32)
        acc_s[...] = jnp.zeros(acc_s.shape, F32)

    q = q_ref[0]
    qm = [q[:, m * dh:(m + 1) * dh].astype(BF16) for m in range(2 * n_heads)]

    def scores(h, key_of_map, dims):
        sa = lax.dot_general(qm[2 * h], key_of_map(2 * h), dims, preferred_element_type=F32)
        sb = lax.dot_general(qm[2 * h + 1], key_of_map(2 * h + 1), dims, preferred_element_type=F32)
        return jnp.concatenate([sa, sb], axis=0)

    kpos = si * width + lax.broadcasted_iota(jnp.int32, (1, width), 1) - past_len
    kpos = kpos.astype(F32)
    s_all = []
    for h in heads:
        parts = [scores(h, lambda m, j=j: k_refs[j][m].astype(BF16), NN_DIMS) for j in range(ppb)]
        s_all.append(jnp.concatenate(parts, axis=1) + slope_ref[h][:, 0:1] * kpos)
    alphas, ps = [], []
    for h in heads:
        m_old = m_s[h]
        m_new = jnp.maximum(m_old, jnp.max(s_all[h], axis=-1, keepdims=True))
        alpha = jnp.exp2(m_old - m_new)
        p = jnp.exp2(s_all[h] - _lane_tile(m_new, width))
        l_s[h] = alpha * l_s[h] + jnp.sum(p, axis=-1, keepdims=True)
        m_s[h] = m_new
        alphas.append(alpha)
        ps.append(p.astype(BF16))
    for h in heads:
        pv = None
        for j in range(ppb):
            t = jnp.dot(ps[h][:, j * page:(j + 1) * page],
                        v_refs[j][pl.ds(h, page, stride=n_heads), :].astype(BF16),
                        preferred_element_type=F32)
            pv = t if pv is None else pv + t
        acc_s[h] = alphas[h] * acc_s[h] + pv

    @pl.when(si == pl.num_programs(1) - 1)
    def _():
        lam = _lam_value(lq1_ref[...], lk1_ref[...], lq2_ref[...], lk2_ref[...], lam_init)
        kn = kn_ref[0]
        vn = vn_ref[0]
        row_n = lax.broadcasted_iota(jnp.int32, (2 * tn, tn), 0)
        col_n = lax.broadcasted_iota(jnp.int32, (2 * tn, tn), 1)
        causal = jnp.where(row_n >= tn, row_n - tn, row_n) >= col_n
        outs = []
        for h in heads:
            s = scores(h, lambda m: kn[:, m * dh:(m + 1) * dh].astype(BF16), NT_DIMS)
            s = jnp.where(causal, s + slope_ref[h][:, 0:1] * col_n.astype(F32), NEG_BIG)
            m_old = m_s[h]
            m_new = jnp.maximum(m_old, jnp.max(s, axis=-1, keepdims=True))
            alpha = jnp.exp2(m_old - m_new)
            p = jnp.exp2(s - m_new[:, 0:tn])
            l_fin = alpha * l_s[h] + jnp.sum(p, axis=-1, keepdims=True)
            acc = alpha * acc_s[h] + jnp.dot(p.astype(BF16), vn[:, h * hw:(h + 1) * hw].astype(BF16),
                                             preferred_element_type=F32)
            outs.append(_diff_finish(acc / l_fin, tn, lam, sw_ref[...], lam_init))
        o_ref[0] = jnp.concatenate(outs, axis=1)


def _sample_attention(q, k_new, v_new, cache_kt, cache_v, page_table, layer, slopes, lq1, lk1, lq2, lk2, subln_w,
                      lam_init):
    b, tn, w = q.shape
    _, _, maps, dh, page = cache_kt.shape
    n_heads = maps // 2
    hw = 2 * dh
    assert hw == V7X_LANES, "softmax statistics are kept lane-replicated at the head width"
    n_pages = page_table.shape[1]
    ppb = math.gcd(n_pages, SATTN_PAGES)
    n_steps = n_pages // ppb
    pt_flat = page_table.reshape(-1)

    def page_index(j, nd):
        return lambda bi, si, pt: (layer, pt[bi * n_pages + si * ppb + j]) + (0,) * nd

    k_specs = [pl.BlockSpec((None, None, maps, dh, page), page_index(j, 3)) for j in range(ppb)]
    v_specs = [pl.BlockSpec((None, None, page * n_heads, hw), page_index(j, 2)) for j in range(ppb)]
    new_spec = pl.BlockSpec((1, tn, w), lambda bi, si, pt: (bi, 0, 0))
    small = lambda n: pl.BlockSpec((1, n), lambda bi, si, pt: (0, 0))
    grid_spec = pltpu.PrefetchScalarGridSpec(
        num_scalar_prefetch=1,
        grid=(b, n_steps),
        in_specs=[new_spec, new_spec, new_spec,
                  pl.BlockSpec((n_heads, 1, V7X_LANES), lambda bi, si, pt: (0, 0, 0)),
                  small(dh), small(dh), small(dh), small(dh), small(hw)] + k_specs + v_specs,
        out_specs=new_spec,
        scratch_shapes=[pltpu.VMEM((n_heads, 2 * tn, hw), F32), pltpu.VMEM((n_heads, 2 * tn, hw), F32),
                        pltpu.VMEM((n_heads, 2 * tn, hw), F32)],
    )
    return pl.pallas_call(
        functools.partial(_sattn_body, ppb=ppb, page=page, n_heads=n_heads, tn=tn, dh=dh,
                          past_len=n_pages * page, lam_init=lam_init),
        grid_spec=grid_spec,
        out_shape=jax.ShapeDtypeStruct((b, tn, w), F32),
        compiler_params=_params(("parallel", "arbitrary")),
        name="sample_attention",
    )(pt_flat, q, k_new, v_new, slopes, lq1, lk1, lq2, lk2, subln_w,
      *([cache_kt] * ppb), *([cache_v] * ppb))


def _merge_body(x_ref, ya_ref, yb_ref, yc_ref, gt_ref, wa_ref, wb_ref, wc_ref, wo_ref, o_ref, *, d):
    gates = gt_ref[...]
    merged = jax.nn.sigmoid(gates[:, 0:d]) * jnp.dot(ya_ref[...].astype(BF16), wa_ref[...], preferred_element_type=F32)
    merged = merged + jax.nn.sigmoid(gates[:, d:2 * d]) * jnp.dot(
        yb_ref[...].astype(BF16), wb_ref[...], preferred_element_type=F32)
    merged = merged + jax.nn.sigmoid(gates[:, 2 * d:3 * d]) * jnp.dot(
        yc_ref[...].astype(BF16), wc_ref[...], preferred_element_type=F32)
    o_ref[...] = x_ref[...] + jnp.dot(merged.astype(BF16), wo_ref[...], preferred_element_type=F32)


def _merge(x, ya, yb, yc, gates, wa, wb, wc, wo, layer):
    m, d = x.shape
    tm = min(m, 512)
    tile = lambda n: pl.BlockSpec((tm, n), lambda i: (i, 0))
    return pl.pallas_call(
        functools.partial(_merge_body, d=d),
        grid=(m // tm,),
        in_specs=[tile(d), tile(ya.shape[1]), tile(yb.shape[1]), tile(yc.shape[1]), tile(3 * d),
                  _layer_spec(wa, layer), _layer_spec(wb, layer), _layer_spec(wc, layer), _layer_spec(wo, layer)],
        out_specs=tile(d),
        out_shape=jax.ShapeDtypeStruct((m, d), F32),
        compiler_params=_params(("parallel",)),
        name="merge",
    )(x, ya, yb, yc, gates, wa, wb, wc, wo)


STACKED_BF16 = ("ffn1_w_gate", "ffn1_w_up", "ffn1_w_down", "ffn2_w_gate", "ffn2_w_up", "ffn2_w_down", "w_in",
                "w_branch_a", "w_branch_b", "w_branch_c", "w_out")


def _prep_layer_weights(l, W, big, dims):
    db, r_w, r_a = dims["db"], dims["r_w"], dims["r_a"]
    row = lambda a: a.reshape(1, -1)

    def ffn(prefix):
        return (row(W[prefix + "_norm"][l]), big[prefix + "_w_gate"], big[prefix + "_w_up"], big[prefix + "_w_down"])

    zeros_w = jnp.zeros((r_a, db), BF16)
    zeros_a = jnp.zeros((r_w, db), BF16)
    head = jnp.arange(db) // dims["hd"]
    return dict(
        ffn1=ffn("ffn1"), ffn2=ffn("ffn2"),
        mix_norm=row(W["mix_norm"][l]), w_in=big["w_in"],
        conv_w=W["conv_w"][l],
        mu=row(W["tm_mu"][l]),
        wpad=jnp.concatenate([W["w_up_w"][l].astype(BF16), zeros_w], axis=0),
        apad=jnp.concatenate([zeros_a, W["a_up_w"][l].astype(BF16)], axis=0),
        w0=row(W["w0"][l]), a0=row(W["a0"][l]), gup=W["g_up_w"][l].astype(BF16),
        k_k=row(W["k_k"][l]), k_a=row(W["k_a"][l]), r_k=row(W["r_k"][l]),
        lnx_w=row(W["lnx_w"][l]), lnx_b=row(W["lnx_b"][l]),
        ones_bd=(head[:, None] == head[None, :]).astype(BF16),
        lq1=row(W["lam_q1"][l]), lk1=row(W["lam_k1"][l]), lq2=row(W["lam_q2"][l]), lk2=row(W["lam_k2"][l]),
        subln_w=row(W["subln_w"][l]),
        wa=big["w_branch_a"], wb=big["w_branch_b"], wc=big["w_branch_c"], wo=big["w_out"],
    )


def _run_trunk(x, layers, final_w, conv0, shift0, wkv0, attend, dims):
    b, t, d = x.shape
    m = b * t
    da, db, ns = dims["da"], dims["db"], dims["ns"]
    qw, kw, dc = dims["qw"], dims["kw"], dims["dc"]
    widths = (3 * da, ns, qw, kw, dc, 3 * d)
    Q_OUT, K_OUT, V_OUT = 2, 3, 4
    depth = len(layers)
    xf = x.reshape(m, d)
    k_rows, wkv_out, shift_out, conv_out = [], [], [], []
    stacks = None
    for l, lw in enumerate(layers):
        xf = _ffn(xf, *lw["ffn1"], l)
        cin, z, q, k, v, gates, *stacks = _mixin(xf, lw["mix_norm"], lw["w_in"], l, widths, Q_OUT,
                                                 dims["dh"] ** -0.5 * LOG2_E, K_OUT, V_OUT, dims["hc"], t, stacks)
        y_a, conv_new = _short_conv(cin.reshape(b, t, 3 * da), conv0[l], lw["conv_w"])
        r_, lw_, k_, v_, a_, b_, g_, shift_new = _rwkv_pre(
            z.reshape(b, t, ns), shift0[l].reshape(b, 1, ns), lw["mu"], lw["wpad"], lw["w0"], lw["apad"], lw["a0"],
            lw["gup"], lw["k_k"], lw["k_a"], lw["ones_bd"], db)
        y_b, wkv_new = _rwkv_scan(r_, lw_, k_, v_, a_, b_, g_, wkv0[l], lw["r_k"], lw["lnx_w"], lw["lnx_b"])
        lam_init = 0.8 - 0.6 * math.exp(-0.3 * l)
        y_c = attend(l, q.reshape(b, t, qw), k.reshape(b, t, kw), v.reshape(b, t, dc), lw, lam_init)
        xf = _merge(xf, y_a.reshape(m, da), y_b.reshape(m, db), y_c.reshape(m, dc), gates,
                    lw["wa"], lw["wb"], lw["wc"], lw["wo"], l)
        xf = _ffn(xf, *lw["ffn2"], l, final_w=final_w if l == depth - 1 else None)
        k_rows.append(k.reshape(b, t, 2 * dims["hc"], dims["dh"]))
        wkv_out.append(wkv_new)
        shift_out.append(shift_new.reshape(b, ns))
        conv_out.append(conv_new)
    v_all = stacks[0].reshape(depth, b, t, dims["hc"], 2 * dims["dh"])
    if len(stacks) > 1:
        k_all = jnp.transpose(stacks[1].reshape(depth, b, 2 * dims["hc"], dims["dh"], t), (0, 1, 4, 2, 3))
    else:
        k_all = jnp.stack(k_rows)
    return (xf.reshape(b, t, d), k_all, v_all, jnp.stack(wkv_out), jnp.stack(shift_out), jnp.stack(conv_out))


def kernel(x_prompt, x_sample, cache_k, cache_v, state_wkv, state_shift, state_conv, page_table, ffn1_norm, ffn1_w_gate, ffn1_w_up, ffn1_w_down, mix_norm, w_in, conv_w, tm_mu, w_up_w, w0, a_up_w, a0, g_up_w, k_k, k_a, r_k, lnx_w, lnx_b, lam_q1, lam_k1, lam_q2, lam_k2, subln_w, w_branch_a, w_branch_b, w_branch_c, w_out, ffn2_norm, ffn2_w_gate, ffn2_w_up, ffn2_w_down, final_norm):
    W = dict(ffn1_norm=ffn1_norm, ffn1_w_gate=ffn1_w_gate, ffn1_w_up=ffn1_w_up, ffn1_w_down=ffn1_w_down,
             mix_norm=mix_norm, w_in=w_in, conv_w=conv_w, tm_mu=tm_mu, w_up_w=w_up_w, w0=w0,
             a_up_w=a_up_w, a0=a0, g_up_w=g_up_w, k_k=k_k, k_a=k_a, r_k=r_k, lnx_w=lnx_w, lnx_b=lnx_b,
             lam_q1=lam_q1, lam_k1=lam_k1, lam_q2=lam_q2, lam_k2=lam_k2, subln_w=subln_w,
             w_branch_a=w_branch_a, w_branch_b=w_branch_b, w_branch_c=w_branch_c, w_out=w_out,
             ffn2_norm=ffn2_norm, ffn2_w_gate=ffn2_w_gate, ffn2_w_up=ffn2_w_up, ffn2_w_down=ffn2_w_down)
    depth = w_in.shape[0]
    d = x_prompt.shape[-1]
    _, n_pool, page, maps, dh = cache_k.shape
    hc = maps // 2
    nh, hd = r_k.shape[1], r_k.shape[2]
    db = nh * hd
    da = conv_w.shape[-1]
    assert conv_w.shape[1] == 3, "short conv kernel is written for width 3"
    r_w, r_a, r_g = w_up_w.shape[1], a_up_w.shape[1], g_up_w.shape[1]
    ns = state_shift.shape[-1]
    assert ns == 3 * db + r_w + r_a + r_g
    assert (r_w + r_a) % V7X_LANES == 0 and r_g % V7X_LANES == 0
    dims = dict(d=d, d_ff=ffn1_w_gate.shape[-1], da=da, db=db, hd=hd, ns=ns, r_w=r_w, r_a=r_a,
                qw=maps * dh, kw=maps * dh, dc=hc * 2 * dh, hc=hc, dh=dh)
    big = {name: W[name].astype(BF16) for name in STACKED_BF16}
    layers = [_prep_layer_weights(l, W, big, dims) for l in range(depth)]
    final_w = final_norm.reshape(1, d)
    slopes = jnp.broadcast_to(
        jnp.asarray([LOG2_E * 2.0 ** (-8.0 * (h + 1) / hc) for h in range(hc)], F32)[:, None, None],
        (hc, 1, V7X_LANES))
    cache_kt = jnp.transpose(cache_k, (0, 1, 3, 4, 2))
    cache_v2 = cache_v.reshape(depth, n_pool, page * hc, 2 * dh)

    def attend_prompt(l, q, k, v, lw, lam_init):
        return _prompt_attention(q, k, v, slopes, lw["lq1"], lw["lk1"], lw["lq2"], lw["lk2"], lw["subln_w"],
                                 lam_init, hc)

    def attend_sample(l, q, k, v, lw, lam_init):
        return _sample_attention(q, k, v, cache_kt, cache_v2, page_table, l, slopes, lw["lq1"], lw["lk1"],
                                 lw["lq2"], lw["lk2"], lw["subln_w"], lam_init)

    bp = x_prompt.shape[0]
    conv0 = jnp.zeros((depth, bp, 2, da), F32)
    shift0 = jnp.zeros((depth, bp, ns), F32)
    wkv0 = jnp.zeros((depth, bp, nh, hd, hd), F32)
    y_p, k_p, v_p, wkv_p, shift_p, conv_p = _run_trunk(
        x_prompt, layers, final_w, conv0, shift0, wkv0, attend_prompt, dims)
    y_s, k_s, v_s, wkv_s, shift_s, conv_s = _run_trunk(
        x_sample, layers, final_w, state_conv, state_shift, state_wkv, attend_sample, dims)
    return (y_p, y_s, k_p, v_p, k_s, v_s, wkv_p, wkv_s, shift_p, shift_s, conv_p, conv_s)
```

```python
import functools
import math

import jax
import jax.numpy as jnp
from jax import lax
from jax.experimental import pallas as pl
from jax.experimental.pallas import tpu as pltpu

F32 = jnp.float32
BF16 = jnp.bfloat16

NORM_EPS = 1e-6
LNX_EPS = 64e-5
SUBLN_EPS = 1e-5
KK_EPS = 1e-24
NEG_BIG = -1e30
LOG2_E = math.log2(math.e)

V7X_LANES = 128
V7X_SUBLANES = 8
V7X_VMEM_LIMIT_BYTES = 56 * 1024 * 1024

NT_DIMS = (((1,), (1,)), ((), ()))
NN_DIMS = (((1,), (0,)), ((), ()))
TN_DIMS = (((0,), (0,)), ((), ()))


def _params(semantics):
    return pltpu.CompilerParams(dimension_semantics=semantics, vmem_limit_bytes=V7X_VMEM_LIMIT_BYTES)


def _const_spec(shape):
    nd = len(shape)
    return pl.BlockSpec(shape, lambda *_: (0,) * nd, pipeline_mode=pl.Buffered(1))


def _layer_spec(w, layer):
    nd = w.ndim - 1
    return pl.BlockSpec((None,) + w.shape[1:], lambda *_: (layer,) + (0,) * nd, pipeline_mode=pl.Buffered(1))


def _act_dtype(rows):
    return BF16 if rows % 16 == 0 else F32


def _rms(x, w, eps):
    ms = jnp.mean(x * x, axis=-1, keepdims=True)
    return x * lax.rsqrt(ms + eps) * w


def _split_bf16(x, n):
    pieces = []
    rem = x
    for i in range(n):
        p = rem.astype(BF16)
        pieces.append(p)
        if i + 1 < n:
            rem = rem - p.astype(F32)
    return pieces


def _mm(a, b, dims=NN_DIMS, pa=1, pb=1):
    a_p = _split_bf16(a, pa)
    b_p = _split_bf16(b, pb)
    order = max(pa, pb)
    acc = None
    for i, ai in enumerate(a_p):
        for j, bj in enumerate(b_p):
            if i + j >= order:
                continue
            t = lax.dot_general(ai, bj, dims, preferred_element_type=F32)
            acc = t if acc is None else acc + t
    return acc


FFN_CHUNK = 256


def _ffn_body(x_ref, nw_ref, wg_ref, wu_ref, wd_ref, *rest, n_chunks, fc, final):
    if final:
        fw_ref, o_ref, acc_ref = rest
    else:
        o_ref, acc_ref = rest
    x = x_ref[...]
    h = _rms(x, nw_ref[...], NORM_EPS).astype(BF16)
    for c in range(n_chunks):
        cs = slice(c * fc, (c + 1) * fc)
        g = jnp.dot(h, wg_ref[:, cs], preferred_element_type=F32)
        u = jnp.dot(h, wu_ref[:, cs], preferred_element_type=F32)
        act = (g * jax.nn.sigmoid(g) * u).astype(BF16)
        d = jnp.dot(act, wd_ref[cs, :], preferred_element_type=F32)
        if c == 0:
            acc_ref[...] = d
        else:
            acc_ref[...] += d
    y = x + 0.5 * acc_ref[...]
    if final:
        y = _rms(y, fw_ref[...], NORM_EPS)
    o_ref[...] = y


def _ffn(x, norm_w, wg, wu, wd, layer, final_w=None):
    m, d = x.shape
    d_ff = wg.shape[2]
    fc = FFN_CHUNK if d_ff % FFN_CHUNK == 0 else V7X_LANES
    n_chunks = d_ff // fc
    tm = min(m, 512)
    final = final_w is not None
    in_specs = [
        pl.BlockSpec((tm, d), lambda i: (i, 0)),
        _const_spec((1, d)),
        _layer_spec(wg, layer),
        _layer_spec(wu, layer),
        _layer_spec(wd, layer),
    ]
    args = [x, norm_w, wg, wu, wd]
    if final:
        in_specs.append(_const_spec((1, d)))
        args.append(final_w)
    return pl.pallas_call(
        functools.partial(_ffn_body, n_chunks=n_chunks, fc=fc, final=final),
        grid=(m // tm,),
        in_specs=in_specs,
        out_specs=pl.BlockSpec((tm, d), lambda i: (i, 0)),
        out_shape=jax.ShapeDtypeStruct((m, d), F32),
        scratch_shapes=[pltpu.VMEM((tm, d), F32)],
        compiler_params=_params(("parallel",)),
        name="ffn",
    )(*args)


def _mixin_body(x_ref, nw_ref, w_ref, *refs, widths, n_carried, q_index, q_scale, k_index, v_index, v_heads,
                col_tile, tm):
    out_refs = refs[n_carried:]
    vil_ref = out_refs[len(widths)]
    kt_ref = out_refs[len(widths) + 1] if len(out_refs) > len(widths) + 1 else None
    h = _rms(x_ref[...], nw_ref[...], NORM_EPS).astype(BF16)
    start = 0
    for idx, (ref, width) in enumerate(zip(out_refs, widths)):
        for c0 in range(0, width, col_tile):
            wd = min(col_tile, width - c0)
            r = jnp.dot(h, w_ref[:, start + c0:start + c0 + wd], preferred_element_type=F32)
            if idx == q_index:
                r = r * q_scale
            ref[:, c0:c0 + wd] = r.astype(ref.dtype)
            if idx == v_index:
                hw = width // v_heads
                for hh in range(c0 // hw, (c0 + wd) // hw):
                    vil_ref[pl.ds(hh, tm, stride=v_heads), :] = r[:, hh * hw - c0:(hh + 1) * hw - c0]
            if idx == k_index and kt_ref is not None:
                kt_ref[c0:c0 + wd, :] = r.T
        start += width


def _mixin(x, norm_w, w_in, layer, widths, q_index, q_scale, k_index, v_index, v_heads, seq_len, carried, bf16_outs):
    m, d = x.shape
    depth = w_in.shape[0]
    tm = min(m, 512)
    hw = widths[v_index] // v_heads
    kw = widths[k_index]
    emit_kt = seq_len % tm == 0
    tiles_per_seq = seq_len // tm if emit_kt else 1
    stack_specs = [pl.BlockSpec((None, tm * v_heads, hw), lambda i: (layer, i, 0))]
    stack_shapes = [jax.ShapeDtypeStruct((depth, m * v_heads, hw), F32)]
    if emit_kt:
        stack_specs.append(pl.BlockSpec((None, None, kw, tm),
                                        lambda i: (layer, i // tiles_per_seq, 0, i % tiles_per_seq)))
        stack_shapes.append(jax.ShapeDtypeStruct((depth, m // seq_len, kw, seq_len), F32))
    carried = list(carried) if carried is not None else []
    n_in = 3
    return pl.pallas_call(
        functools.partial(_mixin_body, widths=widths, n_carried=len(carried), q_index=q_index, q_scale=q_scale,
                          k_index=k_index, v_index=v_index, v_heads=v_heads, col_tile=512, tm=tm),
        grid=(m // tm,),
        in_specs=[pl.BlockSpec((tm, d), lambda i: (i, 0)), _const_spec((1, d)), _layer_spec(w_in, layer)]
                 + [pl.BlockSpec(memory_space=pl.ANY)] * len(carried),
        out_specs=[pl.BlockSpec((tm, w), lambda i: (i, 0)) for w in widths] + stack_specs,
        out_shape=[jax.ShapeDtypeStruct((m, w), _act_dtype(tm) if i in bf16_outs else F32)
                   for i, w in enumerate(widths)] + stack_shapes,
        input_output_aliases={n_in + j: len(widths) + j for j in range(len(carried))},
        compiler_params=_params(("parallel",)),
        name="mixin",
    )(x, norm_w, w_in, *carried)


def _conv_body(cin_ref, st_ref, cw_ref, ya_ref, cn_ref, buf, *, tt, da):
    @pl.when(pl.program_id(1) == 0)
    def _():
        buf[pl.ds(6, 2), :] = st_ref[0]

    c = cin_ref[0]
    xin = c[:, 0:da]
    gb = c[:, da:2 * da]
    gc = c[:, 2 * da:3 * da]
    u = gc * xin
    buf[pl.ds(8, tt), :] = u
    cw = cw_ref[...]
    y = buf[pl.ds(6, tt), :] * cw[0:1, :]
    y = y + buf[pl.ds(7, tt), :] * cw[1:2, :]
    y = y + u * cw[2:3, :]
    ya_ref[0] = (gb * y).astype(ya_ref.dtype)
    tail = buf[pl.ds(tt + 6, 2), :]
    cn_ref[0] = tail
    buf[pl.ds(6, 2), :] = tail


def _short_conv(cin, conv_state, conv_w):
    b, t, w3 = cin.shape
    da = w3 // 3
    tt = min(t, 512)
    return pl.pallas_call(
        functools.partial(_conv_body, tt=tt, da=da),
        grid=(b, t // tt),
        in_specs=[
            pl.BlockSpec((1, tt, w3), lambda i, j: (i, j, 0)),
            pl.BlockSpec((1, 2, da), lambda i, j: (i, 0, 0)),
            pl.BlockSpec((3, da), lambda i, j: (0, 0)),
        ],
        out_specs=[
            pl.BlockSpec((1, tt, da), lambda i, j: (i, j, 0)),
            pl.BlockSpec((1, 2, da), lambda i, j: (i, 0, 0)),
        ],
        out_shape=[jax.ShapeDtypeStruct((b, t, da), _act_dtype(tt)), jax.ShapeDtypeStruct((b, 2, da), F32)],
        scratch_shapes=[pltpu.VMEM((tt + 8, da), F32)],
        compiler_params=_params(("parallel", "arbitrary")),
        name="short_conv",
    )(cin, conv_state, conv_w)


def _pre_body(z_ref, ss_ref, mu_ref, wpad_ref, w0_ref, apad_ref, a0_ref, gup_ref, kk_ref, ka_ref, ones_ref,
              r_ref, lw_ref, k_ref, v_ref, a_ref, b_ref, g_ref, sn_ref, buf, *, tt, db, r_lo, r_g):
    @pl.when(pl.program_id(1) == 0)
    def _():
        buf[pl.ds(7, 1), :] = ss_ref[0]

    z = z_ref[0]
    buf[pl.ds(8, tt), :] = z
    zp = buf[pl.ds(7, tt), :]
    zm = z + (zp - z) * mu_ref[...]
    last = buf[pl.ds(tt + 7, 1), :]
    sn_ref[0] = last
    buf[pl.ds(7, 1), :] = last

    r = zm[:, 0:db]
    k = zm[:, db:2 * db]
    v = zm[:, 2 * db:3 * db]
    lo = zm[:, 3 * db:3 * db + r_lo]
    g_lo = zm[:, 3 * db + r_lo:3 * db + r_lo + r_g]
    w_part = jnp.dot(jnp.tanh(lo).astype(BF16), wpad_ref[...], preferred_element_type=F32)
    a_part = jnp.dot(lo.astype(BF16), apad_ref[...], preferred_element_type=F32)
    xw = -(w0_ref[...] + w_part)
    softplus = jnp.maximum(xw, 0.0) + jnp.log(1.0 + jnp.exp(-jnp.abs(xw)))
    w_log = -softplus - 0.5
    lw = -jnp.exp(w_log)
    a = jax.nn.sigmoid(a0_ref[...] + a_part)
    g = jnp.dot(jax.nn.sigmoid(g_lo).astype(BF16), gup_ref[...], preferred_element_type=F32)
    kk = k * kk_ref[...]
    ss = _mm(kk * kk, ones_ref[...].astype(F32), pa=2, pb=1)
    kkn = kk * lax.rsqrt(jnp.maximum(ss, KK_EPS))
    r_ref[0] = r.astype(r_ref.dtype)
    lw_ref[0] = lw
    k_ref[0] = (k * (1.0 + (a - 1.0) * ka_ref[...])).astype(k_ref.dtype)
    v_ref[0] = v.astype(v_ref.dtype)
    a_ref[0] = (-kkn).astype(a_ref.dtype)
    b_ref[0] = (kkn * a).astype(b_ref.dtype)
    g_ref[0] = g.astype(g_ref.dtype)


def _rwkv_pre(z, shift_state, mu, wpad, w0, apad, a0, gup, k_k, k_a, ones_bd, db):
    b, t, ns = z.shape
    r_lo = wpad.shape[0]
    r_g = gup.shape[0]
    tt = min(t, 512)
    row = lambda n: pl.BlockSpec((1, n), lambda i, j: (0, 0))
    full = lambda s: pl.BlockSpec(s, lambda i, j: (0, 0))
    seq = lambda n: pl.BlockSpec((1, tt, n), lambda i, j: (i, j, 0))
    return pl.pallas_call(
        functools.partial(_pre_body, tt=tt, db=db, r_lo=r_lo, r_g=r_g),
        grid=(b, t // tt),
        in_specs=[
            seq(ns),
            pl.BlockSpec((1, 1, ns), lambda i, j: (i, 0, 0)),
            row(ns), full((r_lo, db)), row(db), full((r_lo, db)), row(db), full((r_g, db)),
            row(db), row(db), full((db, db)),
        ],
        out_specs=[seq(db)] * 7 + [pl.BlockSpec((1, 1, ns), lambda i, j: (i, 0, 0))],
        out_shape=[jax.ShapeDtypeStruct((b, t, db), F32 if i == 1 else _act_dtype(tt)) for i in range(7)]
                  + [jax.ShapeDtypeStruct((b, 1, ns), F32)],
        scratch_shapes=[pltpu.VMEM((tt + 8, ns), F32)],
        compiler_params=_params(("parallel", "arbitrary")),
        name="rwkv_pre",
    )(z, shift_state, mu, wpad, w0, apad, a0, gup, k_k, k_a, ones_bd)


SCAN_SUB = 16
SCAN_P = 1


def _mm_each(a_list, b_list, dims=NN_DIMS):
    return [_mm(a, b, dims=dims, pa=SCAN_P, pb=SCAN_P) for a, b in zip(a_list, b_list)]


def _neumann_inverse_each(l_list, n):
    c = l_list[0].shape[0]
    eye = (lax.broadcasted_iota(jnp.int32, (c, c), 0) == lax.broadcasted_iota(jnp.int32, (c, c), 1)).astype(F32)
    x = [eye + l for l in l_list]
    p = l_list
    span = 2
    while span < n:
        p = _mm_each(p, p)
        xp = _mm_each(x, p)
        x = [xi + d for xi, d in zip(x, xp)]
        span *= 2
    return x


def _unit_lower_inverse_each(l_list):
    c = l_list[0].shape[0]
    if c <= SCAN_SUB:
        return _neumann_inverse_each(l_list, c)
    rows = lax.broadcasted_iota(jnp.int32, (c, c), 0) // SCAN_SUB
    cols = lax.broadcasted_iota(jnp.int32, (c, c), 1) // SCAN_SUB
    same = rows == cols
    t_diag = _neumann_inverse_each([jnp.where(same, l, 0.0) for l in l_list], SCAN_SUB)
    z = _mm_each(t_diag, [jnp.where(same, 0.0, l) for l in l_list])
    nblk = c // SCAN_SUB
    factors = []
    span = 1
    while span < nblk:
        factors.append(z)
        span *= 2
        if span < nblk:
            z = _mm_each(z, z)
    out = t_diag
    for f in reversed(factors):
        d = _mm_each(f, out)
        out = [o + di for o, di in zip(out, d)]
    return out


def _scan_body(r_ref, lw_ref, k_ref, v_ref, a_ref, b_ref, g_ref, s0_ref, rk_ref, lnw_ref, lnb_ref,
               o_ref, sT_ref, st_ref, *, c, bb, nh, hd):
    ci = pl.program_id(1)
    units = [(bi, h) for bi in range(bb) for h in range(nh)]

    @pl.when(ci == 0)
    def _():
        for bi, h in units:
            st_ref[bi, h] = s0_ref[bi, h].T

    rows = lax.broadcasted_iota(jnp.int32, (c, c), 0)
    cols = lax.broadcasted_iota(jnp.int32, (c, c), 1)
    tri = (rows >= cols).astype(F32)
    rows2 = lax.broadcasted_iota(jnp.int32, (c, 2 * c), 0)
    cols2 = lax.broadcasted_iota(jnp.int32, (c, 2 * c), 1)
    cols2 = jnp.where(cols2 >= c, cols2 - c, cols2)
    strict2 = rows2 > cols2
    incl2 = rows2 >= cols2
    ones_cols = jnp.ones((c, hd), F32)
    zeros_cv = jnp.zeros((c, hd), F32)

    ar, bk, bk_rem, v_u, decay = [], [], [], [], []
    for bi in range(bb):
        r = r_ref[bi].astype(F32)
        lw = lw_ref[bi]
        k = k_ref[bi].astype(F32)
        a = a_ref[bi].astype(F32)
        b = b_ref[bi].astype(F32)
        v = v_ref[bi].astype(F32)
        cum = _mm(tri, lw, pa=1, pb=3)
        cum_last = cum[c - 1:c, :]
        e_neg = jnp.exp(-cum)
        e_rem = jnp.exp(cum_last - cum)
        at = a * jnp.exp(cum - lw)
        rt = r * jnp.exp(cum)
        bt = b * e_neg
        kt = k * e_neg
        bh = b * e_rem
        kh = k * e_rem
        cum_cols = _mm(lw, ones_cols, dims=TN_DIMS, pa=3, pb=1)
        for h in range(nh):
            sl = slice(h * hd, (h + 1) * hd)
            ar.append(jnp.concatenate([at[:, sl], rt[:, sl]], axis=0))
            bk.append(jnp.concatenate([bt[:, sl], kt[:, sl]], axis=0))
            bk_rem.append(jnp.concatenate([bh[:, sl], kh[:, sl]], axis=0))
            v_u.append(v[:, sl])
            decay.append(jnp.exp(cum_cols[h * hd:(h + 1) * hd, :]))

    st = [st_ref[bi, h] for bi, h in units]
    m4 = _mm_each(ar, bk, dims=NT_DIMS)
    m_a = [jnp.where(strict2, m[0:c], 0.0) for m in m4]
    m_r = [jnp.where(incl2, m[c:2 * c], 0.0) for m in m4]
    t_inv = _unit_lower_inverse_each([m[:, 0:c] for m in m_a])
    g1 = _mm_each([x[0:c] for x in ar], st)
    g2 = _mm_each(m_a, [jnp.concatenate([zeros_cv, vh], axis=0) for vh in v_u])
    u = _mm_each(t_inv, [x + y for x, y in zip(g1, g2)])
    uv = [jnp.concatenate([ui, vh], axis=0) for ui, vh in zip(u, v_u)]
    y1 = _mm_each([x[c:2 * c] for x in ar], st)
    y2 = _mm_each(m_r, uv)
    st_upd = _mm_each(bk_rem, uv, dims=TN_DIMS)
    for (bi, h), s_old, d, upd in zip(units, st, decay, st_upd):
        st_ref[bi, h] = s_old * d + upd

    rk = rk_ref[...]
    for bi in range(bb):
        r = r_ref[bi].astype(F32)
        k = k_ref[bi].astype(F32)
        rkk = r * k * rk
        yn_parts, bonus_parts = [], []
        for h in range(nh):
            i = bi * nh + h
            sl = slice(h * hd, (h + 1) * hd)
            y = y1[i] + y2[i]
            mean = jnp.mean(y, axis=-1, keepdims=True)
            yc = y - mean
            var = jnp.mean(yc * yc, axis=-1, keepdims=True)
            yn_parts.append(yc * lax.rsqrt(var + LNX_EPS))
            bonus_parts.append(jnp.sum(rkk[:, sl], axis=-1, keepdims=True) * v_u[i])
        yn_all = jnp.concatenate(yn_parts, axis=1)
        bonus_all = jnp.concatenate(bonus_parts, axis=1)
        o_ref[bi] = ((yn_all * lnw_ref[...] + lnb_ref[...] + bonus_all) * g_ref[bi].astype(F32)).astype(o_ref.dtype)

    @pl.when(ci == pl.num_programs(1) - 1)
    def _():
        for bi, h in units:
            sT_ref[bi, h] = st_ref[bi, h].T


def _rwkv_scan(r, lw, k, v, a, b, g, s0, r_k, lnx_w, lnx_b):
    bsz, t, db = r.shape
    _, nh, hd, _ = s0.shape
    c = min(t, 64)
    bb = 4 if bsz % 4 == 0 else 1
    seq = pl.BlockSpec((bb, c, db), lambda i, j: (i, j, 0))
    row = pl.BlockSpec((1, db), lambda i, j: (0, 0))
    st_spec = pl.BlockSpec((bb, nh, hd, hd), lambda i, j: (i, 0, 0, 0))
    return pl.pallas_call(
        functools.partial(_scan_body, c=c, bb=bb, nh=nh, hd=hd),
        grid=(bsz // bb, t // c),
        in_specs=[seq] * 7 + [st_spec, row, row, row],
        out_specs=[seq, st_spec],
        out_shape=[jax.ShapeDtypeStruct((bsz, t, db), _act_dtype(c)), jax.ShapeDtypeStruct((bsz, nh, hd, hd), F32)],
        scratch_shapes=[pltpu.VMEM((bb, nh, hd, hd), F32)],
        compiler_params=_params(("parallel", "arbitrary")),
        name="rwkv_scan",
    )(r, lw, k, v, a, b, g, s0, r_k, lnx_w, lnx_b)


def _lam_value(lq1, lk1, lq2, lk2, lam_init):
    s1 = jnp.sum(lq1 * lk1, axis=-1, keepdims=True)
    s2 = jnp.sum(lq2 * lk2, axis=-1, keepdims=True)
    return jnp.exp(s1) - jnp.exp(s2) + lam_init


def _stack_maps(q, tn, dh):
    q2x = jnp.concatenate([q, q], axis=0)
    row = lax.broadcasted_iota(jnp.int32, q2x.shape, 0)
    lane = lax.broadcasted_iota(jnp.int32, q2x.shape, 1)
    keep = (row < tn) == (lane < dh)
    return jnp.where(keep, q2x, 0.0).astype(BF16)


def _lane_tile(x, width):
    reps = width // V7X_LANES
    return x if reps == 1 else jnp.concatenate([x] * reps, axis=1)


def _diff_finish(o12, tn, lam, subln_w, lam_init):
    o = o12[0:tn] - lam * o12[tn:2 * tn]
    return _rms(o, subln_w, SUBLN_EPS) * (1.0 - lam_init)


PATTN_COLS = 128
PATTN_KEYS = 256
PATTN_HEADS = 4
BIAS_PIECES = 3


def _pattn_body(qi_ref, ki_ref, q_ref, k_ref, v_ref, slope_ref, lq1_ref, lk1_ref, lq2_ref, lk2_ref, sw_ref,
                o_ref, q12_s, ktab_s, s_s, p_s, m_s, l_s, al_s, acc_s, *, tq, dh, hps, n_heads, pk, lam_init):
    step = pl.program_id(2)
    qi = qi_ref[step]
    ki = ki_ref[step]
    nq2 = 2 * tq
    hw = 2 * dh
    heads = range(hps)

    @pl.when(ki == 0)
    def _():
        lane = lax.broadcasted_iota(jnp.int32, (nq2, V7X_LANES), 1)
        ones = jnp.where(lane < BIAS_PIECES, 1.0, 0.0).astype(BF16)
        klane = lax.broadcasted_iota(jnp.int32, (tq, V7X_LANES), 1)
        kpos = lax.broadcasted_iota(jnp.int32, (tq, V7X_LANES), 0).astype(F32)
        for g in heads:
            q12_s[g] = jnp.concatenate([_stack_maps(q_ref[0][:, g * hw:(g + 1) * hw], tq, dh), ones], axis=1)
            rem = slope_ref[g][:, 0:1] * kpos
            tab = jnp.zeros((tq, V7X_LANES), F32)
            for i in range(BIAS_PIECES):
                piece = rem.astype(BF16).astype(F32)
                tab = jnp.where(klane == i, piece, tab)
                rem = rem - piece
            ktab_s[g] = tab.astype(BF16)
        m_s[...] = jnp.full(m_s.shape, NEG_BIG, F32)
        l_s[...] = jnp.zeros(l_s.shape, F32)
        acc_s[...] = jnp.zeros(acc_s.shape, F32)

    def update(diagonal):
        k_all = k_ref[0]
        v_all = v_ref[0]
        vt_bf = []
        for g in heads:
            k_aug = jnp.concatenate([k_all[:, g * hw:(g + 1) * hw].astype(BF16), ktab_s[g]], axis=1)
            s_s[g] = lax.dot_general(k_aug, q12_s[g], NT_DIMS, preferred_element_type=F32)
            vt_bf.append(v_all[:, g * hw:(g + 1) * hw].T.astype(BF16))
        base = ((ki - qi) * tq).astype(F32)
        for k0 in range(0, tq, pk):
            for g in heads:
                shift = slope_ref[g][:, 0:1] * base
                for c0 in range(0, nq2, PATTN_COLS):
                    q0 = c0 % tq
                    cols = pl.ds(c0, PATTN_COLS)
                    if diagonal and k0 > q0 + PATTN_COLS - 1:
                        p_s[g, :, cols] = jnp.zeros((pk, PATTN_COLS), BF16)
                        al_s[g, :, cols] = jnp.ones((1, PATTN_COLS), F32)
                        continue
                    sv = s_s[g, pl.ds(k0, pk), cols]
                    if diagonal and k0 + pk - 1 > q0:
                        qpos = q0 + lax.broadcasted_iota(jnp.int32, (pk, PATTN_COLS), 1)
                        kpos = k0 + lax.broadcasted_iota(jnp.int32, (pk, PATTN_COLS), 0)
                        sv = jnp.where(qpos >= kpos, sv, NEG_BIG)
                    m_old = m_s[g, :, cols]
                    m_new = jnp.maximum(m_old, jnp.max(sv, axis=0, keepdims=True) + shift)
                    alpha = jnp.exp2(m_old - m_new)
                    p = jnp.exp2(sv - (m_new - shift))
                    m_s[g, :, cols] = m_new
                    l_s[g, :, cols] = alpha * l_s[g, :, cols] + jnp.sum(p, axis=0, keepdims=True)
                    al_s[g, :, cols] = alpha
                    p_s[g, :, cols] = p.astype(BF16)
            for g in heads:
                acc_s[g] = al_s[g] * acc_s[g] + jnp.dot(vt_bf[g][:, k0:k0 + pk], p_s[g],
                                                        preferred_element_type=F32)

    @pl.when(ki < qi)
    def _():
        update(False)

    @pl.when(ki == qi)
    def _():
        update(True)
        lam = _lam_value(lq1_ref[...], lk1_ref[...], lq2_ref[...], lk2_ref[...], lam_init)
        outs = []
        for g in heads:
            o12 = (acc_s[g] / l_s[g]).T
            outs.append(_diff_finish(o12, tq, lam, sw_ref[...], lam_init))
        o_ref[0] = (outs[0] if hps == 1 else jnp.concatenate(outs, axis=1)).astype(o_ref.dtype)


def _prompt_attention(q, k, v, slopes, lq1, lk1, lq2, lk2, subln_w, lam_init, n_heads):
    b, t, w = q.shape
    hw = w // n_heads
    assert hw == V7X_LANES, "one head's two maps fill exactly one lane tile"
    dh = hw // 2
    tq = min(t, 512)
    nq = t // tq
    hps = math.gcd(n_heads, PATTN_HEADS)
    pk = min(PATTN_KEYS, tq)
    pairs = [(i, j) for i in range(nq) for j in range(i + 1)]
    qi_tbl = jnp.asarray([p[0] for p in pairs], jnp.int32)
    ki_tbl = jnp.asarray([p[1] for p in pairs], jnp.int32)
    qspec = pl.BlockSpec((1, tq, hps * hw), lambda bi, h, s, qt, kt: (bi, qt[s], h))
    kspec = pl.BlockSpec((1, tq, hps * hw), lambda bi, h, s, qt, kt: (bi, kt[s], h))
    small = lambda n: pl.BlockSpec((1, n), lambda bi, h, s, qt, kt: (0, 0))
    row = pltpu.VMEM((hps, 1, 2 * tq), F32)
    grid_spec = pltpu.PrefetchScalarGridSpec(
        num_scalar_prefetch=2,
        grid=(b, n_heads // hps, len(pairs)),
        in_specs=[qspec, kspec, kspec,
                  pl.BlockSpec((hps, 1, V7X_LANES), lambda bi, h, s, qt, kt: (h, 0, 0)),
                  small(dh), small(dh), small(dh), small(dh), small(hw)],
        out_specs=qspec,
        scratch_shapes=[pltpu.VMEM((hps, 2 * tq, 2 * hw), BF16), pltpu.VMEM((hps, tq, V7X_LANES), BF16),
                        pltpu.VMEM((hps, tq, 2 * tq), F32),
                        pltpu.VMEM((hps, pk, 2 * tq), BF16), row, row, row,
                        pltpu.VMEM((hps, hw, 2 * tq), F32)],
    )
    return pl.pallas_call(
        functools.partial(_pattn_body, tq=tq, dh=dh, hps=hps, n_heads=n_heads, pk=pk, lam_init=lam_init),
        grid_spec=grid_spec,
        out_shape=jax.ShapeDtypeStruct((b, t, w), _act_dtype(tq)),
        compiler_params=_params(("parallel", "parallel", "arbitrary")),
        name="prompt_attention",
    )(qi_tbl, ki_tbl, q, k, v, slopes, lq1, lk1, lq2, lk2, subln_w)


SATTN_PAGES = 32


def _sattn_body(pt_ref, q_ref, kn_ref, vn_ref, slope_ref, lq1_ref, lk1_ref, lq2_ref, lk2_ref, sw_ref, *rest,
                ppb, page, n_heads, tn, dh, past_len, lam_init):
    k_refs = rest[:ppb]
    v_refs = rest[ppb:2 * ppb]
    o_ref, m_s, l_s, acc_s = rest[2 * ppb:]
    del pt_ref
    si = pl.program_id(1)
    hw = 2 * dh
    width = ppb * page
    heads = range(n_heads)

    @pl.when(si == 0)
    def _():
        m_s[...] = jnp.full(m_s.shape, NEG_BIG, F32)
        l_s[...] = jnp.zeros(l_s.shape, F32)
        acc_s[...] = jnp.zeros(acc_s.shape, F32)

    q = q_ref[0]
    qm = [q[:, m * dh:(m + 1) * dh].astype(BF16) for m in range(2 * n_heads)]

    def scores(h, key_of_map, dims):
        sa = lax.dot_general(qm[2 * h], key_of_map(2 * h), dims, preferred_element_type=F32)
        sb = lax.dot_general(qm[2 * h + 1], key_of_map(2 * h + 1), dims, preferred_element_type=F32)
        return jnp.concatenate([sa, sb], axis=0)

    kpos = si * width + lax.broadcasted_iota(jnp.int32, (1, width), 1) - past_len
    kpos = kpos.astype(F32)
    s_all = []
    for h in heads:
        parts = [scores(h, lambda m, j=j: k_refs[j][m].astype(BF16), NN_DIMS) for j in range(ppb)]
        s_all.append(jnp.concatenate(parts, axis=1) + slope_ref[h][:, 0:1] * kpos)
    alphas, ps = [], []
    for h in heads:
        m_old = m_s[h]
        m_new = jnp.maximum(m_old, jnp.max(s_all[h], axis=-1, keepdims=True))
        alpha = jnp.exp2(m_old - m_new)
        p = jnp.exp2(s_all[h] - _lane_tile(m_new, width))
        l_s[h] = alpha * l_s[h] + jnp.sum(p, axis=-1, keepdims=True)
        m_s[h] = m_new
        alphas.append(alpha)
        ps.append(p.astype(BF16))
    for h in heads:
        pv = None
        for j in range(ppb):
            t = jnp.dot(ps[h][:, j * page:(j + 1) * page],
                        v_refs[j][pl.ds(h, page, stride=n_heads), :].astype(BF16),
                        preferred_element_type=F32)
            pv = t if pv is None else pv + t
        acc_s[h] = alphas[h] * acc_s[h] + pv

    @pl.when(si == pl.num_programs(1) - 1)
    def _():
        lam = _lam_value(lq1_ref[...], lk1_ref[...], lq2_ref[...], lk2_ref[...], lam_init)
        kn = kn_ref[0]
        vn = vn_ref[0]
        row_n = lax.broadcasted_iota(jnp.int32, (2 * tn, tn), 0)
        col_n = lax.broadcasted_iota(jnp.int32, (2 * tn, tn), 1)
        causal = jnp.where(row_n >= tn, row_n - tn, row_n) >= col_n
        outs = []
        for h in heads:
            s = scores(h, lambda m: kn[:, m * dh:(m + 1) * dh].astype(BF16), NT_DIMS)
            s = jnp.where(causal, s + slope_ref[h][:, 0:1] * col_n.astype(F32), NEG_BIG)
            m_old = m_s[h]
            m_new = jnp.maximum(m_old, jnp.max(s, axis=-1, keepdims=True))
            alpha = jnp.exp2(m_old - m_new)
            p = jnp.exp2(s - m_new[:, 0:tn])
            l_fin = alpha * l_s[h] + jnp.sum(p, axis=-1, keepdims=True)
            acc = alpha * acc_s[h] + jnp.dot(p.astype(BF16), vn[:, h * hw:(h + 1) * hw].astype(BF16),
                                             preferred_element_type=F32)
            outs.append(_diff_finish(acc / l_fin, tn, lam, sw_ref[...], lam_init))
        o_ref[0] = jnp.concatenate(outs, axis=1)


def _sample_attention(q, k_new, v_new, cache_kt, cache_v, page_table, layer, slopes, lq1, lk1, lq2, lk2, subln_w,
                      lam_init):
    b, tn, w = q.shape
    _, _, maps, dh, page = cache_kt.shape
    n_heads = maps // 2
    hw = 2 * dh
    assert hw == V7X_LANES, "softmax statistics are kept lane-replicated at the head width"
    n_pages = page_table.shape[1]
    ppb = math.gcd(n_pages, SATTN_PAGES)
    n_steps = n_pages // ppb
    pt_flat = page_table.reshape(-1)

    def page_index(j, nd):
        return lambda bi, si, pt: (layer, pt[bi * n_pages + si * ppb + j]) + (0,) * nd

    k_specs = [pl.BlockSpec((None, None, maps, dh, page), page_index(j, 3)) for j in range(ppb)]
    v_specs = [pl.BlockSpec((None, None, page * n_heads, hw), page_index(j, 2)) for j in range(ppb)]
    new_spec = pl.BlockSpec((1, tn, w), lambda bi, si, pt: (bi, 0, 0))
    small = lambda n: pl.BlockSpec((1, n), lambda bi, si, pt: (0, 0))
    grid_spec = pltpu.PrefetchScalarGridSpec(
        num_scalar_prefetch=1,
        grid=(b, n_steps),
        in_specs=[new_spec, new_spec, new_spec,
                  pl.BlockSpec((n_heads, 1, V7X_LANES), lambda bi, si, pt: (0, 0, 0)),
                  small(dh), small(dh), small(dh), small(dh), small(hw)] + k_specs + v_specs,
        out_specs=new_spec,
        scratch_shapes=[pltpu.VMEM((n_heads, 2 * tn, hw), F32), pltpu.VMEM((n_heads, 2 * tn, hw), F32),
                        pltpu.VMEM((n_heads, 2 * tn, hw), F32)],
    )
    return pl.pallas_call(
        functools.partial(_sattn_body, ppb=ppb, page=page, n_heads=n_heads, tn=tn, dh=dh,
                          past_len=n_pages * page, lam_init=lam_init),
        grid_spec=grid_spec,
        out_shape=jax.ShapeDtypeStruct((b, tn, w), F32),
        compiler_params=_params(("parallel", "arbitrary")),
        name="sample_attention",
    )(pt_flat, q, k_new, v_new, slopes, lq1, lk1, lq2, lk2, subln_w,
      *([cache_kt] * ppb), *([cache_v] * ppb))


def _merge_body(x_ref, ya_ref, yb_ref, yc_ref, gt_ref, wa_ref, wb_ref, wc_ref, wo_ref, o_ref, *, d):
    gates = gt_ref[...].astype(F32)
    merged = jax.nn.sigmoid(gates[:, 0:d]) * jnp.dot(ya_ref[...].astype(BF16), wa_ref[...], preferred_element_type=F32)
    merged = merged + jax.nn.sigmoid(gates[:, d:2 * d]) * jnp.dot(
        yb_ref[...].astype(BF16), wb_ref[...], preferred_element_type=F32)
    merged = merged + jax.nn.sigmoid(gates[:, 2 * d:3 * d]) * jnp.dot(
        yc_ref[...].astype(BF16), wc_ref[...], preferred_element_type=F32)
    o_ref[...] = x_ref[...] + jnp.dot(merged.astype(BF16), wo_ref[...], preferred_element_type=F32)


def _merge(x, ya, yb, yc, gates, wa, wb, wc, wo, layer):
    m, d = x.shape
    tm = min(m, 512)
    tile = lambda n: pl.BlockSpec((tm, n), lambda i: (i, 0))
    return pl.pallas_call(
        functools.partial(_merge_body, d=d),
        grid=(m // tm,),
        in_specs=[tile(d), tile(ya.shape[1]), tile(yb.shape[1]), tile(yc.shape[1]), tile(3 * d),
                  _layer_spec(wa, layer), _layer_spec(wb, layer), _layer_spec(wc, layer), _layer_spec(wo, layer)],
        out_specs=tile(d),
        out_shape=jax.ShapeDtypeStruct((m, d), F32),
        compiler_params=_params(("parallel",)),
        name="merge",
    )(x, ya, yb, yc, gates, wa, wb, wc, wo)


STACKED_BF16 = ("ffn1_w_gate", "ffn1_w_up", "ffn1_w_down", "ffn2_w_gate", "ffn2_w_up", "ffn2_w_down", "w_in",
                "w_branch_a", "w_branch_b", "w_branch_c", "w_out")


def _prep_layer_weights(l, W, big, dims):
    db, r_w, r_a = dims["db"], dims["r_w"], dims["r_a"]
    row = lambda a: a.reshape(1, -1)

    def ffn(prefix):
        return (row(W[prefix + "_norm"][l]), big[prefix + "_w_gate"], big[prefix + "_w_up"], big[prefix + "_w_down"])

    zeros_w = jnp.zeros((r_a, db), BF16)
    zeros_a = jnp.zeros((r_w, db), BF16)
    head = jnp.arange(db) // dims["hd"]
    return dict(
        ffn1=ffn("ffn1"), ffn2=ffn("ffn2"),
        mix_norm=row(W["mix_norm"][l]), w_in=big["w_in"],
        conv_w=W["conv_w"][l],
        mu=row(W["tm_mu"][l]),
        wpad=jnp.concatenate([W["w_up_w"][l].astype(BF16), zeros_w], axis=0),
        apad=jnp.concatenate([zeros_a, W["a_up_w"][l].astype(BF16)], axis=0),
        w0=row(W["w0"][l]), a0=row(W["a0"][l]), gup=W["g_up_w"][l].astype(BF16),
        k_k=row(W["k_k"][l]), k_a=row(W["k_a"][l]), r_k=row(W["r_k"][l]),
        lnx_w=row(W["lnx_w"][l]), lnx_b=row(W["lnx_b"][l]),
        ones_bd=(head[:, None] == head[None, :]).astype(BF16),
        lq1=row(W["lam_q1"][l]), lk1=row(W["lam_k1"][l]), lq2=row(W["lam_q2"][l]), lk2=row(W["lam_k2"][l]),
        subln_w=row(W["subln_w"][l]),
        wa=big["w_branch_a"], wb=big["w_branch_b"], wc=big["w_branch_c"], wo=big["w_out"],
    )


def _run_trunk(x, layers, final_w, conv0, shift0, wkv0, attend, dims):
    b, t, d = x.shape
    m = b * t
    da, db, ns = dims["da"], dims["db"], dims["ns"]
    qw, kw, dc = dims["qw"], dims["kw"], dims["dc"]
    widths = (3 * da, ns, qw, kw, dc, 3 * d)
    Q_OUT, K_OUT, V_OUT, GATES_OUT = 2, 3, 4, 5
    depth = len(layers)
    xf = x.reshape(m, d)
    k_rows, wkv_out, shift_out, conv_out = [], [], [], []
    stacks = None
    for l, lw in enumerate(layers):
        xf = _ffn(xf, *lw["ffn1"], l)
        cin, z, q, k, v, gates, *stacks = _mixin(xf, lw["mix_norm"], lw["w_in"], l, widths, Q_OUT,
                                                 dims["dh"] ** -0.5 * LOG2_E, K_OUT, V_OUT, dims["hc"], t, stacks,
                                                 bf16_outs=(GATES_OUT,))
        y_a, conv_new = _short_conv(cin.reshape(b, t, 3 * da), conv0[l], lw["conv_w"])
        r_, lw_, k_, v_, a_, b_, g_, shift_new = _rwkv_pre(
            z.reshape(b, t, ns), shift0[l].reshape(b, 1, ns), lw["mu"], lw["wpad"], lw["w0"], lw["apad"], lw["a0"],
            lw["gup"], lw["k_k"], lw["k_a"], lw["ones_bd"], db)
        y_b, wkv_new = _rwkv_scan(r_, lw_, k_, v_, a_, b_, g_, wkv0[l], lw["r_k"], lw["lnx_w"], lw["lnx_b"])
        lam_init = 0.8 - 0.6 * math.exp(-0.3 * l)
        y_c = attend(l, q.reshape(b, t, qw), k.reshape(b, t, kw), v.reshape(b, t, dc), lw, lam_init)
        xf = _merge(xf, y_a.reshape(m, da), y_b.reshape(m, db), y_c.reshape(m, dc), gates,
                    lw["wa"], lw["wb"], lw["wc"], lw["wo"], l)
        xf = _ffn(xf, *lw["ffn2"], l, final_w=final_w if l == depth - 1 else None)
        k_rows.append(k.reshape(b, t, 2 * dims["hc"], dims["dh"]))
        wkv_out.append(wkv_new)
        shift_out.append(shift_new.reshape(b, ns))
        conv_out.append(conv_new)
    v_all = stacks[0].reshape(depth, b, t, dims["hc"], 2 * dims["dh"])
    if len(stacks) > 1:
        k_all = jnp.transpose(stacks[1].reshape(depth, b, 2 * dims["hc"], dims["dh"], t), (0, 1, 4, 2, 3))
    else:
        k_all = jnp.stack(k_rows)
    return (xf.reshape(b, t, d), k_all, v_all, jnp.stack(wkv_out), jnp.stack(shift_out), jnp.stack(conv_out))


def kernel(x_prompt, x_sample, cache_k, cache_v, state_wkv, state_shift, state_conv, page_table, ffn1_norm, ffn1_w_gate, ffn1_w_up, ffn1_w_down, mix_norm, w_in, conv_w, tm_mu, w_up_w, w0, a_up_w, a0, g_up_w, k_k, k_a, r_k, lnx_w, lnx_b, lam_q1, lam_k1, lam_q2, lam_k2, subln_w, w_branch_a, w_branch_b, w_branch_c, w_out, ffn2_norm, ffn2_w_gate, ffn2_w_up, ffn2_w_down, final_norm):
    W = dict(ffn1_norm=ffn1_norm, ffn1_w_gate=ffn1_w_gate, ffn1_w_up=ffn1_w_up, ffn1_w_down=ffn1_w_down,
             mix_norm=mix_norm, w_in=w_in, conv_w=conv_w, tm_mu=tm_mu, w_up_w=w_up_w, w0=w0,
             a_up_w=a_up_w, a0=a0, g_up_w=g_up_w, k_k=k_k, k_a=k_a, r_k=r_k, lnx_w=lnx_w, lnx_b=lnx_b,
             lam_q1=lam_q1, lam_k1=lam_k1, lam_q2=lam_q2, lam_k2=lam_k2, subln_w=subln_w,
             w_branch_a=w_branch_a, w_branch_b=w_branch_b, w_branch_c=w_branch_c, w_out=w_out,
             ffn2_norm=ffn2_norm, ffn2_w_gate=ffn2_w_gate, ffn2_w_up=ffn2_w_up, ffn2_w_down=ffn2_w_down)
    depth = w_in.shape[0]
    d = x_prompt.shape[-1]
    _, n_pool, page, maps, dh = cache_k.shape
    hc = maps // 2
    nh, hd = r_k.shape[1], r_k.shape[2]
    db = nh * hd
    da = conv_w.shape[-1]
    assert conv_w.shape[1] == 3, "short conv kernel is written for width 3"
    r_w, r_a, r_g = w_up_w.shape[1], a_up_w.shape[1], g_up_w.shape[1]
    ns = state_shift.shape[-1]
    assert ns == 3 * db + r_w + r_a + r_g
    assert (r_w + r_a) % V7X_LANES == 0 and r_g % V7X_LANES == 0
    dims = dict(d=d, d_ff=ffn1_w_gate.shape[-1], da=da, db=db, hd=hd, ns=ns, r_w=r_w, r_a=r_a,
                qw=maps * dh, kw=maps * dh, dc=hc * 2 * dh, hc=hc, dh=dh)
    big = {name: W[name].astype(BF16) for name in STACKED_BF16}
    layers = [_prep_layer_weights(l, W, big, dims) for l in range(depth)]
    final_w = final_norm.reshape(1, d)
    slopes = jnp.broadcast_to(
        jnp.asarray([LOG2_E * 2.0 ** (-8.0 * (h + 1) / hc) for h in range(hc)], F32)[:, None, None],
        (hc, 1, V7X_LANES))
    cache_kt = jnp.transpose(cache_k, (0, 1, 3, 4, 2))
    cache_v2 = cache_v.reshape(depth, n_pool, page * hc, 2 * dh)

    def attend_prompt(l, q, k, v, lw, lam_init):
        return _prompt_attention(q, k, v, slopes, lw["lq1"], lw["lk1"], lw["lq2"], lw["lk2"], lw["subln_w"],
                                 lam_init, hc)

    def attend_sample(l, q, k, v, lw, lam_init):
        return _sample_attention(q, k, v, cache_kt, cache_v2, page_table, l, slopes, lw["lq1"], lw["lk1"],
                                 lw["lq2"], lw["lk2"], lw["subln_w"], lam_init)

    bp = x_prompt.shape[0]
    conv0 = jnp.zeros((depth, bp, 2, da), F32)
    shift0 = jnp.zeros((depth, bp, ns), F32)
    wkv0 = jnp.zeros((depth, bp, nh, hd, hd), F32)
    y_p, k_p, v_p, wkv_p, shift_p, conv_p = _run_trunk(
        x_prompt, layers, final_w, conv0, shift0, wkv0, attend_prompt, dims)
    y_s, k_s, v_s, wkv_s, shift_s, conv_s = _run_trunk(
        x_sample, layers, final_w, state_conv, state_shift, state_wkv, attend_sample, dims)
    return (y_p, y_s, k_p, v_p, k_s, v_s, wkv_p, wkv_s, shift_p, shift_s, conv_p, conv_s)
```

```python
import functools
import math

import jax
import jax.numpy as jnp
from jax import lax
from jax.experimental import pallas as pl
from jax.experimental.pallas import tpu as pltpu

F32 = jnp.float32
BF16 = jnp.bfloat16

NORM_EPS = 1e-6
LNX_EPS = 64e-5
SUBLN_EPS = 1e-5
KK_EPS = 1e-24
NEG_BIG = -1e30
LOG2_E = math.log2(math.e)

V7X_LANES = 128
V7X_SUBLANES = 8
V7X_VMEM_LIMIT_BYTES = 56 * 1024 * 1024

NT_DIMS = (((1,), (1,)), ((), ()))
NN_DIMS = (((1,), (0,)), ((), ()))
TN_DIMS = (((0,), (0,)), ((), ()))


def _params(semantics):
    return pltpu.CompilerParams(dimension_semantics=semantics, vmem_limit_bytes=V7X_VMEM_LIMIT_BYTES)


def _const_spec(shape):
    nd = len(shape)
    return pl.BlockSpec(shape, lambda *_: (0,) * nd, pipeline_mode=pl.Buffered(1))


def _layer_spec(w, layer):
    nd = w.ndim - 1
    return pl.BlockSpec((None,) + w.shape[1:], lambda *_: (layer,) + (0,) * nd, pipeline_mode=pl.Buffered(1))


def _act_dtype(rows):
    return BF16 if rows % 16 == 0 else F32


def _rms(x, w, eps):
    ms = jnp.mean(x * x, axis=-1, keepdims=True)
    return x * lax.rsqrt(ms + eps) * w


def _split_bf16(x, n):
    pieces = []
    rem = x
    for i in range(n):
        p = rem.astype(BF16)
        pieces.append(p)
        if i + 1 < n:
            rem = rem - p.astype(F32)
    return pieces


def _mm(a, b, dims=NN_DIMS, pa=1, pb=1):
    a_p = _split_bf16(a, pa)
    b_p = _split_bf16(b, pb)
    order = max(pa, pb)
    acc = None
    for i, ai in enumerate(a_p):
        for j, bj in enumerate(b_p):
            if i + j >= order:
                continue
            t = lax.dot_general(ai, bj, dims, preferred_element_type=F32)
            acc = t if acc is None else acc + t
    return acc


FFN_CHUNK = 256


def _ffn_body(x_ref, nw_ref, wg_ref, wu_ref, wd_ref, *rest, n_chunks, fc, final):
    if final:
        fw_ref, o_ref, acc_ref = rest
    else:
        o_ref, acc_ref = rest
    x = x_ref[...]
    h = _rms(x, nw_ref[...], NORM_EPS).astype(BF16)
    for c in range(n_chunks):
        cs = slice(c * fc, (c + 1) * fc)
        g = jnp.dot(h, wg_ref[:, cs], preferred_element_type=F32)
        u = jnp.dot(h, wu_ref[:, cs], preferred_element_type=F32)
        act = (g * jax.nn.sigmoid(g) * u).astype(BF16)
        d = jnp.dot(act, wd_ref[cs, :], preferred_element_type=F32)
        if c == 0:
            acc_ref[...] = d
        else:
            acc_ref[...] += d
    y = x + 0.5 * acc_ref[...]
    if final:
        y = _rms(y, fw_ref[...], NORM_EPS)
    o_ref[...] = y


def _ffn(x, norm_w, wg, wu, wd, layer, final_w=None):
    m, d = x.shape
    d_ff = wg.shape[2]
    fc = FFN_CHUNK if d_ff % FFN_CHUNK == 0 else V7X_LANES
    n_chunks = d_ff // fc
    tm = min(m, 512)
    final = final_w is not None
    in_specs = [
        pl.BlockSpec((tm, d), lambda i: (i, 0)),
        _const_spec((1, d)),
        _layer_spec(wg, layer),
        _layer_spec(wu, layer),
        _layer_spec(wd, layer),
    ]
    args = [x, norm_w, wg, wu, wd]
    if final:
        in_specs.append(_const_spec((1, d)))
        args.append(final_w)
    return pl.pallas_call(
        functools.partial(_ffn_body, n_chunks=n_chunks, fc=fc, final=final),
        grid=(m // tm,),
        in_specs=in_specs,
        out_specs=pl.BlockSpec((tm, d), lambda i: (i, 0)),
        out_shape=jax.ShapeDtypeStruct((m, d), F32),
        scratch_shapes=[pltpu.VMEM((tm, d), F32)],
        compiler_params=_params(("parallel",)),
        name="ffn",
    )(*args)


def _mixin_body(x_ref, nw_ref, w_ref, *refs, widths, n_carried, q_index, q_scale, k_index, v_index, v_heads,
                col_tile, tm):
    out_refs = refs[n_carried:]
    vil_ref = out_refs[len(widths)]
    kt_ref = out_refs[len(widths) + 1] if len(out_refs) > len(widths) + 1 else None
    h = _rms(x_ref[...], nw_ref[...], NORM_EPS).astype(BF16)
    start = 0
    for idx, (ref, width) in enumerate(zip(out_refs, widths)):
        for c0 in range(0, width, col_tile):
            wd = min(col_tile, width - c0)
            r = jnp.dot(h, w_ref[:, start + c0:start + c0 + wd], preferred_element_type=F32)
            if idx == q_index:
                r = r * q_scale
            ref[:, c0:c0 + wd] = r.astype(ref.dtype)
            if idx == v_index:
                hw = width // v_heads
                for hh in range(c0 // hw, (c0 + wd) // hw):
                    vil_ref[pl.ds(hh, tm, stride=v_heads), :] = r[:, hh * hw - c0:(hh + 1) * hw - c0]
            if idx == k_index and kt_ref is not None:
                kt_ref[c0:c0 + wd, :] = r.T
        start += width


def _mixin(x, norm_w, w_in, layer, widths, q_index, q_scale, k_index, v_index, v_heads, seq_len, carried, bf16_outs):
    m, d = x.shape
    depth = w_in.shape[0]
    tm = min(m, 512)
    hw = widths[v_index] // v_heads
    kw = widths[k_index]
    emit_kt = seq_len % tm == 0
    tiles_per_seq = seq_len // tm if emit_kt else 1
    stack_specs = [pl.BlockSpec((None, tm * v_heads, hw), lambda i: (layer, i, 0))]
    stack_shapes = [jax.ShapeDtypeStruct((depth, m * v_heads, hw), F32)]
    if emit_kt:
        stack_specs.append(pl.BlockSpec((None, None, kw, tm),
                                        lambda i: (layer, i // tiles_per_seq, 0, i % tiles_per_seq)))
        stack_shapes.append(jax.ShapeDtypeStruct((depth, m // seq_len, kw, seq_len), F32))
    carried = list(carried) if carried is not None else []
    n_in = 3
    return pl.pallas_call(
        functools.partial(_mixin_body, widths=widths, n_carried=len(carried), q_index=q_index, q_scale=q_scale,
                          k_index=k_index, v_index=v_index, v_heads=v_heads, col_tile=512, tm=tm),
        grid=(m // tm,),
        in_specs=[pl.BlockSpec((tm, d), lambda i: (i, 0)), _const_spec((1, d)), _layer_spec(w_in, layer)]
                 + [pl.BlockSpec(memory_space=pl.ANY)] * len(carried),
        out_specs=[pl.BlockSpec((tm, w), lambda i: (i, 0)) for w in widths] + stack_specs,
        out_shape=[jax.ShapeDtypeStruct((m, w), _act_dtype(tm) if i in bf16_outs else F32)
                   for i, w in enumerate(widths)] + stack_shapes,
        input_output_aliases={n_in + j: len(widths) + j for j in range(len(carried))},
        compiler_params=_params(("parallel",)),
        name="mixin",
    )(x, norm_w, w_in, *carried)


def _conv_body(cin_ref, st_ref, cw_ref, ya_ref, cn_ref, buf, *, tt, da):
    @pl.when(pl.program_id(1) == 0)
    def _():
        buf[pl.ds(6, 2), :] = st_ref[0]

    c = cin_ref[0].astype(F32)
    xin = c[:, 0:da]
    gb = c[:, da:2 * da]
    gc = c[:, 2 * da:3 * da]
    u = gc * xin
    buf[pl.ds(8, tt), :] = u
    cw = cw_ref[...]
    y = buf[pl.ds(6, tt), :] * cw[0:1, :]
    y = y + buf[pl.ds(7, tt), :] * cw[1:2, :]
    y = y + u * cw[2:3, :]
    ya_ref[0] = (gb * y).astype(ya_ref.dtype)
    tail = buf[pl.ds(tt + 6, 2), :]
    cn_ref[0] = tail
    buf[pl.ds(6, 2), :] = tail


def _short_conv(cin, conv_state, conv_w):
    b, t, w3 = cin.shape
    da = w3 // 3
    tt = min(t, 512)
    return pl.pallas_call(
        functools.partial(_conv_body, tt=tt, da=da),
        grid=(b, t // tt),
        in_specs=[
            pl.BlockSpec((1, tt, w3), lambda i, j: (i, j, 0)),
            pl.BlockSpec((1, 2, da), lambda i, j: (i, 0, 0)),
            pl.BlockSpec((3, da), lambda i, j: (0, 0)),
        ],
        out_specs=[
            pl.BlockSpec((1, tt, da), lambda i, j: (i, j, 0)),
            pl.BlockSpec((1, 2, da), lambda i, j: (i, 0, 0)),
        ],
        out_shape=[jax.ShapeDtypeStruct((b, t, da), _act_dtype(tt)), jax.ShapeDtypeStruct((b, 2, da), F32)],
        scratch_shapes=[pltpu.VMEM((tt + 8, da), F32)],
        compiler_params=_params(("parallel", "arbitrary")),
        name="short_conv",
    )(cin, conv_state, conv_w)


def _pre_body(z_ref, ss_ref, mu_ref, wpad_ref, w0_ref, apad_ref, a0_ref, gup_ref, kk_ref, ka_ref, ones_ref,
              r_ref, lw_ref, k_ref, v_ref, a_ref, b_ref, g_ref, sn_ref, buf, *, tt, db, r_lo, r_g):
    @pl.when(pl.program_id(1) == 0)
    def _():
        buf[pl.ds(7, 1), :] = ss_ref[0]

    z = z_ref[0]
    buf[pl.ds(8, tt), :] = z
    zp = buf[pl.ds(7, tt), :]
    zm = z + (zp - z) * mu_ref[...]
    last = buf[pl.ds(tt + 7, 1), :]
    sn_ref[0] = last
    buf[pl.ds(7, 1), :] = last

    r = zm[:, 0:db]
    k = zm[:, db:2 * db]
    v = zm[:, 2 * db:3 * db]
    lo = zm[:, 3 * db:3 * db + r_lo]
    g_lo = zm[:, 3 * db + r_lo:3 * db + r_lo + r_g]
    w_part = jnp.dot(jnp.tanh(lo).astype(BF16), wpad_ref[...], preferred_element_type=F32)
    a_part = jnp.dot(lo.astype(BF16), apad_ref[...], preferred_element_type=F32)
    lw = -math.exp(-0.5) * jax.nn.sigmoid(w0_ref[...] + w_part)
    a = jax.nn.sigmoid(a0_ref[...] + a_part)
    g = jnp.dot(jax.nn.sigmoid(g_lo).astype(BF16), gup_ref[...], preferred_element_type=F32)
    kk = k * kk_ref[...]
    ss = _mm(kk * kk, ones_ref[...].astype(F32), pa=2, pb=1)
    kkn = kk * lax.rsqrt(jnp.maximum(ss, KK_EPS))
    r_ref[0] = r.astype(r_ref.dtype)
    lw_ref[0] = lw
    k_ref[0] = (k * (1.0 + (a - 1.0) * ka_ref[...])).astype(k_ref.dtype)
    v_ref[0] = v.astype(v_ref.dtype)
    a_ref[0] = (-kkn).astype(a_ref.dtype)
    b_ref[0] = (kkn * a).astype(b_ref.dtype)
    g_ref[0] = g.astype(g_ref.dtype)


def _rwkv_pre(z, shift_state, mu, wpad, w0, apad, a0, gup, k_k, k_a, ones_bd, db):
    b, t, ns = z.shape
    r_lo = wpad.shape[0]
    r_g = gup.shape[0]
    tt = min(t, 512)
    row = lambda n: pl.BlockSpec((1, n), lambda i, j: (0, 0))
    full = lambda s: pl.BlockSpec(s, lambda i, j: (0, 0))
    seq = lambda n: pl.BlockSpec((1, tt, n), lambda i, j: (i, j, 0))
    return pl.pallas_call(
        functools.partial(_pre_body, tt=tt, db=db, r_lo=r_lo, r_g=r_g),
        grid=(b, t // tt),
        in_specs=[
            seq(ns),
            pl.BlockSpec((1, 1, ns), lambda i, j: (i, 0, 0)),
            row(ns), full((r_lo, db)), row(db), full((r_lo, db)), row(db), full((r_g, db)),
            row(db), row(db), full((db, db)),
        ],
        out_specs=[seq(db)] * 7 + [pl.BlockSpec((1, 1, ns), lambda i, j: (i, 0, 0))],
        out_shape=[jax.ShapeDtypeStruct((b, t, db), F32 if i == 1 else _act_dtype(tt)) for i in range(7)]
                  + [jax.ShapeDtypeStruct((b, 1, ns), F32)],
        scratch_shapes=[pltpu.VMEM((tt + 8, ns), F32)],
        compiler_params=_params(("parallel", "arbitrary")),
        name="rwkv_pre",
    )(z, shift_state, mu, wpad, w0, apad, a0, gup, k_k, k_a, ones_bd)


SCAN_SUB = 16
SCAN_P = 1


def _mm_each(a_list, b_list, dims=NN_DIMS):
    return [_mm(a, b, dims=dims, pa=SCAN_P, pb=SCAN_P) for a, b in zip(a_list, b_list)]


def _neumann_inverse_each(l_list, n):
    c = l_list[0].shape[0]
    eye = (lax.broadcasted_iota(jnp.int32, (c, c), 0) == lax.broadcasted_iota(jnp.int32, (c, c), 1)).astype(F32)
    x = [eye + l for l in l_list]
    p = l_list
    span = 2
    while span < n:
        p = _mm_each(p, p)
        xp = _mm_each(x, p)
        x = [xi + d for xi, d in zip(x, xp)]
        span *= 2
    return x


def _unit_lower_inverse_each(l_list):
    c = l_list[0].shape[0]
    if c <= SCAN_SUB:
        return _neumann_inverse_each(l_list, c)
    rows = lax.broadcasted_iota(jnp.int32, (c, c), 0) // SCAN_SUB
    cols = lax.broadcasted_iota(jnp.int32, (c, c), 1) // SCAN_SUB
    same = rows == cols
    t_diag = _neumann_inverse_each([jnp.where(same, l, 0.0) for l in l_list], SCAN_SUB)
    z = _mm_each(t_diag, [jnp.where(same, 0.0, l) for l in l_list])
    nblk = c // SCAN_SUB
    factors = []
    span = 1
    while span < nblk:
        factors.append(z)
        span *= 2
        if span < nblk:
            z = _mm_each(z, z)
    out = t_diag
    for f in reversed(factors):
        d = _mm_each(f, out)
        out = [o + di for o, di in zip(out, d)]
    return out


def _scan_body(r_ref, lw_ref, k_ref, v_ref, a_ref, b_ref, g_ref, s0_ref, rk_ref, lnw_ref, lnb_ref,
               o_ref, sT_ref, st_ref, *, c, bb, nh, hd):
    ci = pl.program_id(1)
    units = [(bi, h) for bi in range(bb) for h in range(nh)]

    @pl.when(ci == 0)
    def _():
        for bi, h in units:
            st_ref[bi, h] = s0_ref[bi, h].T

    rows = lax.broadcasted_iota(jnp.int32, (c, c), 0)
    cols = lax.broadcasted_iota(jnp.int32, (c, c), 1)
    tri = (rows >= cols).astype(F32)
    rows2 = lax.broadcasted_iota(jnp.int32, (c, 2 * c), 0)
    cols2 = lax.broadcasted_iota(jnp.int32, (c, 2 * c), 1)
    cols2 = jnp.where(cols2 >= c, cols2 - c, cols2)
    strict2 = rows2 > cols2
    incl2 = rows2 >= cols2
    ones_cols = jnp.ones((c, hd), F32)
    zeros_cv = jnp.zeros((c, hd), F32)

    ar, bk, bk_rem, v_u, decay = [], [], [], [], []
    for bi in range(bb):
        r = r_ref[bi].astype(F32)
        lw = lw_ref[bi]
        k = k_ref[bi].astype(F32)
        a = a_ref[bi].astype(F32)
        b = b_ref[bi].astype(F32)
        v = v_ref[bi].astype(F32)
        cum = _mm(tri, lw, pa=1, pb=3)
        cum_last = cum[c - 1:c, :]
        e_neg = jnp.exp(-cum)
        e_rem = jnp.exp(cum_last - cum)
        at = a * jnp.exp(cum - lw)
        rt = r * jnp.exp(cum)
        bt = b * e_neg
        kt = k * e_neg
        bh = b * e_rem
        kh = k * e_rem
        cum_cols = _mm(lw, ones_cols, dims=TN_DIMS, pa=3, pb=1)
        for h in range(nh):
            sl = slice(h * hd, (h + 1) * hd)
            ar.append(jnp.concatenate([at[:, sl], rt[:, sl]], axis=0))
            bk.append(jnp.concatenate([bt[:, sl], kt[:, sl]], axis=0))
            bk_rem.append(jnp.concatenate([bh[:, sl], kh[:, sl]], axis=0))
            v_u.append(v[:, sl])
            decay.append(jnp.exp(cum_cols[h * hd:(h + 1) * hd, :]))

    st = [st_ref[bi, h] for bi, h in units]
    m4 = _mm_each(ar, bk, dims=NT_DIMS)
    m_a = [jnp.where(strict2, m[0:c], 0.0) for m in m4]
    m_r = [jnp.where(incl2, m[c:2 * c], 0.0) for m in m4]
    t_inv = _unit_lower_inverse_each([m[:, 0:c] for m in m_a])
    g1 = _mm_each([x[0:c] for x in ar], st)
    g2 = _mm_each(m_a, [jnp.concatenate([zeros_cv, vh], axis=0) for vh in v_u])
    u = _mm_each(t_inv, [x + y for x, y in zip(g1, g2)])
    uv = [jnp.concatenate([ui, vh], axis=0) for ui, vh in zip(u, v_u)]
    y1 = _mm_each([x[c:2 * c] for x in ar], st)
    y2 = _mm_each(m_r, uv)
    st_upd = _mm_each(bk_rem, uv, dims=TN_DIMS)
    for (bi, h), s_old, d, upd in zip(units, st, decay, st_upd):
        st_ref[bi, h] = s_old * d + upd

    rk = rk_ref[...]
    for bi in range(bb):
        r = r_ref[bi].astype(F32)
        k = k_ref[bi].astype(F32)
        rkk = r * k * rk
        yn_parts, bonus_parts = [], []
        for h in range(nh):
            i = bi * nh + h
            sl = slice(h * hd, (h + 1) * hd)
            y = y1[i] + y2[i]
            mean = jnp.mean(y, axis=-1, keepdims=True)
            yc = y - mean
            var = jnp.mean(yc * yc, axis=-1, keepdims=True)
            yn_parts.append(yc * lax.rsqrt(var + LNX_EPS))
            bonus_parts.append(jnp.sum(rkk[:, sl], axis=-1, keepdims=True) * v_u[i])
        yn_all = jnp.concatenate(yn_parts, axis=1)
        bonus_all = jnp.concatenate(bonus_parts, axis=1)
        o_ref[bi] = ((yn_all * lnw_ref[...] + lnb_ref[...] + bonus_all) * g_ref[bi].astype(F32)).astype(o_ref.dtype)

    @pl.when(ci == pl.num_programs(1) - 1)
    def _():
        for bi, h in units:
            sT_ref[bi, h] = st_ref[bi, h].T


def _rwkv_scan(r, lw, k, v, a, b, g, s0, r_k, lnx_w, lnx_b):
    bsz, t, db = r.shape
    _, nh, hd, _ = s0.shape
    c = min(t, 64)
    bb = 4 if bsz % 4 == 0 else 1
    seq = pl.BlockSpec((bb, c, db), lambda i, j: (i, j, 0))
    row = pl.BlockSpec((1, db), lambda i, j: (0, 0))
    st_spec = pl.BlockSpec((bb, nh, hd, hd), lambda i, j: (i, 0, 0, 0))
    return pl.pallas_call(
        functools.partial(_scan_body, c=c, bb=bb, nh=nh, hd=hd),
        grid=(bsz // bb, t // c),
        in_specs=[seq] * 7 + [st_spec, row, row, row],
        out_specs=[seq, st_spec],
        out_shape=[jax.ShapeDtypeStruct((bsz, t, db), _act_dtype(c)), jax.ShapeDtypeStruct((bsz, nh, hd, hd), F32)],
        scratch_shapes=[pltpu.VMEM((bb, nh, hd, hd), F32)],
        compiler_params=_params(("parallel", "arbitrary")),
        name="rwkv_scan",
    )(r, lw, k, v, a, b, g, s0, r_k, lnx_w, lnx_b)


def _lam_value(lq1, lk1, lq2, lk2, lam_init):
    s1 = jnp.sum(lq1 * lk1, axis=-1, keepdims=True)
    s2 = jnp.sum(lq2 * lk2, axis=-1, keepdims=True)
    return jnp.exp(s1) - jnp.exp(s2) + lam_init


def _stack_maps(q, tn, dh):
    q2x = jnp.concatenate([q, q], axis=0)
    row = lax.broadcasted_iota(jnp.int32, q2x.shape, 0)
    lane = lax.broadcasted_iota(jnp.int32, q2x.shape, 1)
    keep = (row < tn) == (lane < dh)
    return jnp.where(keep, q2x, 0.0).astype(BF16)


def _lane_tile(x, width):
    reps = width // V7X_LANES
    return x if reps == 1 else jnp.concatenate([x] * reps, axis=1)


def _diff_finish(o12, tn, lam, subln_w, lam_init):
    o = o12[0:tn] - lam * o12[tn:2 * tn]
    return _rms(o, subln_w, SUBLN_EPS) * (1.0 - lam_init)


PATTN_COLS = 128
PATTN_KEYS = 256
PATTN_HEADS = 4
BIAS_PIECES = 3


def _pattn_body(qi_ref, ki_ref, q_ref, k_ref, v_ref, slope_ref, lq1_ref, lk1_ref, lq2_ref, lk2_ref, sw_ref,
                o_ref, q12_s, ktab_s, s_s, p_s, m_s, l_s, al_s, acc_s, *, tq, dh, hps, n_heads, pk, lam_init):
    step = pl.program_id(2)
    qi = qi_ref[step]
    ki = ki_ref[step]
    nq2 = 2 * tq
    hw = 2 * dh
    heads = range(hps)

    @pl.when(ki == 0)
    def _():
        lane = lax.broadcasted_iota(jnp.int32, (nq2, V7X_LANES), 1)
        ones = jnp.where(lane < BIAS_PIECES, 1.0, 0.0).astype(BF16)
        klane = lax.broadcasted_iota(jnp.int32, (tq, V7X_LANES), 1)
        kpos = lax.broadcasted_iota(jnp.int32, (tq, V7X_LANES), 0).astype(F32)
        for g in heads:
            q12_s[g] = jnp.concatenate([_stack_maps(q_ref[0][:, g * hw:(g + 1) * hw], tq, dh), ones], axis=1)
            rem = slope_ref[g][:, 0:1] * kpos
            tab = jnp.zeros((tq, V7X_LANES), F32)
            for i in range(BIAS_PIECES):
                piece = rem.astype(BF16).astype(F32)
                tab = jnp.where(klane == i, piece, tab)
                rem = rem - piece
            ktab_s[g] = tab.astype(BF16)
        m_s[...] = jnp.full(m_s.shape, NEG_BIG, F32)
        l_s[...] = jnp.zeros(l_s.shape, F32)
        acc_s[...] = jnp.zeros(acc_s.shape, F32)

    def update(diagonal):
        k_all = k_ref[0]
        v_all = v_ref[0]
        vt_bf = []
        for g in heads:
            k_aug = jnp.concatenate([k_all[:, g * hw:(g + 1) * hw].astype(BF16), ktab_s[g]], axis=1)
            s_s[g] = lax.dot_general(k_aug, q12_s[g], NT_DIMS, preferred_element_type=F32)
            vt_bf.append(v_all[:, g * hw:(g + 1) * hw].T.astype(BF16))
        base = ((ki - qi) * tq).astype(F32)
        for k0 in range(0, tq, pk):
            for g in heads:
                shift = slope_ref[g][:, 0:1] * base
                for c0 in range(0, nq2, PATTN_COLS):
                    q0 = c0 % tq
                    cols = pl.ds(c0, PATTN_COLS)
                    if diagonal and k0 > q0 + PATTN_COLS - 1:
                        p_s[g, :, cols] = jnp.zeros((pk, PATTN_COLS), BF16)
                        al_s[g, :, cols] = jnp.ones((1, PATTN_COLS), F32)
                        continue
                    sv = s_s[g, pl.ds(k0, pk), cols]
                    if diagonal and k0 + pk - 1 > q0:
                        qpos = q0 + lax.broadcasted_iota(jnp.int32, (pk, PATTN_COLS), 1)
                        kpos = k0 + lax.broadcasted_iota(jnp.int32, (pk, PATTN_COLS), 0)
                        sv = jnp.where(qpos >= kpos, sv, NEG_BIG)
                    m_old = m_s[g, :, cols]
                    m_new = jnp.maximum(m_old, jnp.max(sv, axis=0, keepdims=True) + shift)
                    alpha = jnp.exp2(m_old - m_new)
                    p = jnp.exp2(sv - (m_new - shift))
                    m_s[g, :, cols] = m_new
                    l_s[g, :, cols] = alpha * l_s[g, :, cols] + jnp.sum(p, axis=0, keepdims=True)
                    al_s[g, :, cols] = alpha
                    p_s[g, :, cols] = p.astype(BF16)
            for g in heads:
                acc_s[g] = al_s[g] * acc_s[g] + jnp.dot(vt_bf[g][:, k0:k0 + pk], p_s[g],
                                                        preferred_element_type=F32)

    @pl.when(ki < qi)
    def _():
        update(False)

    @pl.when(ki == qi)
    def _():
        update(True)
        lam = _lam_value(lq1_ref[...], lk1_ref[...], lq2_ref[...], lk2_ref[...], lam_init)
        outs = []
        for g in heads:
            o12 = (acc_s[g] / l_s[g]).T
            outs.append(_diff_finish(o12, tq, lam, sw_ref[...], lam_init))
        o_ref[0] = (outs[0] if hps == 1 else jnp.concatenate(outs, axis=1)).astype(o_ref.dtype)


def _prompt_attention(q, k, v, slopes, lq1, lk1, lq2, lk2, subln_w, lam_init, n_heads):
    b, t, w = q.shape
    hw = w // n_heads
    assert hw == V7X_LANES, "one head's two maps fill exactly one lane tile"
    dh = hw // 2
    tq = min(t, 512)
    nq = t // tq
    hps = math.gcd(n_heads, PATTN_HEADS)
    pk = min(PATTN_KEYS, tq)
    pairs = [(i, j) for i in range(nq) for j in range(i + 1)]
    qi_tbl = jnp.asarray([p[0] for p in pairs], jnp.int32)
    ki_tbl = jnp.asarray([p[1] for p in pairs], jnp.int32)
    qspec = pl.BlockSpec((1, tq, hps * hw), lambda bi, h, s, qt, kt: (bi, qt[s], h))
    kspec = pl.BlockSpec((1, tq, hps * hw), lambda bi, h, s, qt, kt: (bi, kt[s], h))
    small = lambda n: pl.BlockSpec((1, n), lambda bi, h, s, qt, kt: (0, 0))
    row = pltpu.VMEM((hps, 1, 2 * tq), F32)
    grid_spec = pltpu.PrefetchScalarGridSpec(
        num_scalar_prefetch=2,
        grid=(b, n_heads // hps, len(pairs)),
        in_specs=[qspec, kspec, kspec,
                  pl.BlockSpec((hps, 1, V7X_LANES), lambda bi, h, s, qt, kt: (h, 0, 0)),
                  small(dh), small(dh), small(dh), small(dh), small(hw)],
        out_specs=qspec,
        scratch_shapes=[pltpu.VMEM((hps, 2 * tq, 2 * hw), BF16), pltpu.VMEM((hps, tq, V7X_LANES), BF16),
                        pltpu.VMEM((hps, tq, 2 * tq), F32),
                        pltpu.VMEM((hps, pk, 2 * tq), BF16), row, row, row,
                        pltpu.VMEM((hps, hw, 2 * tq), F32)],
    )
    return pl.pallas_call(
        functools.partial(_pattn_body, tq=tq, dh=dh, hps=hps, n_heads=n_heads, pk=pk, lam_init=lam_init),
        grid_spec=grid_spec,
        out_shape=jax.ShapeDtypeStruct((b, t, w), _act_dtype(tq)),
        compiler_params=_params(("parallel", "parallel", "arbitrary")),
        name="prompt_attention",
    )(qi_tbl, ki_tbl, q, k, v, slopes, lq1, lk1, lq2, lk2, subln_w)


SATTN_PAGES = 32


def _sattn_body(pt_ref, q_ref, kn_ref, vn_ref, slope_ref, lq1_ref, lk1_ref, lq2_ref, lk2_ref, sw_ref, *rest,
                ppb, page, n_heads, tn, dh, past_len, lam_init):
    k_refs = rest[:ppb]
    v_refs = rest[ppb:2 * ppb]
    o_ref, m_s, l_s, acc_s = rest[2 * ppb:]
    del pt_ref
    si = pl.program_id(1)
    hw = 2 * dh
    width = ppb * page
    heads = range(n_heads)

    @pl.when(si == 0)
    def _():
        m_s[...] = jnp.full(m_s.shape, NEG_BIG, F32)
        l_s[...] = jnp.zeros(l_s.shape, F32)
        acc_s[...] = jnp.zeros(acc_s.shape, F32)

    q = q_ref[0]
    qm = [q[:, m * dh:(m + 1) * dh].astype(BF16) for m in range(2 * n_heads)]

    def scores(h, key_of_map, dims):
        sa = lax.dot_general(qm[2 * h], key_of_map(2 * h), dims, preferred_element_type=F32)
        sb = lax.dot_general(qm[2 * h + 1], key_of_map(2 * h + 1), dims, preferred_element_type=F32)
        return jnp.concatenate([sa, sb], axis=0)

    kpos = si * width + lax.broadcasted_iota(jnp.int32, (1, width), 1) - past_len
    kpos = kpos.astype(F32)
    s_all = []
    for h in heads:
        parts = [scores(h, lambda m, j=j: k_refs[j][m].astype(BF16), NN_DIMS) for j in range(ppb)]
        s_all.append(jnp.concatenate(parts, axis=1) + slope_ref[h][:, 0:1] * kpos)
    alphas, ps = [], []
    for h in heads:
        m_old = m_s[h]
        m_new = jnp.maximum(m_old, jnp.max(s_all[h], axis=-1, keepdims=True))
        alpha = jnp.exp2(m_old - m_new)
        p = jnp.exp2(s_all[h] - _lane_tile(m_new, width))
        l_s[h] = alpha * l_s[h] + jnp.sum(p, axis=-1, keepdims=True)
        m_s[h] = m_new
        alphas.append(alpha)
        ps.append(p.astype(BF16))
    for h in heads:
        pv = None
        for j in range(ppb):
            t = jnp.dot(ps[h][:, j * page:(j + 1) * page],
                        v_refs[j][pl.ds(h, page, stride=n_heads), :].astype(BF16),
                        preferred_element_type=F32)
            pv = t if pv is None else pv + t
        acc_s[h] = alphas[h] * acc_s[h] + pv

    @pl.when(si == pl.num_programs(1) - 1)
    def _():
        lam = _lam_value(lq1_ref[...], lk1_ref[...], lq2_ref[...], lk2_ref[...], lam_init)
        kn = kn_ref[0]
        vn = vn_ref[0]
        row_n = lax.broadcasted_iota(jnp.int32, (2 * tn, tn), 0)
        col_n = lax.broadcasted_iota(jnp.int32, (2 * tn, tn), 1)
        causal = jnp.where(row_n >= tn, row_n - tn, row_n) >= col_n
        outs = []
        for h in heads:
            s = scores(h, lambda m: kn[:, m * dh:(m + 1) * dh].astype(BF16), NT_DIMS)
            s = jnp.where(causal, s + slope_ref[h][:, 0:1] * col_n.astype(F32), NEG_BIG)
            m_old = m_s[h]
            m_new = jnp.maximum(m_old, jnp.max(s, axis=-1, keepdims=True))
            alpha = jnp.exp2(m_old - m_new)
            p = jnp.exp2(s - m_new[:, 0:tn])
            l_fin = alpha * l_s[h] + jnp.sum(p, axis=-1, keepdims=True)
            acc = alpha * acc_s[h] + jnp.dot(p.astype(BF16), vn[:, h * hw:(h + 1) * hw].astype(BF16),
                                             preferred_element_type=F32)
            outs.append(_diff_finish(acc / l_fin, tn, lam, sw_ref[...], lam_init))
        o_ref[0] = jnp.concatenate(outs, axis=1)


def _sample_attention(q, k_new, v_new, cache_kt, cache_v, page_table, layer, slopes, lq1, lk1, lq2, lk2, subln_w,
                      lam_init):
    b, tn, w = q.shape
    _, _, maps, dh, page = cache_kt.shape
    n_heads = maps // 2
    hw = 2 * dh
    assert hw == V7X_LANES, "softmax statistics are kept lane-replicated at the head width"
    n_pages = page_table.shape[1]
    ppb = math.gcd(n_pages, SATTN_PAGES)
    n_steps = n_pages // ppb
    pt_flat = page_table.reshape(-1)

    def page_index(j, nd):
        return lambda bi, si, pt: (layer, pt[bi * n_pages + si * ppb + j]) + (0,) * nd

    k_specs = [pl.BlockSpec((None, None, maps, dh, page), page_index(j, 3)) for j in range(ppb)]
    v_specs = [pl.BlockSpec((None, None, page * n_heads, hw), page_index(j, 2)) for j in range(ppb)]
    new_spec = pl.BlockSpec((1, tn, w), lambda bi, si, pt: (bi, 0, 0))
    small = lambda n: pl.BlockSpec((1, n), lambda bi, si, pt: (0, 0))
    grid_spec = pltpu.PrefetchScalarGridSpec(
        num_scalar_prefetch=1,
        grid=(b, n_steps),
        in_specs=[new_spec, new_spec, new_spec,
                  pl.BlockSpec((n_heads, 1, V7X_LANES), lambda bi, si, pt: (0, 0, 0)),
                  small(dh), small(dh), small(dh), small(dh), small(hw)] + k_specs + v_specs,
        out_specs=new_spec,
        scratch_shapes=[pltpu.VMEM((n_heads, 2 * tn, hw), F32), pltpu.VMEM((n_heads, 2 * tn, hw), F32),
                        pltpu.VMEM((n_heads, 2 * tn, hw), F32)],
    )
    return pl.pallas_call(
        functools.partial(_sattn_body, ppb=ppb, page=page, n_heads=n_heads, tn=tn, dh=dh,
                          past_len=n_pages * page, lam_init=lam_init),
        grid_spec=grid_spec,
        out_shape=jax.ShapeDtypeStruct((b, tn, w), F32),
        compiler_params=_params(("parallel", "arbitrary")),
        name="sample_attention",
    )(pt_flat, q, k_new, v_new, slopes, lq1, lk1, lq2, lk2, subln_w,
      *([cache_kt] * ppb), *([cache_v] * ppb))


def _merge_body(x_ref, ya_ref, yb_ref, yc_ref, gt_ref, wa_ref, wb_ref, wc_ref, wo_ref, o_ref, *, d):
    gates = gt_ref[...].astype(F32)
    merged = jax.nn.sigmoid(gates[:, 0:d]) * jnp.dot(ya_ref[...].astype(BF16), wa_ref[...], preferred_element_type=F32)
    merged = merged + jax.nn.sigmoid(gates[:, d:2 * d]) * jnp.dot(
        yb_ref[...].astype(BF16), wb_ref[...], preferred_element_type=F32)
    merged = merged + jax.nn.sigmoid(gates[:, 2 * d:3 * d]) * jnp.dot(
        yc_ref[...].astype(BF16), wc_ref[...], preferred_element_type=F32)
    o_ref[...] = x_ref[...] + jnp.dot(merged.astype(BF16), wo_ref[...], preferred_element_type=F32)


def _merge(x, ya, yb, yc, gates, wa, wb, wc, wo, layer):
    m, d = x.shape
    tm = min(m, 512)
    tile = lambda n: pl.BlockSpec((tm, n), lambda i: (i, 0))
    return pl.pallas_call(
        functools.partial(_merge_body, d=d),
        grid=(m // tm,),
        in_specs=[tile(d), tile(ya.shape[1]), tile(yb.shape[1]), tile(yc.shape[1]), tile(3 * d),
                  _layer_spec(wa, layer), _layer_spec(wb, layer), _layer_spec(wc, layer), _layer_spec(wo, layer)],
        out_specs=tile(d),
        out_shape=jax.ShapeDtypeStruct((m, d), F32),
        compiler_params=_params(("parallel",)),
        name="merge",
    )(x, ya, yb, yc, gates, wa, wb, wc, wo)


STACKED_BF16 = ("ffn1_w_gate", "ffn1_w_up", "ffn1_w_down", "ffn2_w_gate", "ffn2_w_up", "ffn2_w_down", "w_in",
                "w_branch_a", "w_branch_b", "w_branch_c", "w_out")


def _prep_layer_weights(l, W, big, dims):
    db, r_w, r_a = dims["db"], dims["r_w"], dims["r_a"]
    row = lambda a: a.reshape(1, -1)

    def ffn(prefix):
        return (row(W[prefix + "_norm"][l]), big[prefix + "_w_gate"], big[prefix + "_w_up"], big[prefix + "_w_down"])

    zeros_w = jnp.zeros((r_a, db), BF16)
    zeros_a = jnp.zeros((r_w, db), BF16)
    head = jnp.arange(db) // dims["hd"]
    return dict(
        ffn1=ffn("ffn1"), ffn2=ffn("ffn2"),
        mix_norm=row(W["mix_norm"][l]), w_in=big["w_in"],
        conv_w=W["conv_w"][l],
        mu=row(W["tm_mu"][l]),
        wpad=jnp.concatenate([W["w_up_w"][l].astype(BF16), zeros_w], axis=0),
        apad=jnp.concatenate([zeros_a, W["a_up_w"][l].astype(BF16)], axis=0),
        w0=row(W["w0"][l]), a0=row(W["a0"][l]), gup=W["g_up_w"][l].astype(BF16),
        k_k=row(W["k_k"][l]), k_a=row(W["k_a"][l]), r_k=row(W["r_k"][l]),
        lnx_w=row(W["lnx_w"][l]), lnx_b=row(W["lnx_b"][l]),
        ones_bd=(head[:, None] == head[None, :]).astype(BF16),
        lq1=row(W["lam_q1"][l]), lk1=row(W["lam_k1"][l]), lq2=row(W["lam_q2"][l]), lk2=row(W["lam_k2"][l]),
        subln_w=row(W["subln_w"][l]),
        wa=big["w_branch_a"], wb=big["w_branch_b"], wc=big["w_branch_c"], wo=big["w_out"],
    )


def _run_trunk(x, layers, final_w, conv0, shift0, wkv0, attend, dims):
    b, t, d = x.shape
    m = b * t
    da, db, ns = dims["da"], dims["db"], dims["ns"]
    qw, kw, dc = dims["qw"], dims["kw"], dims["dc"]
    widths = (3 * da, ns, qw, kw, dc, 3 * d)
    CONV_OUT, Q_OUT, K_OUT, V_OUT, GATES_OUT = 0, 2, 3, 4, 5
    depth = len(layers)
    xf = x.reshape(m, d)
    k_rows, wkv_out, shift_out, conv_out = [], [], [], []
    stacks = None
    for l, lw in enumerate(layers):
        xf = _ffn(xf, *lw["ffn1"], l)
        cin, z, q, k, v, gates, *stacks = _mixin(xf, lw["mix_norm"], lw["w_in"], l, widths, Q_OUT,
                                                 dims["dh"] ** -0.5 * LOG2_E, K_OUT, V_OUT, dims["hc"], t, stacks,
                                                 bf16_outs=(CONV_OUT, GATES_OUT) if t % 16 == 0 else (GATES_OUT,))
        y_a, conv_new = _short_conv(cin.reshape(b, t, 3 * da), conv0[l], lw["conv_w"])
        r_, lw_, k_, v_, a_, b_, g_, shift_new = _rwkv_pre(
            z.reshape(b, t, ns), shift0[l].reshape(b, 1, ns), lw["mu"], lw["wpad"], lw["w0"], lw["apad"], lw["a0"],
            lw["gup"], lw["k_k"], lw["k_a"], lw["ones_bd"], db)
        y_b, wkv_new = _rwkv_scan(r_, lw_, k_, v_, a_, b_, g_, wkv0[l], lw["r_k"], lw["lnx_w"], lw["lnx_b"])
        lam_init = 0.8 - 0.6 * math.exp(-0.3 * l)
        y_c = attend(l, q.reshape(b, t, qw), k.reshape(b, t, kw), v.reshape(b, t, dc), lw, lam_init)
        xf = _merge(xf, y_a.reshape(m, da), y_b.reshape(m, db), y_c.reshape(m, dc), gates,
                    lw["wa"], lw["wb"], lw["wc"], lw["wo"], l)
        xf = _ffn(xf, *lw["ffn2"], l, final_w=final_w if l == depth - 1 else None)
        k_rows.append(k.reshape(b, t, 2 * dims["hc"], dims["dh"]))
        wkv_out.append(wkv_new)
        shift_out.append(shift_new.reshape(b, ns))
        conv_out.append(conv_new)
    v_all = stacks[0].reshape(depth, b, t, dims["hc"], 2 * dims["dh"])
    if len(stacks) > 1:
        k_all = jnp.transpose(stacks[1].reshape(depth, b, 2 * dims["hc"], dims["dh"], t), (0, 1, 4, 2, 3))
    else:
        k_all = jnp.stack(k_rows)
    return (xf.reshape(b, t, d), k_all, v_all, jnp.stack(wkv_out), jnp.stack(shift_out), jnp.stack(conv_out))


def kernel(x_prompt, x_sample, cache_k, cache_v, state_wkv, state_shift, state_conv, page_table, ffn1_norm, ffn1_w_gate, ffn1_w_up, ffn1_w_down, mix_norm, w_in, conv_w, tm_mu, w_up_w, w0, a_up_w, a0, g_up_w, k_k, k_a, r_k, lnx_w, lnx_b, lam_q1, lam_k1, lam_q2, lam_k2, subln_w, w_branch_a, w_branch_b, w_branch_c, w_out, ffn2_norm, ffn2_w_gate, ffn2_w_up, ffn2_w_down, final_norm):
    W = dict(ffn1_norm=ffn1_norm, ffn1_w_gate=ffn1_w_gate, ffn1_w_up=ffn1_w_up, ffn1_w_down=ffn1_w_down,
             mix_norm=mix_norm, w_in=w_in, conv_w=conv_w, tm_mu=tm_mu, w_up_w=w_up_w, w0=w0,
             a_up_w=a_up_w, a0=a0, g_up_w=g_up_w, k_k=k_k, k_a=k_a, r_k=r_k, lnx_w=lnx_w, lnx_b=lnx_b,
             lam_q1=lam_q1, lam_k1=lam_k1, lam_q2=lam_q2, lam_k2=lam_k2, subln_w=subln_w,
             w_branch_a=w_branch_a, w_branch_b=w_branch_b, w_branch_c=w_branch_c, w_out=w_out,
             ffn2_norm=ffn2_norm, ffn2_w_gate=ffn2_w_gate, ffn2_w_up=ffn2_w_up, ffn2_w_down=ffn2_w_down)
    depth = w_in.shape[0]
    d = x_prompt.shape[-1]
    _, n_pool, page, maps, dh = cache_k.shape
    hc = maps // 2
    nh, hd = r_k.shape[1], r_k.shape[2]
    db = nh * hd
    da = conv_w.shape[-1]
    assert conv_w.shape[1] == 3, "short conv kernel is written for width 3"
    r_w, r_a, r_g = w_up_w.shape[1], a_up_w.shape[1], g_up_w.shape[1]
    ns = state_shift.shape[-1]
    assert ns == 3 * db + r_w + r_a + r_g
    assert (r_w + r_a) % V7X_LANES == 0 and r_g % V7X_LANES == 0
    dims = dict(d=d, d_ff=ffn1_w_gate.shape[-1], da=da, db=db, hd=hd, ns=ns, r_w=r_w, r_a=r_a,
                qw=maps * dh, kw=maps * dh, dc=hc * 2 * dh, hc=hc, dh=dh)
    big = {name: W[name].astype(BF16) for name in STACKED_BF16}
    layers = [_prep_layer_weights(l, W, big, dims) for l in range(depth)]
    final_w = final_norm.reshape(1, d)
    slopes = jnp.broadcast_to(
        jnp.asarray([LOG2_E * 2.0 ** (-8.0 * (h + 1) / hc) for h in range(hc)], F32)[:, None, None],
        (hc, 1, V7X_LANES))
    cache_kt = jnp.transpose(cache_k, (0, 1, 3, 4, 2))
    cache_v2 = cache_v.reshape(depth, n_pool, page * hc, 2 * dh)

    def attend_prompt(l, q, k, v, lw, lam_init):
        return _prompt_attention(q, k, v, slopes, lw["lq1"], lw["lk1"], lw["lq2"], lw["lk2"], lw["subln_w"],
                                 lam_init, hc)

    def attend_sample(l, q, k, v, lw, lam_init):
        return _sample_attention(q, k, v, cache_kt, cache_v2, page_table, l, slopes, lw["lq1"], lw["lk1"],
                                 lw["lq2"], lw["lk2"], lw["subln_w"], lam_init)

    bp = x_prompt.shape[0]
    conv0 = jnp.zeros((depth, bp, 2, da), F32)
    shift0 = jnp.zeros((depth, bp, ns), F32)
    wkv0 = jnp.zeros((depth, bp, nh, hd, hd), F32)
    y_p, k_p, v_p, wkv_p, shift_p, conv_p = _run_trunk(
        x_prompt, layers, final_w, conv0, shift0, wkv0, attend_prompt, dims)
    y_s, k_s, v_s, wkv_s, shift_s, conv_s = _run_trunk(
        x_sample, layers, final_w, state_conv, state_shift, state_wkv, attend_sample, dims)
    return (y_p, y_s, k_p, v_p, k_s, v_s, wkv_p, wkv_s, shift_p, shift_s, conv_p, conv_s)
```

```python
import functools
import math

import jax
import jax.numpy as jnp
from jax import lax
from jax.experimental import pallas as pl
from jax.experimental.pallas import tpu as pltpu

F32 = jnp.float32
BF16 = jnp.bfloat16

NORM_EPS = 1e-6
LNX_EPS = 64e-5
SUBLN_EPS = 1e-5
KK_EPS = 1e-24
NEG_BIG = -1e30
LOG2_E = math.log2(math.e)

V7X_LANES = 128
V7X_SUBLANES = 8
V7X_VMEM_LIMIT_BYTES = 56 * 1024 * 1024

NT_DIMS = (((1,), (1,)), ((), ()))
NN_DIMS = (((1,), (0,)), ((), ()))
TN_DIMS = (((0,), (0,)), ((), ()))


def _params(semantics):
    return pltpu.CompilerParams(dimension_semantics=semantics, vmem_limit_bytes=V7X_VMEM_LIMIT_BYTES)


def _const_spec(shape):
    nd = len(shape)
    return pl.BlockSpec(shape, lambda *_: (0,) * nd, pipeline_mode=pl.Buffered(1))


def _layer_spec(w, layer):
    nd = w.ndim - 1
    return pl.BlockSpec((None,) + w.shape[1:], lambda *_: (layer,) + (0,) * nd, pipeline_mode=pl.Buffered(1))


def _act_dtype(rows):
    return BF16 if rows % 16 == 0 else F32


def _rms(x, w, eps):
    ms = jnp.mean(x * x, axis=-1, keepdims=True)
    return x * lax.rsqrt(ms + eps) * w


def _split_bf16(x, n):
    pieces = []
    rem = x
    for i in range(n):
        p = rem.astype(BF16)
        pieces.append(p)
        if i + 1 < n:
            rem = rem - p.astype(F32)
    return pieces


def _mm(a, b, dims=NN_DIMS, pa=1, pb=1):
    a_p = _split_bf16(a, pa)
    b_p = _split_bf16(b, pb)
    order = max(pa, pb)
    acc = None
    for i, ai in enumerate(a_p):
        for j, bj in enumerate(b_p):
            if i + j >= order:
                continue
            t = lax.dot_general(ai, bj, dims, preferred_element_type=F32)
            acc = t if acc is None else acc + t
    return acc


FFN_CHUNK = 256


def _ffn_body(x_ref, nw_ref, wg_ref, wu_ref, wd_ref, *rest, n_chunks, fc, final):
    if final:
        fw_ref, o_ref, acc_ref = rest
    else:
        o_ref, acc_ref = rest
    x = x_ref[...]
    h = _rms(x, nw_ref[...], NORM_EPS).astype(BF16)
    for c in range(n_chunks):
        cs = slice(c * fc, (c + 1) * fc)
        g = jnp.dot(h, wg_ref[:, cs], preferred_element_type=F32)
        u = jnp.dot(h, wu_ref[:, cs], preferred_element_type=F32)
        act = (g * jax.nn.sigmoid(g) * u).astype(BF16)
        d = jnp.dot(act, wd_ref[cs, :], preferred_element_type=F32)
        if c == 0:
            acc_ref[...] = d
        else:
            acc_ref[...] += d
    y = x + 0.5 * acc_ref[...]
    if final:
        y = _rms(y, fw_ref[...], NORM_EPS)
    o_ref[...] = y


def _ffn(x, norm_w, wg, wu, wd, layer, final_w=None):
    m, d = x.shape
    d_ff = wg.shape[2]
    fc = FFN_CHUNK if d_ff % FFN_CHUNK == 0 else V7X_LANES
    n_chunks = d_ff // fc
    tm = min(m, 512)
    final = final_w is not None
    in_specs = [
        pl.BlockSpec((tm, d), lambda i: (i, 0)),
        _const_spec((1, d)),
        _layer_spec(wg, layer),
        _layer_spec(wu, layer),
        _layer_spec(wd, layer),
    ]
    args = [x, norm_w, wg, wu, wd]
    if final:
        in_specs.append(_const_spec((1, d)))
        args.append(final_w)
    return pl.pallas_call(
        functools.partial(_ffn_body, n_chunks=n_chunks, fc=fc, final=final),
        grid=(m // tm,),
        in_specs=in_specs,
        out_specs=pl.BlockSpec((tm, d), lambda i: (i, 0)),
        out_shape=jax.ShapeDtypeStruct((m, d), F32),
        scratch_shapes=[pltpu.VMEM((tm, d), F32)],
        compiler_params=_params(("parallel",)),
        name="ffn",
    )(*args)


def _mixin_body(x_ref, nw_ref, w_ref, *refs, widths, n_carried, q_index, q_scale, k_index, v_index, v_heads,
                col_tile, tm):
    out_refs = refs[n_carried:]
    vil_ref = out_refs[len(widths)]
    kt_ref = out_refs[len(widths) + 1] if len(out_refs) > len(widths) + 1 else None
    h = _rms(x_ref[...], nw_ref[...], NORM_EPS).astype(BF16)
    start = 0
    for idx, (ref, width) in enumerate(zip(out_refs, widths)):
        for c0 in range(0, width, col_tile):
            wd = min(col_tile, width - c0)
            r = jnp.dot(h, w_ref[:, start + c0:start + c0 + wd], preferred_element_type=F32)
            if idx == q_index:
                r = r * q_scale
            ref[:, c0:c0 + wd] = r.astype(ref.dtype)
            if idx == v_index:
                hw = width // v_heads
                for hh in range(c0 // hw, (c0 + wd) // hw):
                    vil_ref[pl.ds(hh, tm, stride=v_heads), :] = r[:, hh * hw - c0:(hh + 1) * hw - c0]
            if idx == k_index and kt_ref is not None:
                kt_ref[c0:c0 + wd, :] = r.T
        start += width


def _mixin(x, norm_w, w_in, layer, widths, q_index, q_scale, k_index, v_index, v_heads, seq_len, carried, bf16_outs):
    m, d = x.shape
    depth = w_in.shape[0]
    tm = min(m, 512)
    hw = widths[v_index] // v_heads
    kw = widths[k_index]
    emit_kt = seq_len % tm == 0
    tiles_per_seq = seq_len // tm if emit_kt else 1
    stack_specs = [pl.BlockSpec((None, tm * v_heads, hw), lambda i: (layer, i, 0))]
    stack_shapes = [jax.ShapeDtypeStruct((depth, m * v_heads, hw), F32)]
    if emit_kt:
        stack_specs.append(pl.BlockSpec((None, None, kw, tm),
                                        lambda i: (layer, i // tiles_per_seq, 0, i % tiles_per_seq)))
        stack_shapes.append(jax.ShapeDtypeStruct((depth, m // seq_len, kw, seq_len), F32))
    carried = list(carried) if carried is not None else []
    n_in = 3
    return pl.pallas_call(
        functools.partial(_mixin_body, widths=widths, n_carried=len(carried), q_index=q_index, q_scale=q_scale,
                          k_index=k_index, v_index=v_index, v_heads=v_heads, col_tile=512, tm=tm),
        grid=(m // tm,),
        in_specs=[pl.BlockSpec((tm, d), lambda i: (i, 0)), _const_spec((1, d)), _layer_spec(w_in, layer)]
                 + [pl.BlockSpec(memory_space=pl.ANY)] * len(carried),
        out_specs=[pl.BlockSpec((tm, w), lambda i: (i, 0)) for w in widths] + stack_specs,
        out_shape=[jax.ShapeDtypeStruct((m, w), _act_dtype(tm) if i in bf16_outs else F32)
                   for i, w in enumerate(widths)] + stack_shapes,
        input_output_aliases={n_in + j: len(widths) + j for j in range(len(carried))},
        compiler_params=_params(("parallel",)),
        name="mixin",
    )(x, norm_w, w_in, *carried)


SHORT_SEQ_BATCH = 8


def _seq_batch(b, t):
    return SHORT_SEQ_BATCH if (t <= 64 and b % SHORT_SEQ_BATCH == 0) else 1


def _conv_body(cin_ref, st_ref, cw_ref, ya_ref, cn_ref, buf, *, tt, da, bb):
    first = pl.program_id(1) == 0
    cw = cw_ref[...]
    for bi in range(bb):
        @pl.when(first)
        def _(bi=bi):
            buf[bi, pl.ds(6, 2), :] = st_ref[bi]

        c = cin_ref[bi].astype(F32)
        xin = c[:, 0:da]
        gb = c[:, da:2 * da]
        gc = c[:, 2 * da:3 * da]
        u = gc * xin
        buf[bi, pl.ds(8, tt), :] = u
        y = buf[bi, pl.ds(6, tt), :] * cw[0:1, :]
        y = y + buf[bi, pl.ds(7, tt), :] * cw[1:2, :]
        y = y + u * cw[2:3, :]
        ya_ref[bi] = (gb * y).astype(ya_ref.dtype)
        tail = buf[bi, pl.ds(tt + 6, 2), :]
        cn_ref[bi] = tail
        buf[bi, pl.ds(6, 2), :] = tail


def _short_conv(cin, conv_state, conv_w):
    b, t, w3 = cin.shape
    da = w3 // 3
    tt = min(t, 512)
    bb = _seq_batch(b, t)
    return pl.pallas_call(
        functools.partial(_conv_body, tt=tt, da=da, bb=bb),
        grid=(b // bb, t // tt),
        in_specs=[
            pl.BlockSpec((bb, tt, w3), lambda i, j: (i, j, 0)),
            pl.BlockSpec((bb, 2, da), lambda i, j: (i, 0, 0)),
            pl.BlockSpec((3, da), lambda i, j: (0, 0)),
        ],
        out_specs=[
            pl.BlockSpec((bb, tt, da), lambda i, j: (i, j, 0)),
            pl.BlockSpec((bb, 2, da), lambda i, j: (i, 0, 0)),
        ],
        out_shape=[jax.ShapeDtypeStruct((b, t, da), _act_dtype(tt)), jax.ShapeDtypeStruct((b, 2, da), F32)],
        scratch_shapes=[pltpu.VMEM((bb, tt + 8, da), F32)],
        compiler_params=_params(("parallel", "arbitrary")),
        name="short_conv",
    )(cin, conv_state, conv_w)


def _pre_body(z_ref, ss_ref, mu_ref, wpad_ref, w0_ref, apad_ref, a0_ref, gup_ref, kk_ref, ka_ref, ones_ref,
              r_ref, lw_ref, k_ref, v_ref, a_ref, b_ref, g_ref, sn_ref, buf, *, tt, db, r_lo, r_g, bb):
    first = pl.program_id(1) == 0
    zms = []
    for bi in range(bb):
        @pl.when(first)
        def _(bi=bi):
            buf[bi, pl.ds(7, 1), :] = ss_ref[bi]

        z = z_ref[bi]
        buf[bi, pl.ds(8, tt), :] = z
        zp = buf[bi, pl.ds(7, tt), :]
        zms.append(z + (zp - z) * mu_ref[...])
        last = buf[bi, pl.ds(tt + 7, 1), :]
        sn_ref[bi] = last
        buf[bi, pl.ds(7, 1), :] = last
    zm = zms[0] if bb == 1 else jnp.concatenate(zms, axis=0)

    r = zm[:, 0:db]
    k = zm[:, db:2 * db]
    v = zm[:, 2 * db:3 * db]
    lo = zm[:, 3 * db:3 * db + r_lo]
    g_lo = zm[:, 3 * db + r_lo:3 * db + r_lo + r_g]
    w_part = jnp.dot(jnp.tanh(lo).astype(BF16), wpad_ref[...], preferred_element_type=F32)
    a_part = jnp.dot(lo.astype(BF16), apad_ref[...], preferred_element_type=F32)
    lw = -math.exp(-0.5) * jax.nn.sigmoid(w0_ref[...] + w_part)
    a = jax.nn.sigmoid(a0_ref[...] + a_part)
    g = jnp.dot(jax.nn.sigmoid(g_lo).astype(BF16), gup_ref[...], preferred_element_type=F32)
    kk = k * kk_ref[...]
    ss = _mm(kk * kk, ones_ref[...].astype(F32), pa=2, pb=1)
    kkn = kk * lax.rsqrt(jnp.maximum(ss, KK_EPS))
    k_mod = k * (1.0 + (a - 1.0) * ka_ref[...])
    for bi in range(bb):
        rows = slice(bi * tt, (bi + 1) * tt)
        r_ref[bi] = r[rows].astype(r_ref.dtype)
        lw_ref[bi] = lw[rows]
        k_ref[bi] = k_mod[rows].astype(k_ref.dtype)
        v_ref[bi] = v[rows].astype(v_ref.dtype)
        a_ref[bi] = (-kkn[rows]).astype(a_ref.dtype)
        b_ref[bi] = (kkn * a)[rows].astype(b_ref.dtype)
        g_ref[bi] = g[rows].astype(g_ref.dtype)


def _rwkv_pre(z, shift_state, mu, wpad, w0, apad, a0, gup, k_k, k_a, ones_bd, db):
    b, t, ns = z.shape
    r_lo = wpad.shape[0]
    r_g = gup.shape[0]
    tt = min(t, 512)
    bb = _seq_batch(b, t)
    row = lambda n: pl.BlockSpec((1, n), lambda i, j: (0, 0))
    full = lambda s: pl.BlockSpec(s, lambda i, j: (0, 0))
    seq = lambda n: pl.BlockSpec((bb, tt, n), lambda i, j: (i, j, 0))
    return pl.pallas_call(
        functools.partial(_pre_body, tt=tt, db=db, r_lo=r_lo, r_g=r_g, bb=bb),
        grid=(b // bb, t // tt),
        in_specs=[
            seq(ns),
            pl.BlockSpec((bb, 1, ns), lambda i, j: (i, 0, 0)),
            row(ns), full((r_lo, db)), row(db), full((r_lo, db)), row(db), full((r_g, db)),
            row(db), row(db), full((db, db)),
        ],
        out_specs=[seq(db)] * 7 + [pl.BlockSpec((bb, 1, ns), lambda i, j: (i, 0, 0))],
        out_shape=[jax.ShapeDtypeStruct((b, t, db), F32 if i == 1 else _act_dtype(tt)) for i in range(7)]
                  + [jax.ShapeDtypeStruct((b, 1, ns), F32)],
        scratch_shapes=[pltpu.VMEM((bb, tt + 8, ns), F32)],
        compiler_params=_params(("parallel", "arbitrary")),
        name="rwkv_pre",
    )(z, shift_state, mu, wpad, w0, apad, a0, gup, k_k, k_a, ones_bd)


SCAN_SUB = 16
SCAN_P = 1


def _mm_each(a_list, b_list, dims=NN_DIMS):
    return [_mm(a, b, dims=dims, pa=SCAN_P, pb=SCAN_P) for a, b in zip(a_list, b_list)]


def _neumann_inverse_each(l_list, n):
    c = l_list[0].shape[0]
    eye = (lax.broadcasted_iota(jnp.int32, (c, c), 0) == lax.broadcasted_iota(jnp.int32, (c, c), 1)).astype(F32)
    x = [eye + l for l in l_list]
    p = l_list
    span = 2
    while span < n:
        p = _mm_each(p, p)
        xp = _mm_each(x, p)
        x = [xi + d for xi, d in zip(x, xp)]
        span *= 2
    return x


def _unit_lower_inverse_each(l_list):
    c = l_list[0].shape[0]
    if c <= SCAN_SUB:
        return _neumann_inverse_each(l_list, c)
    rows = lax.broadcasted_iota(jnp.int32, (c, c), 0) // SCAN_SUB
    cols = lax.broadcasted_iota(jnp.int32, (c, c), 1) // SCAN_SUB
    same = rows == cols
    t_diag = _neumann_inverse_each([jnp.where(same, l, 0.0) for l in l_list], SCAN_SUB)
    z = _mm_each(t_diag, [jnp.where(same, 0.0, l) for l in l_list])
    nblk = c // SCAN_SUB
    factors = []
    span = 1
    while span < nblk:
        factors.append(z)
        span *= 2
        if span < nblk:
            z = _mm_each(z, z)
    out = t_diag
    for f in reversed(factors):
        d = _mm_each(f, out)
        out = [o + di for o, di in zip(out, d)]
    return out


def _scan_body(r_ref, lw_ref, k_ref, v_ref, a_ref, b_ref, g_ref, s0_ref, rk_ref, lnw_ref, lnb_ref,
               o_ref, sT_ref, st_ref, *, c, bb, nh, hd):
    ci = pl.program_id(1)
    units = [(bi, h) for bi in range(bb) for h in range(nh)]

    @pl.when(ci == 0)
    def _():
        for bi, h in units:
            st_ref[bi, h] = s0_ref[bi, h].T

    rows = lax.broadcasted_iota(jnp.int32, (c, c), 0)
    cols = lax.broadcasted_iota(jnp.int32, (c, c), 1)
    tri = (rows >= cols).astype(F32)
    rows2 = lax.broadcasted_iota(jnp.int32, (c, 2 * c), 0)
    cols2 = lax.broadcasted_iota(jnp.int32, (c, 2 * c), 1)
    cols2 = jnp.where(cols2 >= c, cols2 - c, cols2)
    strict2 = rows2 > cols2
    incl2 = rows2 >= cols2
    ones_cols = jnp.ones((c, hd), F32)
    zeros_cv = jnp.zeros((c, hd), F32)

    ar, bk, bk_rem, v_u, decay = [], [], [], [], []
    for bi in range(bb):
        r = r_ref[bi].astype(F32)
        lw = lw_ref[bi]
        k = k_ref[bi].astype(F32)
        a = a_ref[bi].astype(F32)
        b = b_ref[bi].astype(F32)
        v = v_ref[bi].astype(F32)
        cum = _mm(tri, lw, pa=1, pb=3)
        cum_last = cum[c - 1:c, :]
        e_neg = jnp.exp(-cum)
        e_rem = jnp.exp(cum_last - cum)
        at = a * jnp.exp(cum - lw)
        rt = r * jnp.exp(cum)
        bt = b * e_neg
        kt = k * e_neg
        bh = b * e_rem
        kh = k * e_rem
        cum_cols = _mm(lw, ones_cols, dims=TN_DIMS, pa=3, pb=1)
        for h in range(nh):
            sl = slice(h * hd, (h + 1) * hd)
            ar.append(jnp.concatenate([at[:, sl], rt[:, sl]], axis=0))
            bk.append(jnp.concatenate([bt[:, sl], kt[:, sl]], axis=0))
            bk_rem.append(jnp.concatenate([bh[:, sl], kh[:, sl]], axis=0))
            v_u.append(v[:, sl])
            decay.append(jnp.exp(cum_cols[h * hd:(h + 1) * hd, :]))

    st = [st_ref[bi, h] for bi, h in units]
    m4 = _mm_each(ar, bk, dims=NT_DIMS)
    m_a = [jnp.where(strict2, m[0:c], 0.0) for m in m4]
    m_r = [jnp.where(incl2, m[c:2 * c], 0.0) for m in m4]
    t_inv = _unit_lower_inverse_each([m[:, 0:c] for m in m_a])
    g1 = _mm_each([x[0:c] for x in ar], st)
    g2 = _mm_each(m_a, [jnp.concatenate([zeros_cv, vh], axis=0) for vh in v_u])
    u = _mm_each(t_inv, [x + y for x, y in zip(g1, g2)])
    uv = [jnp.concatenate([ui, vh], axis=0) for ui, vh in zip(u, v_u)]
    y1 = _mm_each([x[c:2 * c] for x in ar], st)
    y2 = _mm_each(m_r, uv)
    st_upd = _mm_each(bk_rem, uv, dims=TN_DIMS)
    for (bi, h), s_old, d, upd in zip(units, st, decay, st_upd):
        st_ref[bi, h] = s_old * d + upd

    rk = rk_ref[...]
    for bi in range(bb):
        r = r_ref[bi].astype(F32)
        k = k_ref[bi].astype(F32)
        rkk = r * k * rk
        yn_parts, bonus_parts = [], []
        for h in range(nh):
            i = bi * nh + h
            sl = slice(h * hd, (h + 1) * hd)
            y = y1[i] + y2[i]
            mean = jnp.mean(y, axis=-1, keepdims=True)
            yc = y - mean
            var = jnp.mean(yc * yc, axis=-1, keepdims=True)
            yn_parts.append(yc * lax.rsqrt(var + LNX_EPS))
            bonus_parts.append(jnp.sum(rkk[:, sl], axis=-1, keepdims=True) * v_u[i])
        yn_all = jnp.concatenate(yn_parts, axis=1)
        bonus_all = jnp.concatenate(bonus_parts, axis=1)
        o_ref[bi] = ((yn_all * lnw_ref[...] + lnb_ref[...] + bonus_all) * g_ref[bi].astype(F32)).astype(o_ref.dtype)

    @pl.when(ci == pl.num_programs(1) - 1)
    def _():
        for bi, h in units:
            sT_ref[bi, h] = st_ref[bi, h].T


def _rwkv_scan(r, lw, k, v, a, b, g, s0, r_k, lnx_w, lnx_b):
    bsz, t, db = r.shape
    _, nh, hd, _ = s0.shape
    c = min(t, 64)
    bb = 4 if bsz % 4 == 0 else 1
    seq = pl.BlockSpec((bb, c, db), lambda i, j: (i, j, 0))
    row = pl.BlockSpec((1, db), lambda i, j: (0, 0))
    st_spec = pl.BlockSpec((bb, nh, hd, hd), lambda i, j: (i, 0, 0, 0))
    return pl.pallas_call(
        functools.partial(_scan_body, c=c, bb=bb, nh=nh, hd=hd),
        grid=(bsz // bb, t // c),
        in_specs=[seq] * 7 + [st_spec, row, row, row],
        out_specs=[seq, st_spec],
        out_shape=[jax.ShapeDtypeStruct((bsz, t, db), _act_dtype(c)), jax.ShapeDtypeStruct((bsz, nh, hd, hd), F32)],
        scratch_shapes=[pltpu.VMEM((bb, nh, hd, hd), F32)],
        compiler_params=_params(("parallel", "arbitrary")),
        name="rwkv_scan",
    )(r, lw, k, v, a, b, g, s0, r_k, lnx_w, lnx_b)


def _lam_value(lq1, lk1, lq2, lk2, lam_init):
    s1 = jnp.sum(lq1 * lk1, axis=-1, keepdims=True)
    s2 = jnp.sum(lq2 * lk2, axis=-1, keepdims=True)
    return jnp.exp(s1) - jnp.exp(s2) + lam_init


def _stack_maps(q, tn, dh):
    q2x = jnp.concatenate([q, q], axis=0)
    row = lax.broadcasted_iota(jnp.int32, q2x.shape, 0)
    lane = lax.broadcasted_iota(jnp.int32, q2x.shape, 1)
    keep = (row < tn) == (lane < dh)
    return jnp.where(keep, q2x, 0.0).astype(BF16)


def _lane_tile(x, width):
    reps = width // V7X_LANES
    return x if reps == 1 else jnp.concatenate([x] * reps, axis=1)


def _diff_finish(o12, tn, lam, subln_w, lam_init):
    o = o12[0:tn] - lam * o12[tn:2 * tn]
    return _rms(o, subln_w, SUBLN_EPS) * (1.0 - lam_init)


PATTN_COLS = 128
PATTN_KEYS = 256
PATTN_HEADS = 4
BIAS_PIECES = 3


def _pattn_body(qi_ref, ki_ref, q_ref, k_ref, v_ref, slope_ref, lq1_ref, lk1_ref, lq2_ref, lk2_ref, sw_ref,
                o_ref, q12_s, ktab_s, s_s, p_s, m_s, l_s, al_s, acc_s, *, tq, dh, hps, n_heads, pk, lam_init):
    step = pl.program_id(2)
    qi = qi_ref[step]
    ki = ki_ref[step]
    nq2 = 2 * tq
    hw = 2 * dh
    heads = range(hps)

    @pl.when(ki == 0)
    def _():
        lane = lax.broadcasted_iota(jnp.int32, (nq2, V7X_LANES), 1)
        ones = jnp.where(lane < BIAS_PIECES, 1.0, 0.0).astype(BF16)
        klane = lax.broadcasted_iota(jnp.int32, (tq, V7X_LANES), 1)
        kpos = lax.broadcasted_iota(jnp.int32, (tq, V7X_LANES), 0).astype(F32)
        for g in heads:
            q12_s[g] = jnp.concatenate([_stack_maps(q_ref[0][:, g * hw:(g + 1) * hw], tq, dh), ones], axis=1)
            rem = slope_ref[g][:, 0:1] * kpos
            tab = jnp.zeros((tq, V7X_LANES), F32)
            for i in range(BIAS_PIECES):
                piece = rem.astype(BF16).astype(F32)
                tab = jnp.where(klane == i, piece, tab)
                rem = rem - piece
            ktab_s[g] = tab.astype(BF16)
        m_s[...] = jnp.full(m_s.shape, NEG_BIG, F32)
        l_s[...] = jnp.zeros(l_s.shape, F32)
        acc_s[...] = jnp.zeros(acc_s.shape, F32)

    def update(diagonal):
        k_all = k_ref[0]
        v_all = v_ref[0]
        vt_bf = []
        for g in heads:
            k_aug = jnp.concatenate([k_all[:, g * hw:(g + 1) * hw].astype(BF16), ktab_s[g]], axis=1)
            s_s[g] = lax.dot_general(k_aug, q12_s[g], NT_DIMS, preferred_element_type=F32)
            vt_bf.append(v_all[:, g * hw:(g + 1) * hw].T.astype(BF16))
        base = ((ki - qi) * tq).astype(F32)
        for k0 in range(0, tq, pk):
            for g in heads:
                shift = slope_ref[g][:, 0:1] * base
                for c0 in range(0, nq2, PATTN_COLS):
                    q0 = c0 % tq
                    cols = pl.ds(c0, PATTN_COLS)
                    if diagonal and k0 > q0 + PATTN_COLS - 1:
                        p_s[g, :, cols] = jnp.zeros((pk, PATTN_COLS), BF16)
                        al_s[g, :, cols] = jnp.ones((1, PATTN_COLS), F32)
                        continue
                    sv = s_s[g, pl.ds(k0, pk), cols]
                    if diagonal and k0 + pk - 1 > q0:
                        qpos = q0 + lax.broadcasted_iota(jnp.int32, (pk, PATTN_COLS), 1)
                        kpos = k0 + lax.broadcasted_iota(jnp.int32, (pk, PATTN_COLS), 0)
                        sv = jnp.where(qpos >= kpos, sv, NEG_BIG)
                    m_old = m_s[g, :, cols]
                    m_new = jnp.maximum(m_old, jnp.max(sv, axis=0, keepdims=True) + shift)
                    alpha = jnp.exp2(m_old - m_new)
                    p = jnp.exp2(sv - (m_new - shift))
                    m_s[g, :, cols] = m_new
                    l_s[g, :, cols] = alpha * l_s[g, :, cols] + jnp.sum(p, axis=0, keepdims=True)
                    al_s[g, :, cols] = alpha
                    p_s[g, :, cols] = p.astype(BF16)
            for g in heads:
                acc_s[g] = al_s[g] * acc_s[g] + jnp.dot(vt_bf[g][:, k0:k0 + pk], p_s[g],
                                                        preferred_element_type=F32)

    @pl.when(ki < qi)
    def _():
        update(False)

    @pl.when(ki == qi)
    def _():
        update(True)
        lam = _lam_value(lq1_ref[...], lk1_ref[...], lq2_ref[...], lk2_ref[...], lam_init)
        outs = []
        for g in heads:
            o12 = (acc_s[g] / l_s[g]).T
            outs.append(_diff_finish(o12, tq, lam, sw_ref[...], lam_init))
        o_ref[0] = (outs[0] if hps == 1 else jnp.concatenate(outs, axis=1)).astype(o_ref.dtype)


def _prompt_attention(q, k, v, slopes, lq1, lk1, lq2, lk2, subln_w, lam_init, n_heads):
    b, t, w = q.shape
    hw = w // n_heads
    assert hw == V7X_LANES, "one head's two maps fill exactly one lane tile"
    dh = hw // 2
    tq = min(t, 512)
    nq = t // tq
    hps = math.gcd(n_heads, PATTN_HEADS)
    pk = min(PATTN_KEYS, tq)
    pairs = [(i, j) for i in range(nq) for j in range(i + 1)]
    qi_tbl = jnp.asarray([p[0] for p in pairs], jnp.int32)
    ki_tbl = jnp.asarray([p[1] for p in pairs], jnp.int32)
    qspec = pl.BlockSpec((1, tq, hps * hw), lambda bi, h, s, qt, kt: (bi, qt[s], h))
    kspec = pl.BlockSpec((1, tq, hps * hw), lambda bi, h, s, qt, kt: (bi, kt[s], h))
    small = lambda n: pl.BlockSpec((1, n), lambda bi, h, s, qt, kt: (0, 0))
    row = pltpu.VMEM((hps, 1, 2 * tq), F32)
    grid_spec = pltpu.PrefetchScalarGridSpec(
        num_scalar_prefetch=2,
        grid=(b, n_heads // hps, len(pairs)),
        in_specs=[qspec, kspec, kspec,
                  pl.BlockSpec((hps, 1, V7X_LANES), lambda bi, h, s, qt, kt: (h, 0, 0)),
                  small(dh), small(dh), small(dh), small(dh), small(hw)],
        out_specs=qspec,
        scratch_shapes=[pltpu.VMEM((hps, 2 * tq, 2 * hw), BF16), pltpu.VMEM((hps, tq, V7X_LANES), BF16),
                        pltpu.VMEM((hps, tq, 2 * tq), F32),
                        pltpu.VMEM((hps, pk, 2 * tq), BF16), row, row, row,
                        pltpu.VMEM((hps, hw, 2 * tq), F32)],
    )
    return pl.pallas_call(
        functools.partial(_pattn_body, tq=tq, dh=dh, hps=hps, n_heads=n_heads, pk=pk, lam_init=lam_init),
        grid_spec=grid_spec,
        out_shape=jax.ShapeDtypeStruct((b, t, w), _act_dtype(tq)),
        compiler_params=_params(("parallel", "parallel", "arbitrary")),
        name="prompt_attention",
    )(qi_tbl, ki_tbl, q, k, v, slopes, lq1, lk1, lq2, lk2, subln_w)


SATTN_PAGES = 32


def _sattn_body(pt_ref, q_ref, kn_ref, vn_ref, slope_ref, lq1_ref, lk1_ref, lq2_ref, lk2_ref, sw_ref, *rest,
                ppb, page, n_heads, tn, dh, past_len, lam_init):
    k_refs = rest[:ppb]
    v_refs = rest[ppb:2 * ppb]
    o_ref, m_s, l_s, acc_s = rest[2 * ppb:]
    del pt_ref
    si = pl.program_id(1)
    hw = 2 * dh
    width = ppb * page
    heads = range(n_heads)

    @pl.when(si == 0)
    def _():
        m_s[...] = jnp.full(m_s.shape, NEG_BIG, F32)
        l_s[...] = jnp.zeros(l_s.shape, F32)
        acc_s[...] = jnp.zeros(acc_s.shape, F32)

    q = q_ref[0]
    qm = [q[:, m * dh:(m + 1) * dh].astype(BF16) for m in range(2 * n_heads)]

    def scores(h, key_of_map, dims):
        sa = lax.dot_general(qm[2 * h], key_of_map(2 * h), dims, preferred_element_type=F32)
        sb = lax.dot_general(qm[2 * h + 1], key_of_map(2 * h + 1), dims, preferred_element_type=F32)
        return jnp.concatenate([sa, sb], axis=0)

    kpos = si * width + lax.broadcasted_iota(jnp.int32, (1, width), 1) - past_len
    kpos = kpos.astype(F32)
    s_all = []
    for h in heads:
        parts = [scores(h, lambda m, j=j: k_refs[j][m].astype(BF16), NN_DIMS) for j in range(ppb)]
        s_all.append(jnp.concatenate(parts, axis=1) + slope_ref[h][:, 0:1] * kpos)
    alphas, ps = [], []
    for h in heads:
        m_old = m_s[h]
        m_new = jnp.maximum(m_old, jnp.max(s_all[h], axis=-1, keepdims=True))
        alpha = jnp.exp2(m_old - m_new)
        p = jnp.exp2(s_all[h] - _lane_tile(m_new, width))
        l_s[h] = alpha * l_s[h] + jnp.sum(p, axis=-1, keepdims=True)
        m_s[h] = m_new
        alphas.append(alpha)
        ps.append(p.astype(BF16))
    for h in heads:
        pv = None
        for j in range(ppb):
            t = jnp.dot(ps[h][:, j * page:(j + 1) * page],
                        v_refs[j][pl.ds(h, page, stride=n_heads), :].astype(BF16),
                        preferred_element_type=F32)
            pv = t if pv is None else pv + t
        acc_s[h] = alphas[h] * acc_s[h] + pv

    @pl.when(si == pl.num_programs(1) - 1)
    def _():
        lam = _lam_value(lq1_ref[...], lk1_ref[...], lq2_ref[...], lk2_ref[...], lam_init)
        kn = kn_ref[0]
        vn = vn_ref[0]
        row_n = lax.broadcasted_iota(jnp.int32, (2 * tn, tn), 0)
        col_n = lax.broadcasted_iota(jnp.int32, (2 * tn, tn), 1)
        causal = jnp.where(row_n >= tn, row_n - tn, row_n) >= col_n
        outs = []
        for h in heads:
            s = scores(h, lambda m: kn[:, m * dh:(m + 1) * dh].astype(BF16), NT_DIMS)
            s = jnp.where(causal, s + slope_ref[h][:, 0:1] * col_n.astype(F32), NEG_BIG)
            m_old = m_s[h]
            m_new = jnp.maximum(m_old, jnp.max(s, axis=-1, keepdims=True))
            alpha = jnp.exp2(m_old - m_new)
            p = jnp.exp2(s - m_new[:, 0:tn])
            l_fin = alpha * l_s[h] + jnp.sum(p, axis=-1, keepdims=True)
            acc = alpha * acc_s[h] + jnp.dot(p.astype(BF16), vn[:, h * hw:(h + 1) * hw].astype(BF16),
                                             preferred_element_type=F32)
            outs.append(_diff_finish(acc / l_fin, tn, lam, sw_ref[...], lam_init))
        o_ref[0] = jnp.concatenate(outs, axis=1)


def _sample_attention(q, k_new, v_new, cache_kt, cache_v, page_table, layer, slopes, lq1, lk1, lq2, lk2, subln_w,
                      lam_init):
    b, tn, w = q.shape
    _, _, maps, dh, page = cache_kt.shape
    n_heads = maps // 2
    hw = 2 * dh
    assert hw == V7X_LANES, "softmax statistics are kept lane-replicated at the head width"
    n_pages = page_table.shape[1]
    ppb = math.gcd(n_pages, SATTN_PAGES)
    n_steps = n_pages // ppb
    pt_flat = page_table.reshape(-1)

    def page_index(j, nd):
        return lambda bi, si, pt: (layer, pt[bi * n_pages + si * ppb + j]) + (0,) * nd

    k_specs = [pl.BlockSpec((None, None, maps, dh, page), page_index(j, 3)) for j in range(ppb)]
    v_specs = [pl.BlockSpec((None, None, page * n_heads, hw), page_index(j, 2)) for j in range(ppb)]
    new_spec = pl.BlockSpec((1, tn, w), lambda bi, si, pt: (bi, 0, 0))
    small = lambda n: pl.BlockSpec((1, n), lambda bi, si, pt: (0, 0))
    grid_spec = pltpu.PrefetchScalarGridSpec(
        num_scalar_prefetch=1,
        grid=(b, n_steps),
        in_specs=[new_spec, new_spec, new_spec,
                  pl.BlockSpec((n_heads, 1, V7X_LANES), lambda bi, si, pt: (0, 0, 0)),
                  small(dh), small(dh), small(dh), small(dh), small(hw)] + k_specs + v_specs,
        out_specs=new_spec,
        scratch_shapes=[pltpu.VMEM((n_heads, 2 * tn, hw), F32), pltpu.VMEM((n_heads, 2 * tn, hw), F32),
                        pltpu.VMEM((n_heads, 2 * tn, hw), F32)],
    )
    return pl.pallas_call(
        functools.partial(_sattn_body, ppb=ppb, page=page, n_heads=n_heads, tn=tn, dh=dh,
                          past_len=n_pages * page, lam_init=lam_init),
        grid_spec=grid_spec,
        out_shape=jax.ShapeDtypeStruct((b, tn, w), F32),
        compiler_params=_params(("parallel", "arbitrary")),
        name="sample_attention",
    )(pt_flat, q, k_new, v_new, slopes, lq1, lk1, lq2, lk2, subln_w,
      *([cache_kt] * ppb), *([cache_v] * ppb))


def _merge_body(x_ref, ya_ref, yb_ref, yc_ref, gt_ref, wa_ref, wb_ref, wc_ref, wo_ref, o_ref, *, d):
    gates = gt_ref[...].astype(F32)
    merged = jax.nn.sigmoid(gates[:, 0:d]) * jnp.dot(ya_ref[...].astype(BF16), wa_ref[...], preferred_element_type=F32)
    merged = merged + jax.nn.sigmoid(gates[:, d:2 * d]) * jnp.dot(
        yb_ref[...].astype(BF16), wb_ref[...], preferred_element_type=F32)
    merged = merged + jax.nn.sigmoid(gates[:, 2 * d:3 * d]) * jnp.dot(
        yc_ref[...].astype(BF16), wc_ref[...], preferred_element_type=F32)
    o_ref[...] = x_ref[...] + jnp.dot(merged.astype(BF16), wo_ref[...], preferred_element_type=F32)


def _merge(x, ya, yb, yc, gates, wa, wb, wc, wo, layer):
    m, d = x.shape
    tm = min(m, 512)
    tile = lambda n: pl.BlockSpec((tm, n), lambda i: (i, 0))
    return pl.pallas_call(
        functools.partial(_merge_body, d=d),
        grid=(m // tm,),
        in_specs=[tile(d), tile(ya.shape[1]), tile(yb.shape[1]), tile(yc.shape[1]), tile(3 * d),
                  _layer_spec(wa, layer), _layer_spec(wb, layer), _layer_spec(wc, layer), _layer_spec(wo, layer)],
        out_specs=tile(d),
        out_shape=jax.ShapeDtypeStruct((m, d), F32),
        compiler_params=_params(("parallel",)),
        name="merge",
    )(x, ya, yb, yc, gates, wa, wb, wc, wo)


STACKED_BF16 = ("ffn1_w_gate", "ffn1_w_up", "ffn1_w_down", "ffn2_w_gate", "ffn2_w_up", "ffn2_w_down", "w_in",
                "w_branch_a", "w_branch_b", "w_branch_c", "w_out")


def _prep_layer_weights(l, W, big, dims):
    db, r_w, r_a = dims["db"], dims["r_w"], dims["r_a"]
    row = lambda a: a.reshape(1, -1)

    def ffn(prefix):
        return (row(W[prefix + "_norm"][l]), big[prefix + "_w_gate"], big[prefix + "_w_up"], big[prefix + "_w_down"])

    zeros_w = jnp.zeros((r_a, db), BF16)
    zeros_a = jnp.zeros((r_w, db), BF16)
    head = jnp.arange(db) // dims["hd"]
    return dict(
        ffn1=ffn("ffn1"), ffn2=ffn("ffn2"),
        mix_norm=row(W["mix_norm"][l]), w_in=big["w_in"],
        conv_w=W["conv_w"][l],
        mu=row(W["tm_mu"][l]),
        wpad=jnp.concatenate([W["w_up_w"][l].astype(BF16), zeros_w], axis=0),
        apad=jnp.concatenate([zeros_a, W["a_up_w"][l].astype(BF16)], axis=0),
        w0=row(W["w0"][l]), a0=row(W["a0"][l]), gup=W["g_up_w"][l].astype(BF16),
        k_k=row(W["k_k"][l]), k_a=row(W["k_a"][l]), r_k=row(W["r_k"][l]),
        lnx_w=row(W["lnx_w"][l]), lnx_b=row(W["lnx_b"][l]),
        ones_bd=(head[:, None] == head[None, :]).astype(BF16),
        lq1=row(W["lam_q1"][l]), lk1=row(W["lam_k1"][l]), lq2=row(W["lam_q2"][l]), lk2=row(W["lam_k2"][l]),
        subln_w=row(W["subln_w"][l]),
        wa=big["w_branch_a"], wb=big["w_branch_b"], wc=big["w_branch_c"], wo=big["w_out"],
    )


def _run_trunk(x, layers, final_w, conv0, shift0, wkv0, attend, dims):
    b, t, d = x.shape
    m = b * t
    da, db, ns = dims["da"], dims["db"], dims["ns"]
    qw, kw, dc = dims["qw"], dims["kw"], dims["dc"]
    widths = (3 * da, ns, qw, kw, dc, 3 * d)
    CONV_OUT, Q_OUT, K_OUT, V_OUT, GATES_OUT = 0, 2, 3, 4, 5
    depth = len(layers)
    xf = x.reshape(m, d)
    k_rows, wkv_out, shift_out, conv_out = [], [], [], []
    stacks = None
    for l, lw in enumerate(layers):
        xf = _ffn(xf, *lw["ffn1"], l)
        cin, z, q, k, v, gates, *stacks = _mixin(xf, lw["mix_norm"], lw["w_in"], l, widths, Q_OUT,
                                                 dims["dh"] ** -0.5 * LOG2_E, K_OUT, V_OUT, dims["hc"], t, stacks,
                                                 bf16_outs=(CONV_OUT, GATES_OUT) if t % 16 == 0 else (GATES_OUT,))
        y_a, conv_new = _short_conv(cin.reshape(b, t, 3 * da), conv0[l], lw["conv_w"])
        r_, lw_, k_, v_, a_, b_, g_, shift_new = _rwkv_pre(
            z.reshape(b, t, ns), shift0[l].reshape(b, 1, ns), lw["mu"], lw["wpad"], lw["w0"], lw["apad"], lw["a0"],
            lw["gup"], lw["k_k"], lw["k_a"], lw["ones_bd"], db)
        y_b, wkv_new = _rwkv_scan(r_, lw_, k_, v_, a_, b_, g_, wkv0[l], lw["r_k"], lw["lnx_w"], lw["lnx_b"])
        lam_init = 0.8 - 0.6 * math.exp(-0.3 * l)
        y_c = attend(l, q.reshape(b, t, qw), k.reshape(b, t, kw), v.reshape(b, t, dc), lw, lam_init)
        xf = _merge(xf, y_a.reshape(m, da), y_b.reshape(m, db), y_c.reshape(m, dc), gates,
                    lw["wa"], lw["wb"], lw["wc"], lw["wo"], l)
        xf = _ffn(xf, *lw["ffn2"], l, final_w=final_w if l == depth - 1 else None)
        k_rows.append(k.reshape(b, t, 2 * dims["hc"], dims["dh"]))
        wkv_out.append(wkv_new)
        shift_out.append(shift_new.reshape(b, ns))
        conv_out.append(conv_new)
    v_all = stacks[0].reshape(depth, b, t, dims["hc"], 2 * dims["dh"])
    if len(stacks) > 1:
        k_all = jnp.transpose(stacks[1].reshape(depth, b, 2 * dims["hc"], dims["dh"], t), (0, 1, 4, 2, 3))
    else:
        k_all = jnp.stack(k_rows)
    return (xf.reshape(b, t, d), k_all, v_all, jnp.stack(wkv_out), jnp.stack(shift_out), jnp.stack(conv_out))


def kernel(x_prompt, x_sample, cache_k, cache_v, state_wkv, state_shift, state_conv, page_table, ffn1_norm, ffn1_w_gate, ffn1_w_up, ffn1_w_down, mix_norm, w_in, conv_w, tm_mu, w_up_w, w0, a_up_w, a0, g_up_w, k_k, k_a, r_k, lnx_w, lnx_b, lam_q1, lam_k1, lam_q2, lam_k2, subln_w, w_branch_a, w_branch_b, w_branch_c, w_out, ffn2_norm, ffn2_w_gate, ffn2_w_up, ffn2_w_down, final_norm):
    W = dict(ffn1_norm=ffn1_norm, ffn1_w_gate=ffn1_w_gate, ffn1_w_up=ffn1_w_up, ffn1_w_down=ffn1_w_down,
             mix_norm=mix_norm, w_in=w_in, conv_w=conv_w, tm_mu=tm_mu, w_up_w=w_up_w, w0=w0,
             a_up_w=a_up_w, a0=a0, g_up_w=g_up_w, k_k=k_k, k_a=k_a, r_k=r_k, lnx_w=lnx_w, lnx_b=lnx_b,
             lam_q1=lam_q1, lam_k1=lam_k1, lam_q2=lam_q2, lam_k2=lam_k2, subln_w=subln_w,
             w_branch_a=w_branch_a, w_branch_b=w_branch_b, w_branch_c=w_branch_c, w_out=w_out,
             ffn2_norm=ffn2_norm, ffn2_w_gate=ffn2_w_gate, ffn2_w_up=ffn2_w_up, ffn2_w_down=ffn2_w_down)
    depth = w_in.shape[0]
    d = x_prompt.shape[-1]
    _, n_pool, page, maps, dh = cache_k.shape
    hc = maps // 2
    nh, hd = r_k.shape[1], r_k.shape[2]
    db = nh * hd
    da = conv_w.shape[-1]
    assert conv_w.shape[1] == 3, "short conv kernel is written for width 3"
    r_w, r_a, r_g = w_up_w.shape[1], a_up_w.shape[1], g_up_w.shape[1]
    ns = state_shift.shape[-1]
    assert ns == 3 * db + r_w + r_a + r_g
    assert (r_w + r_a) % V7X_LANES == 0 and r_g % V7X_LANES == 0
    dims = dict(d=d, d_ff=ffn1_w_gate.shape[-1], da=da, db=db, hd=hd, ns=ns, r_w=r_w, r_a=r_a,
                qw=maps * dh, kw=maps * dh, dc=hc * 2 * dh, hc=hc, dh=dh)
    big = {name: W[name].astype(BF16) for name in STACKED_BF16}
    layers = [_prep_layer_weights(l, W, big, dims) for l in range(depth)]
    final_w = final_norm.reshape(1, d)
    slopes = jnp.broadcast_to(
        jnp.asarray([LOG2_E * 2.0 ** (-8.0 * (h + 1) / hc) for h in range(hc)], F32)[:, None, None],
        (hc, 1, V7X_LANES))
    cache_kt = jnp.transpose(cache_k, (0, 1, 3, 4, 2))
    cache_v2 = cache_v.reshape(depth, n_pool, page * hc, 2 * dh)

    def attend_prompt(l, q, k, v, lw, lam_init):
        return _prompt_attention(q, k, v, slopes, lw["lq1"], lw["lk1"], lw["lq2"], lw["lk2"], lw["subln_w"],
                                 lam_init, hc)

    def attend_sample(l, q, k, v, lw, lam_init):
        return _sample_attention(q, k, v, cache_kt, cache_v2, page_table, l, slopes, lw["lq1"], lw["lk1"],
                                 lw["lq2"], lw["lk2"], lw["subln_w"], lam_init)

    bp = x_prompt.shape[0]
    conv0 = jnp.zeros((depth, bp, 2, da), F32)
    shift0 = jnp.zeros((depth, bp, ns), F32)
    wkv0 = jnp.zeros((depth, bp, nh, hd, hd), F32)
    y_p, k_p, v_p, wkv_p, shift_p, conv_p = _run_trunk(
        x_prompt, layers, final_w, conv0, shift0, wkv0, attend_prompt, dims)
    y_s, k_s, v_s, wkv_s, shift_s, conv_s = _run_trunk(
        x_sample, layers, final_w, state_conv, state_shift, state_wkv, attend_sample, dims)
    return (y_p, y_s, k_p, v_p, k_s, v_s, wkv_p, wkv_s, shift_p, shift_s, conv_p, conv_s)
```
